```python
import math
import jax
import jax.numpy as jnp
from jax import lax
import numpy as np

D_MODEL = 1024
BATCH = 1
SEQ = 16384
DEPTH = 2

GRID_W = 64
CTX_LEN = 256
N_BRANCH = 4
BRANCH_W = 512
CONV_W = 4
EPS = 1e-6

LRU_W = 512
LRU_BLOCKS = 8
LRU_C = 8.0

RET_H = 4
RET_DK = 64
RET_DV = 128
RET_CHUNK = 128
ROPE_BASE = 10000.0

GDN_H = 4
GDN_DK = 128
GDN_DV = 128
GDN_CHUNK = 64
GDN_CONV_CH = 2 * GDN_H * GDN_DK + GDN_H * GDN_DV

SSD_H = 8
SSD_P = 64
SSD_G = 2
SSD_N = 64
SSD_CHUNK = 128
SSD_CONV_CH = SSD_H * SSD_P + 2 * SSD_G * SSD_N

D_FF = 2816
N_EXPERTS = 8
TOP_K = 2
EXPERT_FF = 3584
N_DENSE = (DEPTH + 1) // 2
N_MOE = DEPTH // 2

GATE_COLS = N_BRANCH * D_MODEL
MIX_SPLITS = (
    LRU_W, LRU_W,
    RET_H * RET_DK, RET_H * RET_DK, RET_H * RET_DV, RET_H * RET_DV,
    GDN_CONV_CH, GDN_H * GDN_DV, 2 * GDN_H, 2 * GDN_H,
    SSD_H * SSD_P, SSD_CONV_CH, 2 * SSD_H,
)
IN_COLS = GATE_COLS + sum(MIX_SPLITS)

kernel_name = "hybrid_prefix_lru_ret_gdn_ssd_moe"


def rms_norm(x, g):
    xf = x.astype(jnp.float32)
    y = xf * lax.rsqrt(jnp.mean(xf * xf, axis=-1, keepdims=True) + EPS)
    return (y * g.astype(jnp.float32)).astype(x.dtype)


def modulate(h, shift, scale):
    return h * (1.0 + scale) + shift


def l2_normalize(t):
    return t * lax.rsqrt(jnp.sum(t * t, axis=-1, keepdims=True) + EPS)


def split_cols(p, sizes):
    return jnp.split(p, np.cumsum(sizes)[:-1].tolist(), axis=-1)


def dw_conv(u, w, b=None):
    left = (CONV_W - 1) // 2
    y = lax.conv_general_dilated(
        u, w[:, None, :].astype(u.dtype), window_strides=(1,),
        padding=[(left, CONV_W - 1 - left)],
        dimension_numbers=("NWC", "WIO", "NWC"),
        feature_group_count=u.shape[-1])
    return y if b is None else y + b.astype(u.dtype)


def flip_seq(*ts):
    return tuple(jnp.flip(t, axis=1) for t in ts)


def axial_rotary(t, row, col):
    n_freq = t.shape[-1] // 4
    freqs = ROPE_BASE ** (-jnp.arange(n_freq, dtype=jnp.float32) / n_freq)
    ang = jnp.concatenate([row[:, None] * freqs, col[:, None] * freqs], axis=-1)
    cos = jnp.cos(ang)[None, :, None, :]
    sin = jnp.sin(ang)[None, :, None, :]
    t1, t2 = t[..., 0::2], t[..., 1::2]
    return jnp.stack([t1 * cos - t2 * sin, t1 * sin + t2 * cos], axis=-1).reshape(t.shape)


def _lin_combine(e1, e2):
    a1, b1 = e1
    a2, b2 = e2
    return a1 * a2, a2 * b1 + b2


def rglru_scan(u, wa, ba, wx, bx, lam, h0, reverse):
    bsz, L, W = u.shape
    ub = u.reshape(bsz, L, LRU_BLOCKS, W // LRU_BLOCKS)
    r = jax.nn.sigmoid(jnp.einsum("blni,nio->blno", ub, wa.astype(jnp.float32)).reshape(bsz, L, W) + ba.astype(jnp.float32))
    i = jax.nn.sigmoid(jnp.einsum("blni,nio->blno", ub, wx.astype(jnp.float32)).reshape(bsz, L, W) + bx.astype(jnp.float32))
    log_a = -LRU_C * r * jax.nn.softplus(-lam.astype(jnp.float32))
    a = jnp.exp(log_a)
    b = jnp.sqrt(-jnp.expm1(2.0 * log_a)) * (i * u)
    a_cum, b_cum = lax.associative_scan(_lin_combine, (a, b), axis=1, reverse=reverse)
    h = a_cum * h0[:, None, :] + b_cum
    return h, (h[:, 0] if reverse else h[:, -1])


def rglru_branch(ctx_in, lat_in, conv_w, conv_b, wa, ba, wx, bx, lam, ctx_out):
    (xc, gc), (xl, gl) = ctx_in, lat_in
    uc = dw_conv(xc, conv_w, conv_b).astype(jnp.float32)
    ul = dw_conv(xl, conv_w, conv_b).astype(jnp.float32)
    h0 = jnp.zeros((xl.shape[0], LRU_W), jnp.float32)
    hc_sum, hl_sum = 0.0, 0.0
    for d in range(2):
        rev = d == 1
        hc, hc_last = rglru_scan(uc, wa[d], ba[d], wx[d], bx[d], lam[d], h0, rev)
        hl, _ = rglru_scan(ul, wa[d], ba[d], wx[d], bx[d], lam[d], hc_last, rev)
        hl_sum = hl_sum + hl
        if ctx_out:
            hc_sum = hc_sum + hc
    yl = (hl_sum * jax.nn.gelu(gl.astype(jnp.float32))).astype(xl.dtype)
    yc = (hc_sum * jax.nn.gelu(gc.astype(jnp.float32))).astype(xc.dtype) if ctx_out else None
    return yc, yl


def retention_chunks(q, k, v, s0, log_gamma, reverse, include_diag, return_y):
    if reverse:
        q, k, v = flip_seq(q, k, v)
    bsz, L, H, dk = q.shape
    dv = v.shape[-1]
    C = RET_CHUNK
    n = L // C
    qc = q.reshape(bsz, n, C, H, dk)
    kc = k.reshape(bsz, n, C, H, dk)
    vc = v.reshape(bsz, n, C, H, dv)
    pos = jnp.arange(C, dtype=jnp.float32)
    k_dec = kc * jnp.exp((C - 1.0 - pos)[:, None] * log_gamma)[None, None, :, :, None]
    kv = jnp.einsum("bnjhk,bnjhv->nbhkv", k_dec, vc)
    chunk_decay = jnp.exp(C * log_gamma)[None, :, None, None]

    def step(s, kv_n):
        return chunk_decay * s + kv_n, s

    s_last, s_prev = lax.scan(step, s0, kv)
    if not return_y:
        return None, s_last
    dist = pos[:, None] - pos[None, :]
    mask = (dist >= 0) if include_diag else (dist > 0)
    decay = jnp.where(mask[None], jnp.exp(jnp.maximum(dist, 0.0)[None] * log_gamma[:, None, None]), 0.0)
    scores = jnp.einsum("bnihk,bnjhk->bnhij", qc, kc) * decay
    y = jnp.einsum("bnhij,bnjhv->bnihv", scores, vc)
    q_dec = qc * jnp.exp((pos + 1.0)[:, None] * log_gamma)[None, None, :, :, None]
    y = (y + jnp.einsum("bnihk,nbhkv->bnihv", q_dec, s_prev)).reshape(bsz, L, H, dv)
    if reverse:
        (y,) = flip_seq(y)
    return y, s_last


def retention_branch(ctx_in, lat_in, norm_g, row, col, ctx_out):
    log_gamma = jnp.log(1.0 - 2.0 ** (-5.0 - jnp.arange(RET_H, dtype=jnp.float32)))

    def heads(q, k, v):
        bsz, L, _ = q.shape
        q = q.astype(jnp.float32).reshape(bsz, L, RET_H, RET_DK) * RET_DK ** -0.5
        k = k.astype(jnp.float32).reshape(bsz, L, RET_H, RET_DK)
        v = v.astype(jnp.float32).reshape(bsz, L, RET_H, RET_DV)
        return q, k, v

    qc, kc, vc = heads(*ctx_in[:3])
    ql, kl, vl = heads(*lat_in[:3])
    ql = axial_rotary(ql, row, col)
    kl = axial_rotary(kl, row, col)
    s0 = jnp.zeros((ql.shape[0], RET_H, RET_DK, RET_DV), jnp.float32)
    yc, yl = 0.0, 0.0
    for rev, diag in ((False, True), (True, False)):
        yc_d, sc = retention_chunks(qc, kc, vc, s0, log_gamma, rev, diag, ctx_out)
        yl_d, _ = retention_chunks(ql, kl, vl, sc, log_gamma, rev, diag, True)
        yl = yl + yl_d
        if ctx_out:
            yc = yc + yc_d

    def out(y, g):
        y = rms_norm(y, norm_g.reshape(RET_H, RET_DV)).reshape(g.shape)
        return (jax.nn.silu(g.astype(jnp.float32)) * y).astype(g.dtype)

    return (out(yc, ctx_in[3]) if ctx_out else None), out(yl, lat_in[3])


def gdn_chunks(q, k, v, g, beta, s0, reverse, return_y):
    if reverse:
        q, k, v, g, beta = flip_seq(q, k, v, g, beta)
    bsz, L, H, dk = q.shape
    dv = v.shape[-1]
    C = GDN_CHUNK
    n = L // C
    qc = q.reshape(bsz, n, C, H, dk)
    kc = k.reshape(bsz, n, C, H, dk)
    vc = v.reshape(bsz, n, C, H, dv)
    bc = beta.reshape(bsz, n, C, H)
    gcs = jnp.cumsum(g.reshape(bsz, n, C, H), axis=2)
    g_last = gcs[:, :, -1]
    pos = jnp.arange(C)
    lower = pos[:, None] >= pos[None, :]
    strict = pos[:, None] > pos[None, :]
    gh = jnp.moveaxis(gcs, -1, 2)
    diff = gh[..., :, None] - gh[..., None, :]
    decay = jnp.where(lower, jnp.exp(jnp.where(lower, diff, 0.0)), 0.0)
    kb = kc * bc[..., None]
    l_mat = jnp.einsum("bnihk,bnjhk->bnhij", kb, kc) * jnp.where(strict, decay, 0.0)
    eye_c = jnp.eye(C, dtype=jnp.float32)
    rhs = jnp.concatenate([vc * bc[..., None], kb * jnp.exp(gcs)[..., None]], axis=-1)
    rhs = jnp.moveaxis(rhs, 2, 3)
    sol = lax.linalg.triangular_solve(eye_c + l_mat, rhs, left_side=True, lower=True, unit_diagonal=True)
    u, w = sol[..., :dv], sol[..., dv:]
    k_dec = kc * jnp.exp(g_last[:, :, None, :] - gcs)[..., None]
    eye_k = jnp.eye(dk, dtype=jnp.float32)
    m_state = jnp.exp(g_last)[..., None, None] * eye_k - jnp.einsum("bnchk,bnhcj->bnhkj", k_dec, w)
    u_state = jnp.einsum("bnchk,bnhcv->bnhkv", k_dec, u)

    def step(s, mu):
        m, uu = mu
        return jnp.einsum("bhkj,bhjv->bhkv", m, s) + uu, s

    s_last, s_prev = lax.scan(step, s0, (jnp.moveaxis(m_state, 1, 0), jnp.moveaxis(u_state, 1, 0)))
    if not return_y:
        return None, s_last
    attn = jnp.einsum("bnihk,bnjhk->bnhij", qc, kc) * decay
    q_eff = jnp.moveaxis(qc * jnp.exp(gcs)[..., None], 2, 3) - attn @ w
    y = jnp.einsum("bnhck,nbhkv->bnhcv", q_eff, s_prev) + attn @ u
    y = jnp.moveaxis(y, 2, 3).reshape(bsz, L, H, dv)
    if reverse:
        (y,) = flip_seq(y)
    return y, s_last


def gdn_branch(ctx_in, lat_in, conv_w, a_log, dt_bias, norm_g, ctx_out):
    def prep(qkv, a, b):
        bsz, L, _ = qkv.shape
        h = jax.nn.silu(dw_conv(qkv, conv_w).astype(jnp.float32))
        q, k, v = split_cols(h, (GDN_H * GDN_DK, GDN_H * GDN_DK, GDN_H * GDN_DV))
        q = l2_normalize(q.reshape(bsz, L, GDN_H, GDN_DK)) * GDN_DK ** -0.5
        k = l2_normalize(k.reshape(bsz, L, GDN_H, GDN_DK))
        v = v.reshape(bsz, L, GDN_H, GDN_DV)
        a = a.astype(jnp.float32).reshape(bsz, L, 2, GDN_H)
        g = -jnp.exp(a_log.astype(jnp.float32)) * jax.nn.softplus(a + dt_bias.astype(jnp.float32))
        beta = jax.nn.sigmoid(b.astype(jnp.float32)).reshape(bsz, L, 2, GDN_H)
        return q, k, v, g, beta

    qkv_c, zc, ac, bc = ctx_in
    qkv_l, zl, al, bl = lat_in
    qc, kc, vc, gc, betac = prep(qkv_c, ac, bc)
    ql, kl, vl, gl, betal = prep(qkv_l, al, bl)
    s0 = jnp.zeros((ql.shape[0], GDN_H, GDN_DK, GDN_DV), jnp.float32)
    oc, ol = 0.0, 0.0
    for d in range(2):
        rev = d == 1
        oc_d, sc = gdn_chunks(qc, kc, vc, gc[:, :, d], betac[:, :, d], s0, rev, ctx_out)
        ol_d, _ = gdn_chunks(ql, kl, vl, gl[:, :, d], betal[:, :, d], sc, rev, True)
        ol = ol + ol_d
        if ctx_out:
            oc = oc + oc_d

    def out(o, z):
        bsz, L, _ = z.shape
        zf = jax.nn.silu(z.astype(jnp.float32)).reshape(bsz, L, GDN_H, GDN_DV)
        return (rms_norm(o, norm_g) * zf).reshape(bsz, L, GDN_H * GDN_DV).astype(z.dtype)

    return (out(oc, zc) if ctx_out else None), out(ol, zl)


def ssd_chunks(x, dt, a, bm, cm, s0, reverse, return_y):
    if reverse:
        x, dt, bm, cm = flip_seq(x, dt, bm, cm)
    bsz, L, H, P = x.shape
    N = bm.shape[-1]
    C = SSD_CHUNK
    n = L // C
    xc = x.reshape(bsz, n, C, H, P)
    dtc = dt.reshape(bsz, n, C, H)
    bc = bm.reshape(bsz, n, C, H, N)
    cc = cm.reshape(bsz, n, C, H, N)
    cs = jnp.cumsum(dtc * a, axis=2)
    cs_last = cs[:, :, -1]
    decay_states = jnp.exp(cs_last[:, :, None] - cs) * dtc
    states = jnp.einsum("bnjhs,bnjh,bnjhp->nbhps", bc, decay_states, xc)

    def step(s, inp):
        dec, st = inp
        return dec[..., None, None] * s + st, s

    s_last, s_prev = lax.scan(step, s0, (jnp.moveaxis(jnp.exp(cs_last), 1, 0), states))
    if not return_y:
        return None, s_last
    pos = jnp.arange(C)
    lower = pos[:, None] >= pos[None, :]
    csh = jnp.moveaxis(cs, -1, 2)
    seg = csh[..., :, None] - csh[..., None, :]
    l_mat = jnp.where(lower, jnp.exp(jnp.where(lower, seg, 0.0)), 0.0)
    scores = jnp.einsum("bnihs,bnjhs->bnhij", cc, bc) * l_mat * jnp.moveaxis(dtc, -1, 2)[..., None, :]
    y = jnp.einsum("bnhij,bnjhp->bnihp", scores, xc)
    y = y + jnp.einsum("bnihs,nbhps->bnihp", cc * jnp.exp(cs)[..., None], s_prev)
    y = y.reshape(bsz, L, H, P)
    if reverse:
        (y,) = flip_seq(y)
    return y, s_last


def ssd_branch(ctx_in, lat_in, conv_w, conv_b, a_log, dt_bias, d_skip, norm_g, ctx_out):
    def prep(xbc, dt_raw):
        bsz, L, _ = xbc.shape
        h = jax.nn.silu(dw_conv(xbc, conv_w, conv_b).astype(jnp.float32))
        xs, bm, cm = split_cols(h, (SSD_H * SSD_P, SSD_G * SSD_N, SSD_G * SSD_N))
        rep = SSD_H // SSD_G
        xs = xs.reshape(bsz, L, SSD_H, SSD_P)
        bm = jnp.repeat(bm.reshape(bsz, L, SSD_G, SSD_N), rep, axis=2)
        cm = jnp.repeat(cm.reshape(bsz, L, SSD_G, SSD_N), rep, axis=2)
        dt = jax.nn.softplus(dt_raw.astype(jnp.float32).reshape(bsz, L, 2, SSD_H) + dt_bias.astype(jnp.float32))
        return xs, bm, cm, dt

    zc, xbc_c, dt_c = ctx_in
    zl, xbc_l, dt_l = lat_in
    xc, bc, cc, dtc = prep(xbc_c, dt_c)
    xl, bl, cl, dtl = prep(xbc_l, dt_l)
    a = -jnp.exp(a_log.astype(jnp.float32))
    s0 = jnp.zeros((xl.shape[0], SSD_H, SSD_P, SSD_N), jnp.float32)
    yc, yl = 0.0, 0.0
    for d in range(2):
        rev = d == 1
        yc_d, sc = ssd_chunks(xc, dtc[:, :, d], a[d], bc, cc, s0, rev, ctx_out)
        yl_d, _ = ssd_chunks(xl, dtl[:, :, d], a[d], bl, cl, sc, rev, True)
        yl = yl + yl_d
        if ctx_out:
            yc = yc + yc_d

    def out(y, xs, z):
        y = (y + d_skip.astype(jnp.float32)[:, None] * xs).reshape(z.shape)
        return rms_norm(y * jax.nn.silu(z.astype(jnp.float32)), norm_g).astype(z.dtype)

    return (out(yc, xc, zc) if ctx_out else None), out(yl, xl, zl)


def merge_branches(ys, gate_pre, w_branch, w_out):
    gates = jax.nn.sigmoid(gate_pre.astype(jnp.float32)).astype(gate_pre.dtype)
    merged = 0.0
    for nb, y in enumerate(ys):
        merged = merged + gates[..., nb * D_MODEL:(nb + 1) * D_MODEL] * (y @ w_branch[nb])
    return merged @ w_out


def token_mixer(hc, hl, row, col, w_in, lru_conv_w, lru_conv_b, lru_wa, lru_ba, lru_wx, lru_bx,
                lru_lambda, ret_norm, gdn_conv_w, gdn_a_log, gdn_dt_bias, gdn_norm, ssd_conv_w,
                ssd_conv_b, ssd_a_log, ssd_dt_bias, ssd_d, ssd_norm, w_branch, w_out, ctx_out):
    w_gate, w_mix = w_in[:, :GATE_COLS], w_in[:, GATE_COLS:]
    pc = split_cols(hc @ w_mix, MIX_SPLITS)
    pl = split_cols(hl @ w_mix, MIX_SPLITS)
    outs = (
        rglru_branch(pc[0:2], pl[0:2], lru_conv_w, lru_conv_b, lru_wa, lru_ba, lru_wx, lru_bx, lru_lambda, ctx_out),
        retention_branch(pc[2:6], pl[2:6], ret_norm, row, col, ctx_out),
        gdn_branch(pc[6:10], pl[6:10], gdn_conv_w, gdn_a_log, gdn_dt_bias, gdn_norm, ctx_out),
        ssd_branch(pc[10:13], pl[10:13], ssd_conv_w, ssd_conv_b, ssd_a_log, ssd_dt_bias, ssd_d, ssd_norm, ctx_out),
    )
    out_l = merge_branches([o[1] for o in outs], hl @ w_gate, w_branch, w_out)
    out_c = merge_branches([o[0] for o in outs], hc @ w_gate, w_branch, w_out) if ctx_out else None
    return out_c, out_l


def swiglu(h, w_gate, w_up, w_down):
    return (jax.nn.silu(h @ w_gate) * (h @ w_up)) @ w_down


def moe_ffn(h, router, wg, wu, wd):
    logits = (h @ router).astype(jnp.float32)
    top_v, top_i = lax.top_k(logits, TOP_K)
    probs = jax.nn.softmax(top_v, axis=-1)
    gates = jnp.sum(jax.nn.one_hot(top_i, N_EXPERTS, dtype=jnp.float32) * probs[..., None], axis=-2)
    out = jnp.zeros_like(h)
    for e in range(N_EXPERTS):
        out = out + gates[..., e:e + 1].astype(h.dtype) * swiglu(h, wg[e], wu[e], wd[e])
    return out


def setup_inputs(seed: int = 0) -> dict:
    key = jax.random.key(seed)
    keys = list(jax.random.split(key, 48))

    def nrm(shape, scale):
        return scale * jax.random.normal(keys.pop(), shape, jnp.float32)

    def unif(shape, lo, hi):
        return jax.random.uniform(keys.pop(), shape, jnp.float32, lo, hi)

    def gain(shape):
        return 1.0 + nrm(shape, 0.05)

    def dt_bias(shape):
        dt = jnp.exp(unif(shape, math.log(1e-3), math.log(1e-1)))
        return dt + jnp.log(-jnp.expm1(-dt))

    D = D_MODEL
    bw = LRU_W // LRU_BLOCKS
    lru_a = unif((DEPTH, 2, LRU_W), 0.9, 0.999) ** (1.0 / LRU_C)
    return {
        "x": nrm((BATCH, SEQ, D), 1.0),
        "c": nrm((BATCH, D), 1.0),
        "ctx": nrm((BATCH, CTX_LEN, D), 1.0),
        "c_ctx": nrm((D,), 1.0),
        "w_mod": nrm((DEPTH, D, 6 * D), 0.5 * D ** -0.5),
        "b_mod": nrm((DEPTH, 6 * D), 0.01),
        "norm_mix": gain((DEPTH, D)),
        "norm_ffn": gain((DEPTH, D)),
        "w_in": nrm((DEPTH, D, IN_COLS), D ** -0.5),
        "lru_conv_w": nrm((DEPTH, CONV_W, LRU_W), CONV_W ** -0.5),
        "lru_conv_b": nrm((DEPTH, LRU_W), 0.01),
        "lru_wa": nrm((DEPTH, 2, LRU_BLOCKS, bw, bw), bw ** -0.5),
        "lru_ba": nrm((DEPTH, 2, LRU_W), 0.01),
        "lru_wx": nrm((DEPTH, 2, LRU_BLOCKS, bw, bw), bw ** -0.5),
        "lru_bx": nrm((DEPTH, 2, LRU_W), 0.01),
        "lru_lambda": jnp.log(lru_a) - jnp.log1p(-lru_a),
        "ret_norm": gain((DEPTH, RET_H * RET_DV)),
        "gdn_conv_w": nrm((DEPTH, CONV_W, GDN_CONV_CH), CONV_W ** -0.5),
        "gdn_a_log": jnp.log(unif((DEPTH, 2, GDN_H), 1.0, 16.0)),
        "gdn_dt_bias": dt_bias((DEPTH, 2, GDN_H)),
        "gdn_norm": gain((DEPTH, GDN_DV)),
        "ssd_conv_w": nrm((DEPTH, CONV_W, SSD_CONV_CH), CONV_W ** -0.5),
        "ssd_conv_b": nrm((DEPTH, SSD_CONV_CH), 0.01),
        "ssd_a_log": jnp.log(unif((DEPTH, 2, SSD_H), 1.0, 16.0)),
        "ssd_dt_bias": dt_bias((DEPTH, 2, SSD_H)),
        "ssd_d": gain((DEPTH, SSD_H)),
        "ssd_norm": gain((DEPTH, SSD_H * SSD_P)),
        "w_branch": nrm((DEPTH, N_BRANCH, BRANCH_W, D), BRANCH_W ** -0.5),
        "w_out": nrm((DEPTH, D, D), D ** -0.5),
        "ffn_wg": nrm((N_DENSE, D, D_FF), D ** -0.5),
        "ffn_wu": nrm((N_DENSE, D, D_FF), D ** -0.5),
        "ffn_wd": nrm((N_DENSE, D_FF, D), D_FF ** -0.5),
        "moe_router": nrm((N_MOE, D, N_EXPERTS), D ** -0.5),
        "moe_wg": nrm((N_MOE, N_EXPERTS, D, EXPERT_FF), D ** -0.5),
        "moe_wu": nrm((N_MOE, N_EXPERTS, D, EXPERT_FF), D ** -0.5),
        "moe_wd": nrm((N_MOE, N_EXPERTS, EXPERT_FF, D), EXPERT_FF ** -0.5),
        "final_norm": gain((D,)),
    }


def reference(x, c, ctx, c_ctx, w_mod, b_mod, norm_mix, norm_ffn, w_in, lru_conv_w, lru_conv_b,
              lru_wa, lru_ba, lru_wx, lru_bx, lru_lambda, ret_norm, gdn_conv_w, gdn_a_log, gdn_dt_bias,
              gdn_norm, ssd_conv_w, ssd_conv_b, ssd_a_log, ssd_dt_bias, ssd_d, ssd_norm, w_branch, w_out,
              ffn_wg, ffn_wu, ffn_wd, moe_router, moe_wg, moe_wu, moe_wd, final_norm):
    n_lat = x.shape[1]
    rows = n_lat // GRID_W
    row = jnp.repeat(jnp.arange(rows, dtype=jnp.float32), GRID_W)
    col = (jnp.arange(n_lat) % GRID_W).astype(jnp.float32)
    xl, xc = x, ctx
    for layer in range(DEPTH):
        ctx_out = layer < DEPTH - 1
        mod_l = jnp.split((jax.nn.silu(c) @ w_mod[layer] + b_mod[layer])[:, None, :], 6, axis=-1)
        n_cm = 6 if ctx_out else 2
        mod_c = jnp.split(jax.nn.silu(c_ctx) @ w_mod[layer][:, :n_cm * D_MODEL] + b_mod[layer][:n_cm * D_MODEL], n_cm, axis=-1)
        hl = modulate(rms_norm(xl, norm_mix[layer]), mod_l[0], mod_l[1])
        hc = modulate(rms_norm(xc, norm_mix[layer]), mod_c[0], mod_c[1])
        oc, ol = token_mixer(hc, hl, row, col, w_in[layer], lru_conv_w[layer], lru_conv_b[layer],
                             lru_wa[layer], lru_ba[layer], lru_wx[layer], lru_bx[layer], lru_lambda[layer],
                             ret_norm[layer], gdn_conv_w[layer], gdn_a_log[layer], gdn_dt_bias[layer],
                             gdn_norm[layer], ssd_conv_w[layer], ssd_conv_b[layer], ssd_a_log[layer],
                             ssd_dt_bias[layer], ssd_d[layer], ssd_norm[layer], w_branch[layer], w_out[layer],
                             ctx_out)
        xl = xl + mod_l[2] * ol
        hl = modulate(rms_norm(xl, norm_ffn[layer]), mod_l[3], mod_l[4])
        if ctx_out:
            xc = xc + mod_c[2] * oc
            hc = modulate(rms_norm(xc, norm_ffn[layer]), mod_c[3], mod_c[4])
        j = layer // 2
        if layer % 2 == 0:
            xl = xl + mod_l[5] * swiglu(hl, ffn_wg[j], ffn_wu[j], ffn_wd[j])
            if ctx_out:
                xc = xc + mod_c[5] * swiglu(hc, ffn_wg[j], ffn_wu[j], ffn_wd[j])
        else:
            xl = xl + mod_l[5] * moe_ffn(hl, moe_router[j], moe_wg[j], moe_wu[j], moe_wd[j])
            if ctx_out:
                xc = xc + mod_c[5] * moe_ffn(hc, moe_router[j], moe_wg[j], moe_wu[j], moe_wd[j])
    return rms_norm(xl, final_norm)
```

```python
import functools
import math

import numpy as np
import jax
import jax.numpy as jnp
from jax import lax
from jax.experimental import pallas as pl
from jax.experimental.pallas import tpu as pltpu

F32 = jnp.float32
BF16 = jnp.bfloat16
HI = lax.Precision.HIGHEST

EPS = 1e-6
GRID_W = 64
N_BRANCH = 4
BRANCH_W = 512
CONV_W = 4
LRU_W = 512
LRU_BLOCKS = 8
LRU_C = 8.0
RET_H, RET_DK, RET_DV, RET_CHUNK = 4, 64, 128, 128
ROPE_BASE = 10000.0
GDN_H, GDN_DK, GDN_DV, GDN_CHUNK = 4, 128, 128, 64
SSD_H, SSD_P, SSD_G, SSD_N, SSD_CHUNK = 8, 64, 2, 64, 128
N_EXPERTS = 8
GDN_QKV = 2 * GDN_H * GDN_DK + GDN_H * GDN_DV
SSD_XBC = SSD_H * SSD_P + 2 * SSD_G * SSD_N
MIX_SPLITS = (LRU_W, LRU_W, RET_H * RET_DK, RET_H * RET_DK, RET_H * RET_DV, RET_H * RET_DV,
              GDN_QKV, GDN_H * GDN_DV, 2 * GDN_H, 2 * GDN_H, SSD_H * SSD_P, SSD_XBC, 2 * SSD_H)

LANES = 128
VMEM_LIMIT = 56 * 1024 * 1024

ROW_TILE = 256
MIX_ROWS = 256
MOE_BLOCK = 1024
MOE_SLOTS = 128
MOE_FSPLIT = 4

CONV_COLS = LRU_W + GDN_QKV + SSD_XBC
RET_COLS = 2 * RET_H * RET_DK + 2 * RET_H * RET_DV
SMALL_COLS = LANES


def _cparams(sem):
    return pltpu.CompilerParams(dimension_semantics=sem, vmem_limit_bytes=VMEM_LIMIT)


def _dot(a, b, precision=None):
    return jnp.dot(a, b, preferred_element_type=F32, precision=precision)


def _dot_nt(a, b, precision=None):
    return lax.dot_general(a, b, (((1,), (1,)), ((), ())), preferred_element_type=F32, precision=precision)


def _dot_tn(a, b, precision=None):
    return lax.dot_general(a, b, (((0,), (0,)), ((), ())), preferred_element_type=F32, precision=precision)


def _sigmoid(x):
    return 1.0 / (1.0 + jnp.exp(-x))


def _silu(x):
    return x * _sigmoid(x)


def _softplus(x):
    return jnp.maximum(x, 0.0) + jnp.log1p(jnp.exp(-jnp.abs(x)))


def _gelu_tanh(x):
    return 0.5 * x * (1.0 + jnp.tanh(math.sqrt(2.0 / math.pi) * (x + 0.044715 * (x * x * x))))


def _rms_mod(x, g, shift, scale):
    ms = jnp.mean(x * x, axis=-1, keepdims=True)
    return (x * lax.rsqrt(ms + EPS) * g) * (1.0 + scale) + shift


def _pick_mod(mod_ref, is_ctx, k, d):
    return jnp.where(is_ctx, mod_ref[1:2, k * d:(k + 1) * d], mod_ref[0:1, k * d:(k + 1) * d])


def _full(shape):
    n = len(shape)
    return pl.BlockSpec(shape, lambda *_: (0,) * n)


def _mod_kernel(ct_ref, w_ref, b_ref, o_ref):
    s = _silu(ct_ref[...])
    w = w_ref[...]
    b = b_ref[...]
    o_ref[0:1, :] = jnp.sum(s[:, 0:1] * w, axis=0, keepdims=True) + b
    o_ref[1:2, :] = jnp.sum(s[:, 1:2] * w, axis=0, keepdims=True) + b


def _mod_call(ct, w, b):
    d, n = w.shape
    tn = 512
    return pl.pallas_call(
        _mod_kernel,
        grid=(n // tn,),
        in_specs=[_full((d, 2)), pl.BlockSpec((d, tn), lambda j: (0, j)), pl.BlockSpec((1, tn), lambda j: (0, j))],
        out_specs=pl.BlockSpec((2, tn), lambda j: (0, j)),
        out_shape=jax.ShapeDtypeStruct((2, n), F32),
        compiler_params=_cparams(("arbitrary",)),
        name="adaln_mod",
    )(ct, w, b)


def _proj_kernel(x_ref, mod_ref, g_ref, w_ref, cos_ref, sin_ref,
                 conv_o, lg_o, ret_o, gz_o, sz_o, sm_o, *, n_ctx_tiles):
    is_ctx = pl.program_id(0) < n_ctx_tiles
    d = x_ref.shape[1]
    tm = x_ref.shape[0]
    h = _rms_mod(x_ref[...], g_ref[...], _pick_mod(mod_ref, is_ctx, 0, d), _pick_mod(mod_ref, is_ctx, 1, d)).astype(BF16)

    def mm(a, b):
        return _dot(h, w_ref[:, a:b])

    c = 0
    conv_o[...] = mm(c, c + CONV_COLS)
    c += CONV_COLS
    lg_o[...] = mm(c, c + LRU_W)
    c += LRU_W
    qk_w = 2 * RET_H * RET_DK
    half = RET_DK // 2
    lane = lax.broadcasted_iota(jnp.int32, (tm, RET_H * RET_DK), 1)
    first = (lane % RET_DK) < half
    cos = cos_ref[...]
    sin = sin_ref[...]

    def rot(t):
        partner = jnp.where(first, pltpu.roll(t, RET_H * RET_DK - half, axis=1), pltpu.roll(t, half, axis=1))
        return t * cos + partner * sin

    ret_o[:, 0:qk_w // 2] = rot(mm(c, c + qk_w // 2)) * (RET_DK ** -0.5)
    ret_o[:, qk_w // 2:qk_w] = rot(mm(c + qk_w // 2, c + qk_w))
    ret_o[:, qk_w:RET_COLS] = mm(c + qk_w, c + RET_COLS)
    c += RET_COLS
    gz_o[...] = mm(c, c + GDN_H * GDN_DV)
    c += GDN_H * GDN_DV
    sz_o[...] = mm(c, c + SSD_H * SSD_P)
    c += SSD_H * SSD_P
    sm_o[...] = mm(c, c + SMALL_COLS)


def _proj_call(x_all, mod, g, w, cos_t, sin_t, n_ctx):
    lt, d = x_all.shape
    tm = ROW_TILE
    widths = (CONV_COLS, LRU_W, RET_COLS, GDN_H * GDN_DV, SSD_H * SSD_P, SMALL_COLS)
    row = lambda wd: pl.BlockSpec((tm, wd), lambda i: (i, 0))
    return pl.pallas_call(
        functools.partial(_proj_kernel, n_ctx_tiles=n_ctx // tm),
        grid=(lt // tm,),
        in_specs=[row(d), _full(mod.shape), _full(g.shape), _full(w.shape), row(cos_t.shape[1]), row(sin_t.shape[1])],
        out_specs=[row(wd) for wd in widths],
        out_shape=[jax.ShapeDtypeStruct((lt, wd), F32) for wd in widths],
        compiler_params=_cparams(("parallel",)),
        name="mix_proj",
    )(x_all, mod, g, w, cos_t, sin_t)


def _conv_kernel(u_ref, p_ref, n_ref, w_ref, b_ref, lru_o, gdn_o, ssd_o, *, seg_blocks):
    i = pl.program_id(0)
    r = u_ref.shape[0]
    has_prev = jnp.logical_and(i != 0, i != seg_blocks).astype(F32)
    has_next = jnp.logical_and(i != seg_blocks - 1, i != pl.num_programs(0) - 1).astype(F32)
    rows = lax.broadcasted_iota(jnp.int32, (r, 1), 0)
    halo = p_ref.shape[0]

    def conv(c0, c1):
        u = u_ref[:, c0:c1]
        pm = p_ref[halo - 1:halo, c0:c1] * has_prev
        n0 = n_ref[0:1, c0:c1] * has_next
        n1 = n_ref[1:2, c0:c1] * has_next
        um1 = jnp.where(rows == 0, pm, pltpu.roll(u, 1, axis=0))
        up1 = jnp.where(rows == r - 1, n0, pltpu.roll(u, r - 1, axis=0))
        up2 = jnp.where(rows == r - 2, n0, jnp.where(rows == r - 1, n1, pltpu.roll(u, r - 2, axis=0)))
        w = w_ref[:, c0:c1]
        return w[0:1] * um1 + w[1:2] * u + w[2:3] * up1 + w[3:4] * up2 + b_ref[:, c0:c1]

    step = 2 * LANES
    for c0 in range(0, LRU_W, step):
        lru_o[:, c0:c0 + step] = conv(c0, c0 + step)
    base = LRU_W
    for hd in range(2 * GDN_H):
        c0 = hd * GDN_DK
        t = _silu(conv(base + c0, base + c0 + GDN_DK))
        t = t * lax.rsqrt(jnp.sum(t * t, axis=-1, keepdims=True) + EPS)
        if hd < GDN_H:
            t = t * (GDN_DK ** -0.5)
        gdn_o[:, c0:c0 + GDN_DK] = t
    for c0 in range(2 * GDN_H * GDN_DK, GDN_QKV, step):
        gdn_o[:, c0:c0 + step] = _silu(conv(base + c0, base + c0 + step))
    base = LRU_W + GDN_QKV
    for c0 in range(0, SSD_XBC, step):
        ssd_o[:, c0:c0 + step] = _silu(conv(base + c0, base + c0 + step))


def _conv_call(u, w, b, n_ctx):
    lt, wd = u.shape
    r = ROW_TILE
    halo = 8
    nb = lt // r
    hb = r // halo
    widths = (LRU_W, GDN_QKV, SSD_XBC)
    return pl.pallas_call(
        functools.partial(_conv_kernel, seg_blocks=n_ctx // r),
        grid=(nb,),
        in_specs=[pl.BlockSpec((r, wd), lambda i: (i, 0)),
                  pl.BlockSpec((halo, wd), lambda i: (jnp.maximum(i * hb - 1, 0), 0)),
                  pl.BlockSpec((halo, wd), lambda i: (jnp.minimum((i + 1) * hb, nb * hb - 1), 0)),
                  _full(w.shape), _full(b.shape)],
        out_specs=[pl.BlockSpec((r, x), lambda i: (i, 0)) for x in widths],
        out_shape=[jax.ShapeDtypeStruct((lt, x), F32) for x in widths],
        compiler_params=_cparams(("parallel",)),
        name="dw_conv",
    )(u, u, u, w, b)


def _bwd_block(i, n_ctx_blocks, n_blocks):
    return jnp.where(i < n_ctx_blocks, n_ctx_blocks - 1 - i, n_blocks + n_ctx_blocks - 1 - i)


def _dir_specs(r, width, ncb, nb):
    return (pl.BlockSpec((r, width), lambda i: (i, 0)),
            pl.BlockSpec((r, width), lambda i: (_bwd_block(i, ncb, nb), 0)))


def _dir_specs_t(rows, r, ncb, nb):
    return (pl.BlockSpec((rows, r), lambda i: (0, i)),
            pl.BlockSpec((rows, r), lambda i: (0, _bwd_block(i, ncb, nb))))


def _chunk_order(d, n):
    return range(n) if d == 0 else range(n - 1, -1, -1)


def _lru_kernel(uf_ref, ub_ref, w_ref, b_ref, lam_ref, yf_o, yb_o, carry_ref):
    @pl.when(pl.program_id(0) == 0)
    def _():
        carry_ref[...] = jnp.zeros_like(carry_ref)

    r = uf_ref.shape[0]
    rows = lax.broadcasted_iota(jnp.int32, (r, 1), 0)

    def run(d, u_ref, o_ref):
        u = u_ref[...]
        gates = _sigmoid(_dot(u.astype(BF16), w_ref[d]) + b_ref[d])
        rg = gates[:, 0:LRU_W]
        ig = gates[:, LRU_W:2 * LRU_W]
        log_a = (-LRU_C) * rg * _softplus(-lam_ref[d])
        a = jnp.exp(log_a)
        b = jnp.sqrt(1.0 - jnp.exp(2.0 * log_a)) * (ig * u)
        sh = 1
        while sh < r:
            if d == 0:
                valid = rows >= sh
                a_s = pltpu.roll(a, sh, axis=0)
                b_s = pltpu.roll(b, sh, axis=0)
            else:
                valid = rows < r - sh
                a_s = pltpu.roll(a, r - sh, axis=0)
                b_s = pltpu.roll(b, r - sh, axis=0)
            b = jnp.where(valid, a * b_s + b, b)
            a = jnp.where(valid, a * a_s, a)
            sh *= 2
        h = a * carry_ref[d, 0:1, :] + b
        o_ref[...] = h
        carry_ref[d, 0:1, :] = h[r - 1:r, :] if d == 0 else h[0:1, :]

    run(0, uf_ref, yf_o)
    run(1, ub_ref, yb_o)


def _lru_call(u, w, b, lam, n_ctx):
    lt = u.shape[0]
    r = MIX_ROWS
    nb, ncb = lt // r, n_ctx // r
    fs, bs = _dir_specs(r, LRU_W, ncb, nb)
    return pl.pallas_call(
        _lru_kernel,
        grid=(nb,),
        in_specs=[fs, bs, _full(w.shape), _full(b.shape), _full(lam.shape)],
        out_specs=[fs, bs],
        out_shape=[jax.ShapeDtypeStruct((lt, LRU_W), F32)] * 2,
        scratch_shapes=[pltpu.VMEM((2, 8, LRU_W), F32)],
        compiler_params=_cparams(("arbitrary",)),
        name="rglru_scan",
    )(u, u, w, b, lam)


def _ret_kernel(xf_ref, xb_ref, dmat_ref, qd_ref, kd_ref, sdec_ref, bd_ref, yf_o, yb_o, s_ref):
    @pl.when(pl.program_id(0) == 0)
    def _():
        s_ref[...] = jnp.zeros_like(s_ref)

    c = RET_CHUNK
    qw = RET_H * RET_DK
    lane_head = lax.broadcasted_iota(jnp.int32, (c, qw), 1) // RET_DK

    def run(d, x_ref, o_ref):
        for ci in _chunk_order(d, x_ref.shape[0] // c):
            rs = slice(ci * c, (ci + 1) * c)
            q = x_ref[rs, 0:qw]
            k = x_ref[rs, qw:2 * qw]
            kb = k.astype(BF16)
            vb = x_ref[rs, 2 * qw:2 * qw + RET_H * RET_DV].astype(BF16)
            s = s_ref[d]
            y_inter = _dot((q * qd_ref[d]).astype(BF16), s.astype(BF16))
            upd = _dot_tn((k * kd_ref[d]).astype(BF16), vb)
            s_ref[d] = sdec_ref[...] * s + bd_ref[...] * upd
            for hd in range(RET_H):
                qh = jnp.where(lane_head == hd, q, 0.0).astype(BF16)
                sc = _dot_nt(qh, kb) * dmat_ref[d, hd]
                vs = slice(hd * RET_DV, (hd + 1) * RET_DV)
                o_ref[rs, vs] = _dot(sc.astype(BF16), vb[:, vs]) + y_inter[:, vs]

    run(0, xf_ref, yf_o)
    run(1, xb_ref, yb_o)


def _ret_tables():
    c = RET_CHUNK
    lg = jnp.log(1.0 - 2.0 ** (-5.0 - jnp.arange(RET_H, dtype=F32)))
    pos = jnp.arange(c, dtype=F32)
    dist = pos[:, None] - pos[None, :]
    d_f = jnp.where(dist >= 0, jnp.exp(jnp.maximum(dist, 0.0)[None] * lg[:, None, None]), 0.0)
    d_b = jnp.where(dist < 0, jnp.exp(jnp.maximum(-dist, 0.0)[None] * lg[:, None, None]), 0.0)
    dmat = jnp.stack([d_f, d_b])
    rep = lambda t: jnp.repeat(t, RET_DK, axis=1)
    qd = jnp.stack([rep(jnp.exp((pos + 1.0)[:, None] * lg)), rep(jnp.exp((c - pos)[:, None] * lg))])
    kd = jnp.stack([rep(jnp.exp((c - 1.0 - pos)[:, None] * lg)), rep(jnp.exp(pos[:, None] * lg))])
    hk = jnp.repeat(jnp.arange(RET_H), RET_DK)
    hv = jnp.repeat(jnp.arange(RET_H), RET_DV)
    bd = (hk[:, None] == hv[None, :]).astype(F32)
    sdec = jnp.broadcast_to(jnp.repeat(jnp.exp(c * lg), RET_DK)[:, None], bd.shape)
    return dmat, qd, kd, sdec, bd


def _ret_call(x, n_ctx):
    lt = x.shape[0]
    r = MIX_ROWS
    nb, ncb = lt // r, n_ctx // r
    tabs = _ret_tables()
    fs, bs = _dir_specs(r, 2 * RET_H * RET_DK + RET_H * RET_DV, ncb, nb)
    os_f, os_b = _dir_specs(r, RET_H * RET_DV, ncb, nb)
    return pl.pallas_call(
        _ret_kernel,
        grid=(nb,),
        in_specs=[fs, bs] + [_full(t.shape) for t in tabs],
        out_specs=[os_f, os_b],
        out_shape=[jax.ShapeDtypeStruct((lt, RET_H * RET_DV), F32)] * 2,
        scratch_shapes=[pltpu.VMEM((2, RET_H * RET_DK, RET_H * RET_DV), F32)],
        compiler_params=_cparams(("arbitrary",)),
        name="retention_scan",
    )(x, x, *tabs)


def _ssd_kernel(xf_ref, xb_ref, smf_ref, smb_ref, stf_ref, stb_ref, tri_ref, trit_ref,
                dtb_c_ref, dtb_r_ref, alog_c_ref, alog_r_ref, dskip_ref, yf_o, yb_o, s_ref):
    @pl.when(pl.program_id(0) == 0)
    def _():
        s_ref[...] = jnp.zeros_like(s_ref)

    c = SSD_CHUNK
    xw = SSD_H * SSD_P
    gw = SSD_G * SSD_N
    lo = lax.broadcasted_iota(jnp.int32, (c, LANES), 1) < SSD_N
    row_lo = lax.broadcasted_iota(jnp.int32, (LANES, 1), 0) < SSD_N
    lane_lo = lax.broadcasted_iota(jnp.int32, (1, LANES), 1) < SSD_N
    bd = row_lo == lane_lo
    dt0 = 2 * 2 * GDN_H

    def run(d, x_ref, sm_ref, st_ref, o_ref):
        mask = tri_ref[d] > 0.0
        a_c = -jnp.exp(alog_c_ref[d])
        a_r = -jnp.exp(alog_r_ref[d])
        for ci in _chunk_order(d, x_ref.shape[0] // c):
            rs = slice(ci * c, (ci + 1) * c)
            dtc = _softplus(sm_ref[rs, dt0 + d * SSD_H:dt0 + (d + 1) * SSD_H] + dtb_c_ref[d])
            dtr = _softplus(st_ref[dt0 + d * SSD_H:dt0 + (d + 1) * SSD_H, rs] + dtb_r_ref[d])
            cs_c = _dot(tri_ref[d], dtc * a_c, HI)
            cs_r = _dot(dtr * a_r, trit_ref[d], HI)
            tot = cs_c[c - 1:c, :] if d == 0 else cs_c[0:1, :]
            e_c = jnp.exp(cs_c)
            dec_c = jnp.exp(tot - cs_c) * dtc
            e_tot = jnp.exp(tot)
            x = x_ref[rs, 0:xw]
            bm = x_ref[rs, xw:xw + gw]
            cm = x_ref[rs, xw + gw:xw + 2 * gw]
            bmb = bm.astype(BF16)
            b_roll = pltpu.roll(bm, SSD_N, axis=1)
            c_roll = pltpu.roll(cm, SSD_N, axis=1)
            for g in range(SSD_G):
                keep = lo if g == 0 else jnp.logical_not(lo)
                cb = _dot_nt(jnp.where(keep, cm, 0.0).astype(BF16), bmb)
                c_dup = jnp.where(keep, cm, c_roll)
                b_dup = jnp.where(keep, bm, b_roll)
                for m in range(g * 2, g * 2 + 2):
                    h0, h1 = 2 * m, 2 * m + 1
                    sc = []
                    for hd in (h0, h1):
                        seg = cs_c[:, hd:hd + 1] - cs_r[hd:hd + 1, :]
                        lmat = jnp.where(mask, jnp.exp(jnp.where(mask, seg, 0.0)), 0.0)
                        sc.append(cb * lmat * dtr[hd:hd + 1, :])
                    scb = jnp.concatenate(sc, axis=1).astype(BF16)
                    ls = slice(m * LANES, (m + 1) * LANES)
                    xp = x[:, ls]
                    xs = jnp.concatenate([jnp.where(lo, xp, 0.0), jnp.where(lo, 0.0, xp)], axis=0).astype(BF16)
                    st = s_ref[d, m]
                    e_pair = jnp.where(lo, e_c[:, h0:h0 + 1], e_c[:, h1:h1 + 1])
                    y = _dot(scb, xs) + _dot((c_dup * e_pair).astype(BF16), st.astype(BF16))
                    if d == 0:
                        y = y + dskip_ref[:, ls] * xp
                    o_ref[rs, ls] = y
                    dec_pair = jnp.where(lo, dec_c[:, h0:h0 + 1], dec_c[:, h1:h1 + 1])
                    upd = _dot_tn((b_dup * dec_pair).astype(BF16), xp.astype(BF16))
                    sdec = jnp.where(row_lo, e_tot[:, h0:h0 + 1], e_tot[:, h1:h1 + 1])
                    s_ref[d, m] = sdec * st + jnp.where(bd, upd, 0.0)

    run(0, xf_ref, smf_ref, stf_ref, yf_o)
    run(1, xb_ref, smb_ref, stb_ref, yb_o)


def _tri_tables(c):
    pos = jnp.arange(c)
    lower = (pos[:, None] >= pos[None, :]).astype(F32)
    tri = jnp.stack([lower, lower.T])
    trit = jnp.stack([lower.T, lower])
    return tri, trit


def _ssd_call(xbc, small, small_t, dt_bias, a_log, d_skip, n_ctx):
    lt = xbc.shape[0]
    r = MIX_ROWS
    nb, ncb = lt // r, n_ctx // r
    tri, trit = _tri_tables(SSD_CHUNK)
    params = (dt_bias[:, None, :], dt_bias[:, :, None], a_log[:, None, :], a_log[:, :, None],
              jnp.repeat(d_skip, SSD_P)[None, :])
    xs = _dir_specs(r, SSD_XBC, ncb, nb)
    ss = _dir_specs(r, SMALL_COLS, ncb, nb)
    ts = _dir_specs_t(small_t.shape[0], r, ncb, nb)
    os_ = _dir_specs(r, SSD_H * SSD_P, ncb, nb)
    return pl.pallas_call(
        _ssd_kernel,
        grid=(nb,),
        in_specs=[*xs, *ss, *ts, _full(tri.shape), _full(trit.shape)] + [_full(p.shape) for p in params],
        out_specs=list(os_),
        out_shape=[jax.ShapeDtypeStruct((lt, SSD_H * SSD_P), F32)] * 2,
        scratch_shapes=[pltpu.VMEM((2, SSD_H // 2, 2 * SSD_N, 2 * SSD_P), F32)],
        compiler_params=_cparams(("arbitrary",)),
        name="ssd_scan",
    )(xbc, xbc, small, small, small_t, small_t, tri, trit, *params)


def _gdn_kernel(xf_ref, xb_ref, smf_ref, smb_ref, stf_ref, stb_ref, tri_ref, trit_ref,
                dtb_c_ref, dtb_r_ref, alog_c_ref, alog_r_ref, yf_o, yb_o, s_ref):
    @pl.when(pl.program_id(0) == 0)
    def _():
        s_ref[...] = jnp.zeros_like(s_ref)

    c = GDN_CHUNK
    kw = GDN_H * GDN_DK
    ri = lax.broadcasted_iota(jnp.int32, (c, c), 0)
    ci_ = lax.broadcasted_iota(jnp.int32, (c, c), 1)
    eye = (ri == ci_).astype(F32)

    def run(d, x_ref, sm_ref, st_ref, o_ref):
        incl = tri_ref[d] > 0.0
        strict = jnp.logical_and(incl, ri != ci_)
        for ck in _chunk_order(d, x_ref.shape[0] // c):
            rs = slice(ck * c, (ck + 1) * c)
            a_c = sm_ref[rs, d * GDN_H:(d + 1) * GDN_H]
            b_c = sm_ref[rs, 2 * GDN_H + d * GDN_H:2 * GDN_H + (d + 1) * GDN_H]
            a_r = st_ref[d * GDN_H:(d + 1) * GDN_H, rs]
            g_c = -jnp.exp(alog_c_ref[d]) * _softplus(a_c + dtb_c_ref[d])
            g_r = -jnp.exp(alog_r_ref[d]) * _softplus(a_r + dtb_r_ref[d])
            beta_c = _sigmoid(b_c)
            gcs_c = _dot(tri_ref[d], g_c, HI)
            gcs_r = _dot(g_r, trit_ref[d], HI)
            g_last = gcs_c[c - 1:c, :] if d == 0 else gcs_c[0:1, :]
            e_c = jnp.exp(gcs_c)
            kdec_c = jnp.exp(g_last - gcs_c)
            e_last = jnp.exp(g_last)
            for hd in range(GDN_H):
                qh = x_ref[rs, hd * GDN_DK:(hd + 1) * GDN_DK]
                kh = x_ref[rs, kw + hd * GDN_DK:kw + (hd + 1) * GDN_DK]
                vh = x_ref[rs, 2 * kw + hd * GDN_DV:2 * kw + (hd + 1) * GDN_DV]
                bc = beta_c[:, hd:hd + 1]
                seg = gcs_c[:, hd:hd + 1] - gcs_r[hd:hd + 1, :]
                dmat = jnp.where(incl, jnp.exp(jnp.where(incl, seg, 0.0)), 0.0)
                kbeta = kh * bc
                khb = kh.astype(BF16)
                lm = jnp.where(strict, _dot_nt(kbeta.astype(BF16), khb) * dmat, 0.0)
                attn = _dot_nt(qh.astype(BF16), khb) * dmat
                inv = eye - lm
                pw = lm
                for _ in range(int(math.log2(c)) - 1):
                    pw = _dot(pw, pw, HI)
                    inv = inv + _dot(inv, pw, HI)
                rhs = jnp.concatenate([vh * bc, kbeta * e_c[:, hd:hd + 1]], axis=1)
                sol = _dot(inv, rhs, HI)
                u = sol[:, 0:GDN_DV]
                w = sol[:, GDN_DV:GDN_DV + GDN_DK]
                s = s_ref[d, hd]
                sb = s.astype(BF16)
                vp = u - _dot(w.astype(BF16), sb)
                vpb = vp.astype(BF16)
                o_ref[rs, hd * GDN_DV:(hd + 1) * GDN_DV] = (
                    _dot((qh * e_c[:, hd:hd + 1]).astype(BF16), sb) + _dot(attn.astype(BF16), vpb))
                s_ref[d, hd] = e_last[:, hd:hd + 1] * s + _dot_tn((kh * kdec_c[:, hd:hd + 1]).astype(BF16), vpb)

    run(0, xf_ref, smf_ref, stf_ref, yf_o)
    run(1, xb_ref, smb_ref, stb_ref, yb_o)


def _gdn_call(qkv, small, small_t, a_log, dt_bias, n_ctx):
    lt = qkv.shape[0]
    r = MIX_ROWS
    nb, ncb = lt // r, n_ctx // r
    tri, trit = _tri_tables(GDN_CHUNK)
    params = (dt_bias[:, None, :], dt_bias[:, :, None], a_log[:, None, :], a_log[:, :, None])
    xs = _dir_specs(r, GDN_QKV, ncb, nb)
    ss = _dir_specs(r, SMALL_COLS, ncb, nb)
    ts = _dir_specs_t(small_t.shape[0], r, ncb, nb)
    os_ = _dir_specs(r, GDN_H * GDN_DV, ncb, nb)
    return pl.pallas_call(
        _gdn_kernel,
        grid=(nb,),
        in_specs=[*xs, *ss, *ts, _full(tri.shape), _full(trit.shape)] + [_full(p.shape) for p in params],
        out_specs=list(os_),
        out_shape=[jax.ShapeDtypeStruct((lt, GDN_H * GDN_DV), F32)] * 2,
        scratch_shapes=[pltpu.VMEM((2, GDN_H, GDN_DK, GDN_DV), F32)],
        compiler_params=_cparams(("arbitrary",)),
        name="gdn_scan",
    )(qkv, qkv, small, small, small_t, small_t, tri, trit, *params)


def _head_rms(y, n_heads, width):
    parts = []
    for hd in range(n_heads):
        t = y[:, hd * width:(hd + 1) * width]
        parts.append(t * lax.rsqrt(jnp.mean(t * t, axis=-1, keepdims=True) + EPS))
    return jnp.concatenate(parts, axis=1)


def _merge_kernel(x_ref, mod_ref, g_ref, wg_ref, wb_ref, wo_ref,
                  lf_ref, lb_ref, lg_ref, rf_ref, rb_ref, rg_ref, gf_ref, gb_ref, gz_ref, sf_ref, sb_ref, sz_ref,
                  rn_ref, gn_ref, sn_ref, o_ref, *, n_ctx_tiles, tile0):
    is_ctx = pl.program_id(0) + tile0 < n_ctx_tiles
    d = x_ref.shape[1]
    x = x_ref[...]
    h = _rms_mod(x, g_ref[...], _pick_mod(mod_ref, is_ctx, 0, d), _pick_mod(mod_ref, is_ctx, 1, d)).astype(BF16)
    ys = (
        (lf_ref[...] + lb_ref[...]) * _gelu_tanh(lg_ref[...]),
        _head_rms(rf_ref[...] + rb_ref[...], RET_H, RET_DV) * rn_ref[...] * _silu(rg_ref[:, 2 * RET_H * RET_DK + RET_H * RET_DV:]),
        _head_rms(gf_ref[...] + gb_ref[...], GDN_H, GDN_DV) * gn_ref[...] * _silu(gz_ref[...]),
    )
    ssd = (sf_ref[...] + sb_ref[...]) * _silu(sz_ref[...])
    ys = ys + (ssd * lax.rsqrt(jnp.mean(ssd * ssd, axis=-1, keepdims=True) + EPS) * sn_ref[...],)
    merged = None
    for nb_, y in enumerate(ys):
        gate = _sigmoid(_dot(h, wg_ref[:, nb_ * d:(nb_ + 1) * d]))
        t = gate * _dot(y.astype(BF16), wb_ref[nb_])
        merged = t if merged is None else merged + t
    out = _dot(merged.astype(BF16), wo_ref[...])
    o_ref[...] = x + _pick_mod(mod_ref, is_ctx, 2, d) * out


def _merge_call(x_all, mod, g, w_gate, w_branch, w_out, branches, norms, n_ctx, row_start):
    lt, d = x_all.shape
    tm = ROW_TILE
    t0 = row_start // tm
    nt = lt // tm - t0
    row = lambda wd: pl.BlockSpec((tm, wd), lambda i: (i + t0, 0))
    ret_g = branches[5]
    in_specs = [row(d), _full(mod.shape), _full(g.shape), _full(w_gate.shape), _full(w_branch.shape), _full(w_out.shape)]
    in_specs += [row(b.shape[1]) for b in branches]
    in_specs += [_full(n.shape) for n in norms]
    del ret_g
    return pl.pallas_call(
        functools.partial(_merge_kernel, n_ctx_tiles=n_ctx // tm, tile0=t0),
        grid=(nt,),
        in_specs=in_specs,
        out_specs=pl.BlockSpec((tm, d), lambda i: (i, 0)),
        out_shape=jax.ShapeDtypeStruct((nt * tm, d), F32),
        compiler_params=_cparams(("parallel",)),
        name="merge_out",
    )(x_all, mod, g, w_gate, w_branch, w_out, *branches, *norms)


def _ffn_kernel(x_ref, mod_ref, g_ref, wg_ref, wu_ref, wd_ref, fn_ref, o_ref, *, n_ctx_tiles, final):
    is_ctx = pl.program_id(0) < n_ctx_tiles
    d = x_ref.shape[1]
    x = x_ref[...]
    h = _rms_mod(x, g_ref[...], _pick_mod(mod_ref, is_ctx, 3, d), _pick_mod(mod_ref, is_ctx, 4, d)).astype(BF16)
    act = (_silu(_dot(h, wg_ref[...])) * _dot(h, wu_ref[...])).astype(BF16)
    y = x + _pick_mod(mod_ref, is_ctx, 5, d) * _dot(act, wd_ref[...])
    if final:
        y = y * lax.rsqrt(jnp.mean(y * y, axis=-1, keepdims=True) + EPS) * fn_ref[...]
    o_ref[...] = y


def _ffn_call(x_rows, mod, g, wg, wu, wd, final_g, n_ctx_rows, final):
    n, d = x_rows.shape
    tm = ROW_TILE
    row = pl.BlockSpec((tm, d), lambda i: (i, 0))
    return pl.pallas_call(
        functools.partial(_ffn_kernel, n_ctx_tiles=n_ctx_rows // tm, final=final),
        grid=(n // tm,),
        in_specs=[row, _full(mod.shape), _full(g.shape), _full(wg.shape), _full(wu.shape), _full(wd.shape), _full(final_g.shape)],
        out_specs=row,
        out_shape=jax.ShapeDtypeStruct((n, d), F32),
        compiler_params=_cparams(("parallel",)),
        name="dense_swiglu",
    )(x_rows, mod, g, wg, wu, wd, final_g)


def _router_kernel(x_ref, mod_ref, g_ref, rt_ref, tri_ref, h_o, gate_o, slot_o, cnt_o):
    d = x_ref.shape[1]
    h = _rms_mod(x_ref[...], g_ref[...], mod_ref[0:1, 3 * d:4 * d], mod_ref[0:1, 4 * d:5 * d])
    h_o[...] = h.astype(BF16)
    logits = _dot_nt(rt_ref[...], h, HI)
    e, b = logits.shape
    eid = lax.broadcasted_iota(jnp.int32, (e, b), 0)
    m1 = jnp.max(logits, axis=0, keepdims=True)
    i1 = jnp.min(jnp.where(logits == m1, eid, e), axis=0, keepdims=True)
    rest = jnp.where(eid == i1, -jnp.inf, logits)
    m2 = jnp.max(rest, axis=0, keepdims=True)
    i2 = jnp.min(jnp.where(rest == m2, eid, e), axis=0, keepdims=True)
    t = jnp.exp(m2 - m1)
    p1 = 1.0 / (1.0 + t)
    p2 = t / (1.0 + t)
    sel1 = eid == i1
    sel2 = eid == i2
    gate_o[...] = jnp.where(sel1, p1, jnp.where(sel2, p2, 0.0))
    sel = jnp.logical_or(sel1, sel2)
    rank = _dot(sel.astype(BF16), tri_ref[...])
    slot_o[...] = jnp.where(sel, rank, -1.0).astype(jnp.int32)
    cnt = jnp.sum(sel.astype(F32), axis=1, keepdims=True)
    cnt_o[...] = jnp.broadcast_to(cnt, (e, LANES))[None].astype(jnp.int32)


def _router_call(xl, mod, g, router_t):
    n, d = xl.shape
    b = MOE_BLOCK
    nblk = n // b
    e = router_t.shape[0]
    pos = jnp.arange(b)
    tri = (pos[:, None] < pos[None, :]).astype(BF16)
    return pl.pallas_call(
        _router_kernel,
        grid=(nblk,),
        in_specs=[pl.BlockSpec((b, d), lambda i: (i, 0)), _full(mod.shape), _full(g.shape), _full(router_t.shape), _full(tri.shape)],
        out_specs=[pl.BlockSpec((b, d), lambda i: (i, 0)), pl.BlockSpec((e, b), lambda i: (0, i)),
                   pl.BlockSpec((e, b), lambda i: (0, i)), pl.BlockSpec((1, e, LANES), lambda i: (i, 0, 0))],
        out_shape=[jax.ShapeDtypeStruct((n, d), BF16), jax.ShapeDtypeStruct((e, n), F32),
                   jax.ShapeDtypeStruct((e, n), jnp.int32), jax.ShapeDtypeStruct((nblk, e, LANES), jnp.int32)],
        compiler_params=_cparams(("parallel",)),
        name="moe_router",
    )(xl, mod, g, router_t, tri)


def _moe_kernel(nt_ref, x_ref, h_ref, gate_ref, slot_ref, mod_ref, fn_ref, wg_ref, wu_ref, wd_ref, o_ref,
                acc_ref, hs_ref, ys_ref, *, final):
    bi, ei, fi = pl.program_id(0), pl.program_id(1), pl.program_id(2)
    n_e, n_f = pl.num_programs(1), pl.num_programs(2)
    ts = MOE_SLOTS
    b, d = x_ref.shape
    nt = nt_ref[bi * n_e + ei]
    sid = lax.broadcasted_iota(jnp.int32, (ts, b), 0)

    def onehot(j):
        return slot_ref[pl.ds(ei, 1), :] == sid + j * ts

    @pl.when(jnp.logical_and(ei == 0, fi == 0))
    def _():
        acc_ref[...] = jnp.zeros_like(acc_ref)

    @pl.when(fi == 0)
    def _():
        def gather(j, carry):
            oh = onehot(j).astype(BF16)
            hs_ref[j] = _dot(oh, h_ref[...]).astype(BF16)
            return carry
        lax.fori_loop(0, nt, gather, 0)

    def expert(j, carry):
        hs = hs_ref[j]
        act = (_silu(_dot(hs, wg_ref[0])) * _dot(hs, wu_ref[0])).astype(BF16)
        y = _dot(act, wd_ref[0])

        @pl.when(fi == 0)
        def _():
            ys_ref[j] = y

        @pl.when(fi != 0)
        def _():
            ys_ref[j] = ys_ref[j] + y
        return carry
    lax.fori_loop(0, nt, expert, 0)

    @pl.when(fi == n_f - 1)
    def _():
        def scatter(j, carry):
            oh = onehot(j)
            gs = jnp.sum(jnp.where(oh, gate_ref[pl.ds(ei, 1), :], 0.0), axis=1, keepdims=True)
            acc_ref[...] += _dot_tn(oh.astype(BF16), (ys_ref[j] * gs).astype(BF16))
            return carry
        lax.fori_loop(0, nt, scatter, 0)

    @pl.when(jnp.logical_and(ei == n_e - 1, fi == n_f - 1))
    def _():
        y = x_ref[...] + mod_ref[0:1, 5 * d:6 * d] * acc_ref[...]
        if final:
            y = y * lax.rsqrt(jnp.mean(y * y, axis=-1, keepdims=True) + EPS) * fn_ref[...]
        o_ref[...] = y


def _moe_call(xl, h2, gate_t, slot_t, ntiles, mod, final_g, wg, wu, wd, final):
    n, d = xl.shape
    b = MOE_BLOCK
    e, _, f = wg.shape
    fs = f // MOE_FSPLIT
    grid_spec = pltpu.PrefetchScalarGridSpec(
        num_scalar_prefetch=1,
        grid=(n // b, e, MOE_FSPLIT),
        in_specs=[pl.BlockSpec((b, d), lambda i, j, k, nt: (i, 0)),
                  pl.BlockSpec((b, d), lambda i, j, k, nt: (i, 0)),
                  pl.BlockSpec((e, b), lambda i, j, k, nt: (0, i)),
                  pl.BlockSpec((e, b), lambda i, j, k, nt: (0, i)),
                  pl.BlockSpec(mod.shape, lambda i, j, k, nt: (0, 0)),
                  pl.BlockSpec(final_g.shape, lambda i, j, k, nt: (0, 0)),
                  pl.BlockSpec((1, d, fs), lambda i, j, k, nt: (j, 0, k)),
                  pl.BlockSpec((1, d, fs), lambda i, j, k, nt: (j, 0, k)),
                  pl.BlockSpec((1, fs, d), lambda i, j, k, nt: (j, k, 0))],
        out_specs=pl.BlockSpec((b, d), lambda i, j, k, nt: (i, 0)),
        scratch_shapes=[pltpu.VMEM((b, d), F32), pltpu.VMEM((b // MOE_SLOTS, MOE_SLOTS, d), BF16),
                        pltpu.VMEM((b // MOE_SLOTS, MOE_SLOTS, d), F32)],
    )
    return pl.pallas_call(
        functools.partial(_moe_kernel, final=final),
        grid_spec=grid_spec,
        out_shape=jax.ShapeDtypeStruct((n, d), F32),
        compiler_params=_cparams(("parallel", "arbitrary", "arbitrary")),
        name="moe_experts",
    )(ntiles, xl, h2, gate_t, slot_t, mod, final_g, wg, wu, wd)


def _mix_column_order():
    off = np.concatenate([[0], np.cumsum(MIX_SPLITS)])
    seg = lambda k: np.arange(off[k], off[k + 1])
    half = RET_DK // 2
    deint = np.concatenate([np.arange(half) * 2, np.arange(half) * 2 + 1])
    qk_perm = (np.arange(RET_H)[:, None] * RET_DK + deint[None, :]).reshape(-1)
    small = np.concatenate([seg(8), seg(9), seg(12)])
    return np.concatenate([
        seg(0), seg(6), seg(11),
        seg(1),
        seg(2)[qk_perm], seg(3)[qk_perm], seg(4), seg(5),
        seg(7), seg(10),
        small, np.full(SMALL_COLS - small.size, -1),
    ])


def _block_diag(w):
    n, i, o = w.shape
    eye = jnp.eye(n, dtype=w.dtype)
    return (eye[:, None, :, None] * w[:, :, None, :]).reshape(n * i, n * o)


def _rotary_tables(n_lat, n_ctx):
    rows = n_lat // GRID_W
    row = jnp.repeat(jnp.arange(rows, dtype=F32), GRID_W)
    col = (jnp.arange(n_lat) % GRID_W).astype(F32)
    n_freq = RET_DK // 4
    freqs = ROPE_BASE ** (-jnp.arange(n_freq, dtype=F32) / n_freq)
    ang = jnp.concatenate([row[:, None] * freqs, col[:, None] * freqs], axis=-1)
    cos, sin = jnp.cos(ang), jnp.sin(ang)
    cos_t = jnp.tile(jnp.concatenate([cos, cos], axis=-1), (1, RET_H))
    sin_t = jnp.tile(jnp.concatenate([-sin, sin], axis=-1), (1, RET_H))
    width = RET_H * RET_DK
    cos_t = jnp.concatenate([jnp.ones((n_ctx, width), F32), cos_t], axis=0)
    sin_t = jnp.concatenate([jnp.zeros((n_ctx, width), F32), sin_t], axis=0)
    return cos_t, sin_t


def kernel(x, c, ctx, c_ctx, w_mod, b_mod, norm_mix, norm_ffn, w_in, lru_conv_w, lru_conv_b, lru_wa, lru_ba, lru_wx, lru_bx, lru_lambda, ret_norm, gdn_conv_w, gdn_a_log, gdn_dt_bias, gdn_norm, ssd_conv_w, ssd_conv_b, ssd_a_log, ssd_dt_bias, ssd_d, ssd_norm, w_branch, w_out, ffn_wg, ffn_wu, ffn_wd, moe_router, moe_wg, moe_wu, moe_wd, final_norm):
    assert x.shape[0] == 1 and c.shape[0] == 1 and ctx.shape[0] == 1
    depth = w_mod.shape[0]
    n_lat, d = x.shape[1], x.shape[2]
    n_ctx = ctx.shape[1]
    assert n_ctx % ROW_TILE == 0 and n_lat % ROW_TILE == 0 and n_ctx % MIX_ROWS == 0 and n_lat % MIX_ROWS == 0
    gate_cols = N_BRANCH * d
    order = _mix_column_order()
    order_idx = jnp.asarray(np.where(order < 0, 0, order) + gate_cols)
    order_keep = jnp.asarray((order >= 0).astype(np.float32))
    cos_t, sin_t = _rotary_tables(n_lat, n_ctx)
    ct = jnp.stack([c[0], c_ctx], axis=1)
    final_g = final_norm[None, :]

    x_all = jnp.concatenate([ctx[0], x[0]], axis=0)
    for layer in range(depth):
        ctx_out = layer < depth - 1
        last = layer == depth - 1
        mod = _mod_call(ct, w_mod[layer], b_mod[layer][None, :])
        w_mix = (w_in[layer][:, order_idx] * order_keep).astype(BF16)
        (p_conv, p_lg, p_ret, p_gz, p_sz, p_small) = _proj_call(
            x_all, mod, norm_mix[layer][None, :], w_mix, cos_t, sin_t, n_ctx)
        conv_w = jnp.concatenate([lru_conv_w[layer], gdn_conv_w[layer], ssd_conv_w[layer]], axis=1)
        conv_b = jnp.concatenate([lru_conv_b[layer], jnp.zeros((GDN_QKV,), F32), ssd_conv_b[layer]])[None, :]
        lru_u, gdn_qkv, ssd_xbc = _conv_call(p_conv, conv_w, conv_b, n_ctx)
        small_t = p_small[:, 0:32].T

        lru_w = jnp.stack([jnp.concatenate([_block_diag(lru_wa[layer, dd]), _block_diag(lru_wx[layer, dd])], axis=1)
                           for dd in range(2)]).astype(BF16)
        lru_b = jnp.concatenate([lru_ba[layer], lru_bx[layer]], axis=1)[:, None, :]
        lru_f, lru_b_ = _lru_call(lru_u, lru_w, lru_b, lru_lambda[layer][:, None, :], n_ctx)
        ret_f, ret_b = _ret_call(p_ret, n_ctx)
        gdn_f, gdn_b = _gdn_call(gdn_qkv, p_small, small_t, gdn_a_log[layer], gdn_dt_bias[layer], n_ctx)
        ssd_f, ssd_b = _ssd_call(ssd_xbc, p_small, small_t, ssd_dt_bias[layer], ssd_a_log[layer], ssd_d[layer], n_ctx)

        branches = (lru_f, lru_b_, p_lg, ret_f, ret_b, p_ret, gdn_f, gdn_b, p_gz, ssd_f, ssd_b, p_sz)
        norms = (ret_norm[layer][None, :], jnp.tile(gdn_norm[layer], GDN_H)[None, :], ssd_norm[layer][None, :])
        row_start = 0 if ctx_out else n_ctx
        x_rows = _merge_call(x_all, mod, norm_mix[layer][None, :], w_in[layer][:, :gate_cols].astype(BF16),
                             w_branch[layer].astype(BF16), w_out[layer].astype(BF16), branches, norms, n_ctx, row_start)
        n_ctx_rows = n_ctx - row_start
        j = layer // 2
        if layer % 2 == 0:
            x_rows = _ffn_call(x_rows, mod, norm_ffn[layer][None, :], ffn_wg[j].astype(BF16), ffn_wu[j].astype(BF16),
                               ffn_wd[j].astype(BF16), final_g, n_ctx_rows, last)
        else:
            assert not ctx_out, "expert layers that must also emit context tokens are not supported"
            assert x_rows.shape[0] % MOE_BLOCK == 0
            h2, gate_t, slot_t, cnt = _router_call(x_rows, mod, norm_ffn[layer][None, :], moe_router[j].T)
            ntiles = ((cnt[:, :, 0] + (MOE_SLOTS - 1)) // MOE_SLOTS).reshape(-1)
            x_rows = _moe_call(x_rows, h2, gate_t, slot_t, ntiles, mod, final_g, moe_wg[j].astype(BF16),
                               moe_wu[j].astype(BF16), moe_wd[j].astype(BF16), last)
        x_all = x_rows
    out = x_all[x_all.shape[0] - n_lat:]
    return out[None]
```

```python
import functools
import math

import numpy as np
import jax
import jax.numpy as jnp
from jax import lax
from jax.experimental import pallas as pl
from jax.experimental.pallas import tpu as pltpu

F32 = jnp.float32
BF16 = jnp.bfloat16
HI = lax.Precision.HIGHEST

EPS = 1e-6
GRID_W = 64
N_BRANCH = 4
BRANCH_W = 512
CONV_W = 4
LRU_W = 512
LRU_BLOCKS = 8
LRU_C = 8.0
RET_H, RET_DK, RET_DV, RET_CHUNK = 4, 64, 128, 128
ROPE_BASE = 10000.0
GDN_H, GDN_DK, GDN_DV, GDN_CHUNK = 4, 128, 128, 64
SSD_H, SSD_P, SSD_G, SSD_N, SSD_CHUNK = 8, 64, 2, 64, 128
N_EXPERTS = 8
GDN_QKV = 2 * GDN_H * GDN_DK + GDN_H * GDN_DV
SSD_XBC = SSD_H * SSD_P + 2 * SSD_G * SSD_N
MIX_SPLITS = (LRU_W, LRU_W, RET_H * RET_DK, RET_H * RET_DK, RET_H * RET_DV, RET_H * RET_DV,
              GDN_QKV, GDN_H * GDN_DV, 2 * GDN_H, 2 * GDN_H, SSD_H * SSD_P, SSD_XBC, 2 * SSD_H)

LANES = 128
VMEM_LIMIT = 56 * 1024 * 1024

ROW_TILE = 256
MIX_ROWS = 256
MOE_BLOCK = 1024
MOE_SLOTS = 320
MOE_FSPLIT = 2

CONV_COLS = LRU_W + GDN_QKV + SSD_XBC
RET_COLS = 2 * RET_H * RET_DK + 2 * RET_H * RET_DV
SMALL_COLS = LANES


def _cparams(sem):
    return pltpu.CompilerParams(dimension_semantics=sem, vmem_limit_bytes=VMEM_LIMIT)


def _dot(a, b, precision=None):
    return jnp.dot(a, b, preferred_element_type=F32, precision=precision)


def _dot_nt(a, b, precision=None):
    return lax.dot_general(a, b, (((1,), (1,)), ((), ())), preferred_element_type=F32, precision=precision)


def _dot_tn(a, b, precision=None):
    return lax.dot_general(a, b, (((0,), (0,)), ((), ())), preferred_element_type=F32, precision=precision)


def _sigmoid(x):
    return 1.0 / (1.0 + jnp.exp(-x))


def _silu(x):
    return x * _sigmoid(x)


def _softplus(x):
    return jnp.maximum(x, 0.0) + jnp.log1p(jnp.exp(-jnp.abs(x)))


def _gelu_tanh(x):
    return 0.5 * x * (1.0 + jnp.tanh(math.sqrt(2.0 / math.pi) * (x + 0.044715 * (x * x * x))))


def _rms_mod(x, g, shift, scale):
    ms = jnp.mean(x * x, axis=-1, keepdims=True)
    return (x * lax.rsqrt(ms + EPS) * g) * (1.0 + scale) + shift


def _pick_mod(mod_ref, is_ctx, k, d):
    return jnp.where(is_ctx, mod_ref[1:2, k * d:(k + 1) * d], mod_ref[0:1, k * d:(k + 1) * d])


def _full(shape):
    n = len(shape)
    return pl.BlockSpec(shape, lambda *_: (0,) * n)


def _mod_kernel(ct_ref, w_ref, b_ref, o_ref):
    s = _silu(ct_ref[...])
    w = w_ref[...]
    b = b_ref[...]
    o_ref[0:1, :] = jnp.sum(s[:, 0:1] * w, axis=0, keepdims=True) + b
    o_ref[1:2, :] = jnp.sum(s[:, 1:2] * w, axis=0, keepdims=True) + b


def _mod_call(ct, w, b):
    d, n = w.shape
    tn = 512
    return pl.pallas_call(
        _mod_kernel,
        grid=(n // tn,),
        in_specs=[_full((d, 2)), pl.BlockSpec((d, tn), lambda j: (0, j)), pl.BlockSpec((1, tn), lambda j: (0, j))],
        out_specs=pl.BlockSpec((2, tn), lambda j: (0, j)),
        out_shape=jax.ShapeDtypeStruct((2, n), F32),
        compiler_params=_cparams(("arbitrary",)),
        name="adaln_mod",
    )(ct, w, b)


def _proj_kernel(x_ref, mod_ref, g_ref, w_ref, cos_ref, sin_ref,
                 conv_o, lg_o, ret_o, gz_o, sz_o, sm_o, *, n_ctx_tiles):
    is_ctx = pl.program_id(0) < n_ctx_tiles
    d = x_ref.shape[1]
    tm = x_ref.shape[0]
    h = _rms_mod(x_ref[...], g_ref[...], _pick_mod(mod_ref, is_ctx, 0, d), _pick_mod(mod_ref, is_ctx, 1, d)).astype(BF16)

    def mm(a, b):
        return _dot(h, w_ref[:, a:b])

    c = 0
    conv_o[...] = mm(c, c + CONV_COLS)
    c += CONV_COLS
    lg_o[...] = mm(c, c + LRU_W)
    c += LRU_W
    qk_w = 2 * RET_H * RET_DK
    half = RET_DK // 2
    lane = lax.broadcasted_iota(jnp.int32, (tm, RET_H * RET_DK), 1)
    first = (lane % RET_DK) < half
    cos = cos_ref[...]
    sin = sin_ref[...]

    def rot(t):
        partner = jnp.where(first, pltpu.roll(t, RET_H * RET_DK - half, axis=1), pltpu.roll(t, half, axis=1))
        return t * cos + partner * sin

    ret_o[:, 0:qk_w // 2] = rot(mm(c, c + qk_w // 2)) * (RET_DK ** -0.5)
    ret_o[:, qk_w // 2:qk_w] = rot(mm(c + qk_w // 2, c + qk_w))
    ret_o[:, qk_w:RET_COLS] = mm(c + qk_w, c + RET_COLS)
    c += RET_COLS
    gz_o[...] = mm(c, c + GDN_H * GDN_DV)
    c += GDN_H * GDN_DV
    sz_o[...] = mm(c, c + SSD_H * SSD_P)
    c += SSD_H * SSD_P
    sm_o[...] = mm(c, c + SMALL_COLS)


def _proj_call(x_all, mod, g, w, cos_t, sin_t, n_ctx):
    lt, d = x_all.shape
    tm = ROW_TILE
    widths = (CONV_COLS, LRU_W, RET_COLS, GDN_H * GDN_DV, SSD_H * SSD_P, SMALL_COLS)
    row = lambda wd: pl.BlockSpec((tm, wd), lambda i: (i, 0))
    return pl.pallas_call(
        functools.partial(_proj_kernel, n_ctx_tiles=n_ctx // tm),
        grid=(lt // tm,),
        in_specs=[row(d), _full(mod.shape), _full(g.shape), _full(w.shape), row(cos_t.shape[1]), row(sin_t.shape[1])],
        out_specs=[row(wd) for wd in widths],
        out_shape=[jax.ShapeDtypeStruct((lt, wd), F32) for wd in widths],
        compiler_params=_cparams(("parallel",)),
        name="mix_proj",
    )(x_all, mod, g, w, cos_t, sin_t)


def _conv_kernel(u_ref, p_ref, n_ref, w_ref, b_ref, lru_o, gdn_o, ssd_o, *, seg_blocks):
    i = pl.program_id(0)
    r = u_ref.shape[0]
    has_prev = jnp.logical_and(i != 0, i != seg_blocks).astype(F32)
    has_next = jnp.logical_and(i != seg_blocks - 1, i != pl.num_programs(0) - 1).astype(F32)
    rows = lax.broadcasted_iota(jnp.int32, (r, 1), 0)
    halo = p_ref.shape[0]

    def conv(c0, c1):
        u = u_ref[:, c0:c1]
        pm = p_ref[halo - 1:halo, c0:c1] * has_prev
        n0 = n_ref[0:1, c0:c1] * has_next
        n1 = n_ref[1:2, c0:c1] * has_next
        um1 = jnp.where(rows == 0, pm, pltpu.roll(u, 1, axis=0))
        up1 = jnp.where(rows == r - 1, n0, pltpu.roll(u, r - 1, axis=0))
        up2 = jnp.where(rows == r - 2, n0, jnp.where(rows == r - 1, n1, pltpu.roll(u, r - 2, axis=0)))
        w = w_ref[:, c0:c1]
        return w[0:1] * um1 + w[1:2] * u + w[2:3] * up1 + w[3:4] * up2 + b_ref[:, c0:c1]

    step = 2 * LANES
    for c0 in range(0, LRU_W, step):
        lru_o[:, c0:c0 + step] = conv(c0, c0 + step)
    base = LRU_W
    for hd in range(2 * GDN_H):
        c0 = hd * GDN_DK
        t = _silu(conv(base + c0, base + c0 + GDN_DK))
        t = t * lax.rsqrt(jnp.sum(t * t, axis=-1, keepdims=True) + EPS)
        if hd < GDN_H:
            t = t * (GDN_DK ** -0.5)
        gdn_o[:, c0:c0 + GDN_DK] = t
    for c0 in range(2 * GDN_H * GDN_DK, GDN_QKV, step):
        gdn_o[:, c0:c0 + step] = _silu(conv(base + c0, base + c0 + step))
    base = LRU_W + GDN_QKV
    for c0 in range(0, SSD_XBC, step):
        ssd_o[:, c0:c0 + step] = _silu(conv(base + c0, base + c0 + step))


def _conv_call(u, w, b, n_ctx):
    lt, wd = u.shape
    r = ROW_TILE
    halo = 8
    nb = lt // r
    hb = r // halo
    widths = (LRU_W, GDN_QKV, SSD_XBC)
    return pl.pallas_call(
        functools.partial(_conv_kernel, seg_blocks=n_ctx // r),
        grid=(nb,),
        in_specs=[pl.BlockSpec((r, wd), lambda i: (i, 0)),
                  pl.BlockSpec((halo, wd), lambda i: (jnp.maximum(i * hb - 1, 0), 0)),
                  pl.BlockSpec((halo, wd), lambda i: (jnp.minimum((i + 1) * hb, nb * hb - 1), 0)),
                  _full(w.shape), _full(b.shape)],
        out_specs=[pl.BlockSpec((r, x), lambda i: (i, 0)) for x in widths],
        out_shape=[jax.ShapeDtypeStruct((lt, x), F32) for x in widths],
        compiler_params=_cparams(("parallel",)),
        name="dw_conv",
    )(u, u, u, w, b)


def _bwd_block(i, n_ctx_blocks, n_blocks):
    return jnp.where(i < n_ctx_blocks, n_ctx_blocks - 1 - i, n_blocks + n_ctx_blocks - 1 - i)


def _dir_specs(r, width, ncb, nb):
    return (pl.BlockSpec((r, width), lambda i: (i, 0)),
            pl.BlockSpec((r, width), lambda i: (_bwd_block(i, ncb, nb), 0)))


def _dir_specs_t(rows, r, ncb, nb):
    return (pl.BlockSpec((rows, r), lambda i: (0, i)),
            pl.BlockSpec((rows, r), lambda i: (0, _bwd_block(i, ncb, nb))))


def _chunk_order(d, n):
    return range(n) if d == 0 else range(n - 1, -1, -1)


def _lru_kernel(uf_ref, ub_ref, w_ref, b_ref, lam_ref, yf_o, yb_o, carry_ref):
    @pl.when(pl.program_id(0) == 0)
    def _():
        carry_ref[...] = jnp.zeros_like(carry_ref)

    r = uf_ref.shape[0]
    rows = lax.broadcasted_iota(jnp.int32, (r, 1), 0)

    def run(d, u_ref, o_ref):
        u = u_ref[...]
        gates = _sigmoid(_dot(u.astype(BF16), w_ref[d]) + b_ref[d])
        rg = gates[:, 0:LRU_W]
        ig = gates[:, LRU_W:2 * LRU_W]
        log_a = (-LRU_C) * rg * _softplus(-lam_ref[d])
        a = jnp.exp(log_a)
        b = jnp.sqrt(1.0 - jnp.exp(2.0 * log_a)) * (ig * u)
        sh = 1
        while sh < r:
            if d == 0:
                valid = rows >= sh
                a_s = pltpu.roll(a, sh, axis=0)
                b_s = pltpu.roll(b, sh, axis=0)
            else:
                valid = rows < r - sh
                a_s = pltpu.roll(a, r - sh, axis=0)
                b_s = pltpu.roll(b, r - sh, axis=0)
            b = jnp.where(valid, a * b_s + b, b)
            a = jnp.where(valid, a * a_s, a)
            sh *= 2
        h = a * carry_ref[d, 0:1, :] + b
        o_ref[...] = h
        carry_ref[d, 0:1, :] = h[r - 1:r, :] if d == 0 else h[0:1, :]

    run(0, uf_ref, yf_o)
    run(1, ub_ref, yb_o)


def _lru_call(u, w, b, lam, n_ctx):
    lt = u.shape[0]
    r = MIX_ROWS
    nb, ncb = lt // r, n_ctx // r
    fs, bs = _dir_specs(r, LRU_W, ncb, nb)
    return pl.pallas_call(
        _lru_kernel,
        grid=(nb,),
        in_specs=[fs, bs, _full(w.shape), _full(b.shape), _full(lam.shape)],
        out_specs=[fs, bs],
        out_shape=[jax.ShapeDtypeStruct((lt, LRU_W), F32)] * 2,
        scratch_shapes=[pltpu.VMEM((2, 8, LRU_W), F32)],
        compiler_params=_cparams(("arbitrary",)),
        name="rglru_scan",
    )(u, u, w, b, lam)


def _ret_kernel(xf_ref, xb_ref, dmat_ref, qd_ref, kd_ref, sdec_ref, bd_ref, yf_o, yb_o, s_ref):
    @pl.when(pl.program_id(0) == 0)
    def _():
        s_ref[...] = jnp.zeros_like(s_ref)

    c = RET_CHUNK
    qw = RET_H * RET_DK
    lane_head = lax.broadcasted_iota(jnp.int32, (c, qw), 1) // RET_DK

    def run(d, x_ref, o_ref):
        for ci in _chunk_order(d, x_ref.shape[0] // c):
            rs = slice(ci * c, (ci + 1) * c)
            q = x_ref[rs, 0:qw]
            k = x_ref[rs, qw:2 * qw]
            kb = k.astype(BF16)
            vb = x_ref[rs, 2 * qw:2 * qw + RET_H * RET_DV].astype(BF16)
            s = s_ref[d]
            y_inter = _dot((q * qd_ref[d]).astype(BF16), s.astype(BF16))
            upd = _dot_tn((k * kd_ref[d]).astype(BF16), vb)
            s_ref[d] = sdec_ref[...] * s + bd_ref[...] * upd
            for hd in range(RET_H):
                qh = jnp.where(lane_head == hd, q, 0.0).astype(BF16)
                sc = _dot_nt(qh, kb) * dmat_ref[d, hd]
                vs = slice(hd * RET_DV, (hd + 1) * RET_DV)
                o_ref[rs, vs] = _dot(sc.astype(BF16), vb[:, vs]) + y_inter[:, vs]

    run(0, xf_ref, yf_o)
    run(1, xb_ref, yb_o)


def _ret_tables():
    c = RET_CHUNK
    lg = jnp.log(1.0 - 2.0 ** (-5.0 - jnp.arange(RET_H, dtype=F32)))
    pos = jnp.arange(c, dtype=F32)
    dist = pos[:, None] - pos[None, :]
    d_f = jnp.where(dist >= 0, jnp.exp(jnp.maximum(dist, 0.0)[None] * lg[:, None, None]), 0.0)
    d_b = jnp.where(dist < 0, jnp.exp(jnp.maximum(-dist, 0.0)[None] * lg[:, None, None]), 0.0)
    dmat = jnp.stack([d_f, d_b])
    rep = lambda t: jnp.repeat(t, RET_DK, axis=1)
    qd = jnp.stack([rep(jnp.exp((pos + 1.0)[:, None] * lg)), rep(jnp.exp((c - pos)[:, None] * lg))])
    kd = jnp.stack([rep(jnp.exp((c - 1.0 - pos)[:, None] * lg)), rep(jnp.exp(pos[:, None] * lg))])
    hk = jnp.repeat(jnp.arange(RET_H), RET_DK)
    hv = jnp.repeat(jnp.arange(RET_H), RET_DV)
    bd = (hk[:, None] == hv[None, :]).astype(F32)
    sdec = jnp.broadcast_to(jnp.repeat(jnp.exp(c * lg), RET_DK)[:, None], bd.shape)
    return dmat, qd, kd, sdec, bd


def _ret_call(x, n_ctx):
    lt = x.shape[0]
    r = MIX_ROWS
    nb, ncb = lt // r, n_ctx // r
    tabs = _ret_tables()
    fs, bs = _dir_specs(r, 2 * RET_H * RET_DK + RET_H * RET_DV, ncb, nb)
    os_f, os_b = _dir_specs(r, RET_H * RET_DV, ncb, nb)
    return pl.pallas_call(
        _ret_kernel,
        grid=(nb,),
        in_specs=[fs, bs] + [_full(t.shape) for t in tabs],
        out_specs=[os_f, os_b],
        out_shape=[jax.ShapeDtypeStruct((lt, RET_H * RET_DV), F32)] * 2,
        scratch_shapes=[pltpu.VMEM((2, RET_H * RET_DK, RET_H * RET_DV), F32)],
        compiler_params=_cparams(("arbitrary",)),
        name="retention_scan",
    )(x, x, *tabs)


def _ssd_kernel(xf_ref, xb_ref, smf_ref, smb_ref, stf_ref, stb_ref, tri_ref, trit_ref,
                dtb_c_ref, dtb_r_ref, alog_c_ref, alog_r_ref, dskip_ref, yf_o, yb_o, s_ref):
    @pl.when(pl.program_id(0) == 0)
    def _():
        s_ref[...] = jnp.zeros_like(s_ref)

    c = SSD_CHUNK
    xw = SSD_H * SSD_P
    gw = SSD_G * SSD_N
    lo = lax.broadcasted_iota(jnp.int32, (c, LANES), 1) < SSD_N
    row_lo = lax.broadcasted_iota(jnp.int32, (LANES, 1), 0) < SSD_N
    lane_lo = lax.broadcasted_iota(jnp.int32, (1, LANES), 1) < SSD_N
    bd = row_lo == lane_lo
    dt0 = 2 * 2 * GDN_H

    def run(d, x_ref, sm_ref, st_ref, o_ref):
        mask = tri_ref[d] > 0.0
        a_c = -jnp.exp(alog_c_ref[d])
        a_r = -jnp.exp(alog_r_ref[d])
        for ci in _chunk_order(d, x_ref.shape[0] // c):
            rs = slice(ci * c, (ci + 1) * c)
            dtc = _softplus(sm_ref[rs, dt0 + d * SSD_H:dt0 + (d + 1) * SSD_H] + dtb_c_ref[d])
            dtr = _softplus(st_ref[dt0 + d * SSD_H:dt0 + (d + 1) * SSD_H, rs] + dtb_r_ref[d])
            cs_c = _dot(tri_ref[d], dtc * a_c, HI)
            cs_r = _dot(dtr * a_r, trit_ref[d], HI)
            tot = cs_c[c - 1:c, :] if d == 0 else cs_c[0:1, :]
            e_c = jnp.exp(cs_c)
            dec_c = jnp.exp(tot - cs_c) * dtc
            e_tot = jnp.exp(tot)
            x = x_ref[rs, 0:xw]
            bm = x_ref[rs, xw:xw + gw]
            cm = x_ref[rs, xw + gw:xw + 2 * gw]
            bmb = bm.astype(BF16)
            b_roll = pltpu.roll(bm, SSD_N, axis=1)
            c_roll = pltpu.roll(cm, SSD_N, axis=1)
            for g in range(SSD_G):
                keep = lo if g == 0 else jnp.logical_not(lo)
                cb = _dot_nt(jnp.where(keep, cm, 0.0).astype(BF16), bmb)
                c_dup = jnp.where(keep, cm, c_roll)
                b_dup = jnp.where(keep, bm, b_roll)
                for m in range(g * 2, g * 2 + 2):
                    h0, h1 = 2 * m, 2 * m + 1
                    sc = []
                    for hd in (h0, h1):
                        seg = cs_c[:, hd:hd + 1] - cs_r[hd:hd + 1, :]
                        lmat = jnp.where(mask, jnp.exp(jnp.where(mask, seg, 0.0)), 0.0)
                        sc.append(cb * lmat * dtr[hd:hd + 1, :])
                    scb = jnp.concatenate(sc, axis=1).astype(BF16)
                    ls = slice(m * LANES, (m + 1) * LANES)
                    xp = x[:, ls]
                    xs = jnp.concatenate([jnp.where(lo, xp, 0.0), jnp.where(lo, 0.0, xp)], axis=0).astype(BF16)
                    st = s_ref[d, m]
                    e_pair = jnp.where(lo, e_c[:, h0:h0 + 1], e_c[:, h1:h1 + 1])
                    y = _dot(scb, xs) + _dot((c_dup * e_pair).astype(BF16), st.astype(BF16))
                    if d == 0:
                        y = y + dskip_ref[:, ls] * xp
                    o_ref[rs, ls] = y
                    dec_pair = jnp.where(lo, dec_c[:, h0:h0 + 1], dec_c[:, h1:h1 + 1])
                    upd = _dot_tn((b_dup * dec_pair).astype(BF16), xp.astype(BF16))
                    sdec = jnp.where(row_lo, e_tot[:, h0:h0 + 1], e_tot[:, h1:h1 + 1])
                    s_ref[d, m] = sdec * st + jnp.where(bd, upd, 0.0)

    run(0, xf_ref, smf_ref, stf_ref, yf_o)
    run(1, xb_ref, smb_ref, stb_ref, yb_o)


def _tri_tables(c):
    pos = jnp.arange(c)
    lower = (pos[:, None] >= pos[None, :]).astype(F32)
    tri = jnp.stack([lower, lower.T])
    trit = jnp.stack([lower.T, lower])
    return tri, trit


def _ssd_call(xbc, small, small_t, dt_bias, a_log, d_skip, n_ctx):
    lt = xbc.shape[0]
    r = MIX_ROWS
    nb, ncb = lt // r, n_ctx // r
    tri, trit = _tri_tables(SSD_CHUNK)
    params = (dt_bias[:, None, :], dt_bias[:, :, None], a_log[:, None, :], a_log[:, :, None],
              jnp.repeat(d_skip, SSD_P)[None, :])
    xs = _dir_specs(r, SSD_XBC, ncb, nb)
    ss = _dir_specs(r, SMALL_COLS, ncb, nb)
    ts = _dir_specs_t(small_t.shape[0], r, ncb, nb)
    os_ = _dir_specs(r, SSD_H * SSD_P, ncb, nb)
    return pl.pallas_call(
        _ssd_kernel,
        grid=(nb,),
        in_specs=[*xs, *ss, *ts, _full(tri.shape), _full(trit.shape)] + [_full(p.shape) for p in params],
        out_specs=list(os_),
        out_shape=[jax.ShapeDtypeStruct((lt, SSD_H * SSD_P), F32)] * 2,
        scratch_shapes=[pltpu.VMEM((2, SSD_H // 2, 2 * SSD_N, 2 * SSD_P), F32)],
        compiler_params=_cparams(("arbitrary",)),
        name="ssd_scan",
    )(xbc, xbc, small, small, small_t, small_t, tri, trit, *params)


def _gdn_kernel(xf_ref, xb_ref, smf_ref, smb_ref, stf_ref, stb_ref, tri_ref, trit_ref,
                dtb_c_ref, dtb_r_ref, alog_c_ref, alog_r_ref, yf_o, yb_o, s_ref):
    @pl.when(pl.program_id(0) == 0)
    def _():
        s_ref[...] = jnp.zeros_like(s_ref)

    c = GDN_CHUNK
    kw = GDN_H * GDN_DK
    ri = lax.broadcasted_iota(jnp.int32, (c, c), 0)
    ci_ = lax.broadcasted_iota(jnp.int32, (c, c), 1)
    eye = (ri == ci_).astype(F32)

    nch = xf_ref.shape[0] // c
    refs = ((xf_ref, smf_ref, stf_ref, yf_o), (xb_ref, smb_ref, stb_ref, yb_o))
    incl = [tri_ref[d] > 0.0 for d in range(2)]
    strict = [jnp.logical_and(incl[d], ri != ci_) for d in range(2)]

    def rows(ck):
        return slice(ck * c, (ck + 1) * c)

    def q_of(d, ck, hd):
        return refs[d][0][rows(ck), hd * GDN_DK:(hd + 1) * GDN_DK]

    def k_of(d, ck, hd):
        return refs[d][0][rows(ck), kw + hd * GDN_DK:kw + (hd + 1) * GDN_DK]

    def v_of(d, ck, hd):
        return refs[d][0][rows(ck), 2 * kw + hd * GDN_DV:2 * kw + (hd + 1) * GDN_DV]

    sc = {}
    for d in range(2):
        _, sm_ref, st_ref, _ = refs[d]
        for ck in range(nch):
            rs = rows(ck)
            a_c = sm_ref[rs, d * GDN_H:(d + 1) * GDN_H]
            b_c = sm_ref[rs, 2 * GDN_H + d * GDN_H:2 * GDN_H + (d + 1) * GDN_H]
            a_r = st_ref[d * GDN_H:(d + 1) * GDN_H, rs]
            g_c = -jnp.exp(alog_c_ref[d]) * _softplus(a_c + dtb_c_ref[d])
            g_r = -jnp.exp(alog_r_ref[d]) * _softplus(a_r + dtb_r_ref[d])
            gcs_c = _dot(tri_ref[d], g_c, HI)
            gcs_r = _dot(g_r, trit_ref[d], HI)
            g_last = gcs_c[c - 1:c, :] if d == 0 else gcs_c[0:1, :]
            sc[d, ck] = dict(beta=_sigmoid(b_c), gcs_c=gcs_c, gcs_r=gcs_r, e_c=jnp.exp(gcs_c),
                             kdec=jnp.exp(g_last - gcs_c), e_last=jnp.exp(g_last))

    items = [(d, ck, hd) for ck in range(nch) for d in range(2) for hd in range(GDN_H)]

    lm, attn = {}, {}
    for it in items:
        d, ck, hd = it
        s_ = sc[d, ck]
        seg = s_["gcs_c"][:, hd:hd + 1] - s_["gcs_r"][hd:hd + 1, :]
        dmat = jnp.where(incl[d], jnp.exp(jnp.where(incl[d], seg, 0.0)), 0.0)
        kh = k_of(*it)
        khb = kh.astype(BF16)
        lm[it] = jnp.where(strict[d], _dot_nt((kh * s_["beta"][:, hd:hd + 1]).astype(BF16), khb) * dmat, 0.0)
        attn[it] = (_dot_nt(q_of(*it).astype(BF16), khb) * dmat).astype(BF16)

    inv = {it: eye - lm[it] for it in items}
    pw = {it: lm[it].astype(BF16) for it in items}
    for _ in range(int(math.log2(c)) - 1):
        pw = {it: _dot(pw[it], pw[it]).astype(BF16) for it in items}
        inv = {it: inv[it] + _dot(inv[it].astype(BF16), pw[it]) for it in items}

    u, w = {}, {}
    for it in items:
        d, ck, hd = it
        s_ = sc[d, ck]
        bc = s_["beta"][:, hd:hd + 1]
        rhs = jnp.concatenate([v_of(*it) * bc, k_of(*it) * (bc * s_["e_c"][:, hd:hd + 1])], axis=1)
        sol = _dot(inv[it].astype(BF16), rhs.astype(BF16))
        u[it] = sol[:, 0:GDN_DV]
        w[it] = sol[:, GDN_DV:GDN_DV + GDN_DK].astype(BF16)

    state = {(d, hd): s_ref[d, hd] for d in range(2) for hd in range(GDN_H)}
    for pos in range(nch):
        cur = [(d, pos if d == 0 else nch - 1 - pos, hd) for d in range(2) for hd in range(GDN_H)]
        sb = {it: state[it[0], it[2]].astype(BF16) for it in cur}
        ws = {it: _dot(w[it], sb[it]) for it in cur}
        yq = {it: _dot((q_of(*it) * sc[it[0], it[1]]["e_c"][:, it[2]:it[2] + 1]).astype(BF16), sb[it]) for it in cur}
        for it in cur:
            d, ck, hd = it
            s_ = sc[d, ck]
            vpb = (u[it] - ws[it]).astype(BF16)
            refs[d][3][rows(ck), hd * GDN_DV:(hd + 1) * GDN_DV] = yq[it] + _dot(attn[it], vpb)
            state[d, hd] = (s_["e_last"][:, hd:hd + 1] * state[d, hd]
                            + _dot_tn((k_of(*it) * s_["kdec"][:, hd:hd + 1]).astype(BF16), vpb))
    for (d, hd), s in state.items():
        s_ref[d, hd] = s


def _gdn_call(qkv, small, small_t, a_log, dt_bias, n_ctx):
    lt = qkv.shape[0]
    r = MIX_ROWS
    nb, ncb = lt // r, n_ctx // r
    tri, trit = _tri_tables(GDN_CHUNK)
    params = (dt_bias[:, None, :], dt_bias[:, :, None], a_log[:, None, :], a_log[:, :, None])
    xs = _dir_specs(r, GDN_QKV, ncb, nb)
    ss = _dir_specs(r, SMALL_COLS, ncb, nb)
    ts = _dir_specs_t(small_t.shape[0], r, ncb, nb)
    os_ = _dir_specs(r, GDN_H * GDN_DV, ncb, nb)
    return pl.pallas_call(
        _gdn_kernel,
        grid=(nb,),
        in_specs=[*xs, *ss, *ts, _full(tri.shape), _full(trit.shape)] + [_full(p.shape) for p in params],
        out_specs=list(os_),
        out_shape=[jax.ShapeDtypeStruct((lt, GDN_H * GDN_DV), F32)] * 2,
        scratch_shapes=[pltpu.VMEM((2, GDN_H, GDN_DK, GDN_DV), F32)],
        compiler_params=_cparams(("arbitrary",)),
        name="gdn_scan",
    )(qkv, qkv, small, small, small_t, small_t, tri, trit, *params)


def _head_rms(y, n_heads, width):
    parts = []
    for hd in range(n_heads):
        t = y[:, hd * width:(hd + 1) * width]
        parts.append(t * lax.rsqrt(jnp.mean(t * t, axis=-1, keepdims=True) + EPS))
    return jnp.concatenate(parts, axis=1)


def _merge_kernel(x_ref, mod_ref, g_ref, wg_ref, wb_ref, wo_ref,
                  lf_ref, lb_ref, lg_ref, rf_ref, rb_ref, rg_ref, gf_ref, gb_ref, gz_ref, sf_ref, sb_ref, sz_ref,
                  rn_ref, gn_ref, sn_ref, o_ref, *, n_ctx_tiles, tile0):
    is_ctx = pl.program_id(0) + tile0 < n_ctx_tiles
    d = x_ref.shape[1]
    x = x_ref[...]
    h = _rms_mod(x, g_ref[...], _pick_mod(mod_ref, is_ctx, 0, d), _pick_mod(mod_ref, is_ctx, 1, d)).astype(BF16)
    ys = (
        (lf_ref[...] + lb_ref[...]) * _gelu_tanh(lg_ref[...]),
        _head_rms(rf_ref[...] + rb_ref[...], RET_H, RET_DV) * rn_ref[...] * _silu(rg_ref[:, 2 * RET_H * RET_DK + RET_H * RET_DV:]),
        _head_rms(gf_ref[...] + gb_ref[...], GDN_H, GDN_DV) * gn_ref[...] * _silu(gz_ref[...]),
    )
    ssd = (sf_ref[...] + sb_ref[...]) * _silu(sz_ref[...])
    ys = ys + (ssd * lax.rsqrt(jnp.mean(ssd * ssd, axis=-1, keepdims=True) + EPS) * sn_ref[...],)
    merged = None
    for nb_, y in enumerate(ys):
        gate = _sigmoid(_dot(h, wg_ref[:, nb_ * d:(nb_ + 1) * d]))
        t = gate * _dot(y.astype(BF16), wb_ref[nb_])
        merged = t if merged is None else merged + t
    out = _dot(merged.astype(BF16), wo_ref[...])
    o_ref[...] = x + _pick_mod(mod_ref, is_ctx, 2, d) * out


def _merge_call(x_all, mod, g, w_gate, w_branch, w_out, branches, norms, n_ctx, row_start):
    lt, d = x_all.shape
    tm = ROW_TILE
    t0 = row_start // tm
    nt = lt // tm - t0
    row = lambda wd: pl.BlockSpec((tm, wd), lambda i: (i + t0, 0))
    ret_g = branches[5]
    in_specs = [row(d), _full(mod.shape), _full(g.shape), _full(w_gate.shape), _full(w_branch.shape), _full(w_out.shape)]
    in_specs += [row(b.shape[1]) for b in branches]
    in_specs += [_full(n.shape) for n in norms]
    del ret_g
    return pl.pallas_call(
        functools.partial(_merge_kernel, n_ctx_tiles=n_ctx // tm, tile0=t0),
        grid=(nt,),
        in_specs=in_specs,
        out_specs=pl.BlockSpec((tm, d), lambda i: (i, 0)),
        out_shape=jax.ShapeDtypeStruct((nt * tm, d), F32),
        compiler_params=_cparams(("parallel",)),
        name="merge_out",
    )(x_all, mod, g, w_gate, w_branch, w_out, *branches, *norms)


def _ffn_kernel(x_ref, mod_ref, g_ref, wg_ref, wu_ref, wd_ref, fn_ref, o_ref, *, n_ctx_tiles, final):
    is_ctx = pl.program_id(0) < n_ctx_tiles
    d = x_ref.shape[1]
    x = x_ref[...]
    h = _rms_mod(x, g_ref[...], _pick_mod(mod_ref, is_ctx, 3, d), _pick_mod(mod_ref, is_ctx, 4, d)).astype(BF16)
    act = (_silu(_dot(h, wg_ref[...])) * _dot(h, wu_ref[...])).astype(BF16)
    y = x + _pick_mod(mod_ref, is_ctx, 5, d) * _dot(act, wd_ref[...])
    if final:
        y = y * lax.rsqrt(jnp.mean(y * y, axis=-1, keepdims=True) + EPS) * fn_ref[...]
    o_ref[...] = y


def _ffn_call(x_rows, mod, g, wg, wu, wd, final_g, n_ctx_rows, final):
    n, d = x_rows.shape
    tm = ROW_TILE
    row = pl.BlockSpec((tm, d), lambda i: (i, 0))
    return pl.pallas_call(
        functools.partial(_ffn_kernel, n_ctx_tiles=n_ctx_rows // tm, final=final),
        grid=(n // tm,),
        in_specs=[row, _full(mod.shape), _full(g.shape), _full(wg.shape), _full(wu.shape), _full(wd.shape), _full(final_g.shape)],
        out_specs=row,
        out_shape=jax.ShapeDtypeStruct((n, d), F32),
        compiler_params=_cparams(("parallel",)),
        name="dense_swiglu",
    )(x_rows, mod, g, wg, wu, wd, final_g)


def _router_kernel(x_ref, mod_ref, g_ref, rt_ref, tri_ref, h_o, gate_o, slot_o, cnt_o):
    d = x_ref.shape[1]
    h = _rms_mod(x_ref[...], g_ref[...], mod_ref[0:1, 3 * d:4 * d], mod_ref[0:1, 4 * d:5 * d])
    h_o[...] = h.astype(BF16)
    logits = _dot_nt(rt_ref[...], h, HI)
    e, b = logits.shape
    eid = lax.broadcasted_iota(jnp.int32, (e, b), 0)
    m1 = jnp.max(logits, axis=0, keepdims=True)
    i1 = jnp.min(jnp.where(logits == m1, eid, e), axis=0, keepdims=True)
    rest = jnp.where(eid == i1, -jnp.inf, logits)
    m2 = jnp.max(rest, axis=0, keepdims=True)
    i2 = jnp.min(jnp.where(rest == m2, eid, e), axis=0, keepdims=True)
    t = jnp.exp(m2 - m1)
    p1 = 1.0 / (1.0 + t)
    p2 = t / (1.0 + t)
    sel1 = eid == i1
    sel2 = eid == i2
    gate_o[...] = jnp.where(sel1, p1, jnp.where(sel2, p2, 0.0))
    sel = jnp.logical_or(sel1, sel2)
    rank = _dot(sel.astype(BF16), tri_ref[...])
    slot_o[...] = jnp.where(sel, rank, -1.0).astype(jnp.int32)
    cnt = jnp.sum(sel.astype(F32), axis=1, keepdims=True)
    cnt_o[...] = jnp.broadcast_to(cnt, (e, LANES))[None].astype(jnp.int32)


def _router_call(xl, mod, g, router_t):
    n, d = xl.shape
    b = MOE_BLOCK
    nblk = n // b
    e = router_t.shape[0]
    pos = jnp.arange(b)
    tri = (pos[:, None] < pos[None, :]).astype(BF16)
    return pl.pallas_call(
        _router_kernel,
        grid=(nblk,),
        in_specs=[pl.BlockSpec((b, d), lambda i: (i, 0)), _full(mod.shape), _full(g.shape), _full(router_t.shape), _full(tri.shape)],
        out_specs=[pl.BlockSpec((b, d), lambda i: (i, 0)), pl.BlockSpec((e, b), lambda i: (0, i)),
                   pl.BlockSpec((e, b), lambda i: (0, i)), pl.BlockSpec((1, e, LANES), lambda i: (i, 0, 0))],
        out_shape=[jax.ShapeDtypeStruct((n, d), BF16), jax.ShapeDtypeStruct((e, n), F32),
                   jax.ShapeDtypeStruct((e, n), jnp.int32), jax.ShapeDtypeStruct((nblk, e, LANES), jnp.int32)],
        compiler_params=_cparams(("parallel",)),
        name="moe_router",
    )(xl, mod, g, router_t, tri)


def _moe_kernel(nt_ref, x_ref, h_ref, gate_ref, slot_ref, mod_ref, fn_ref, wg_ref, wu_ref, wd_ref, o_ref,
                hs_ref, ys_ref, *, final):
    bi, ei, fi = pl.program_id(0), pl.program_id(1), pl.program_id(2)
    n_e, n_f = pl.num_programs(1), pl.num_programs(2)
    ts = MOE_SLOTS
    b, d = x_ref.shape
    nt = nt_ref[bi * n_e + ei]
    sid = lax.broadcasted_iota(jnp.int32, (ts, b), 0)

    def onehot(j):
        return slot_ref[pl.ds(ei, 1), :] == sid + j * ts

    @pl.when(jnp.logical_and(ei == 0, fi == 0))
    def _():
        o_ref[...] = jnp.zeros_like(o_ref)

    @pl.when(fi == 0)
    def _():
        def gather(j, carry):
            oh = onehot(j).astype(BF16)
            hs_ref[j] = _dot(oh, h_ref[...]).astype(BF16)
            return carry
        lax.fori_loop(0, nt, gather, 0)

    def expert(j, carry):
        hs = hs_ref[j]
        act = (_silu(_dot(hs, wg_ref[0])) * _dot(hs, wu_ref[0])).astype(BF16)
        y = _dot(act, wd_ref[0])

        @pl.when(fi == 0)
        def _():
            ys_ref[j] = y

        @pl.when(fi != 0)
        def _():
            ys_ref[j] = ys_ref[j] + y
        return carry
    lax.fori_loop(0, nt, expert, 0)

    @pl.when(fi == n_f - 1)
    def _():
        def scatter(j, carry):
            oh = onehot(j)
            gs = jnp.sum(jnp.where(oh, gate_ref[pl.ds(ei, 1), :], 0.0), axis=1, keepdims=True)
            o_ref[...] += _dot_tn(oh.astype(BF16), (ys_ref[j] * gs).astype(BF16))
            return carry
        lax.fori_loop(0, nt, scatter, 0)

    @pl.when(jnp.logical_and(ei == n_e - 1, fi == n_f - 1))
    def _():
        y = x_ref[...] + mod_ref[0:1, 5 * d:6 * d] * o_ref[...]
        if final:
            y = y * lax.rsqrt(jnp.mean(y * y, axis=-1, keepdims=True) + EPS) * fn_ref[...]
        o_ref[...] = y


def _moe_call(xl, h2, gate_t, slot_t, ntiles, mod, final_g, wg, wu, wd, final):
    n, d = xl.shape
    b = MOE_BLOCK
    e, _, f = wg.shape
    fs = f // MOE_FSPLIT
    n_tiles_max = pl.cdiv(b, MOE_SLOTS)
    once = pl.Buffered(1)
    grid_spec = pltpu.PrefetchScalarGridSpec(
        num_scalar_prefetch=1,
        grid=(n // b, e, MOE_FSPLIT),
        in_specs=[pl.BlockSpec((b, d), lambda i, j, k, nt: (i, 0), pipeline_mode=once),
                  pl.BlockSpec((b, d), lambda i, j, k, nt: (i, 0), pipeline_mode=once),
                  pl.BlockSpec((e, b), lambda i, j, k, nt: (0, i)),
                  pl.BlockSpec((e, b), lambda i, j, k, nt: (0, i)),
                  pl.BlockSpec(mod.shape, lambda i, j, k, nt: (0, 0)),
                  pl.BlockSpec(final_g.shape, lambda i, j, k, nt: (0, 0)),
                  pl.BlockSpec((1, d, fs), lambda i, j, k, nt: (j, 0, k)),
                  pl.BlockSpec((1, d, fs), lambda i, j, k, nt: (j, 0, k)),
                  pl.BlockSpec((1, fs, d), lambda i, j, k, nt: (j, k, 0))],
        out_specs=pl.BlockSpec((b, d), lambda i, j, k, nt: (i, 0)),
        scratch_shapes=[pltpu.VMEM((n_tiles_max, MOE_SLOTS, d), BF16), pltpu.VMEM((n_tiles_max, MOE_SLOTS, d), F32)],
    )
    return pl.pallas_call(
        functools.partial(_moe_kernel, final=final),
        grid_spec=grid_spec,
        out_shape=jax.ShapeDtypeStruct((n, d), F32),
        compiler_params=_cparams(("parallel", "arbitrary", "arbitrary")),
        name="moe_experts",
    )(ntiles, xl, h2, gate_t, slot_t, mod, final_g, wg, wu, wd)


def _mix_column_order():
    off = np.concatenate([[0], np.cumsum(MIX_SPLITS)])
    seg = lambda k: np.arange(off[k], off[k + 1])
    half = RET_DK // 2
    deint = np.concatenate([np.arange(half) * 2, np.arange(half) * 2 + 1])
    qk_perm = (np.arange(RET_H)[:, None] * RET_DK + deint[None, :]).reshape(-1)
    small = np.concatenate([seg(8), seg(9), seg(12)])
    return np.concatenate([
        seg(0), seg(6), seg(11),
        seg(1),
        seg(2)[qk_perm], seg(3)[qk_perm], seg(4), seg(5),
        seg(7), seg(10),
        small, np.full(SMALL_COLS - small.size, -1),
    ])


def _block_diag(w):
    n, i, o = w.shape
    eye = jnp.eye(n, dtype=w.dtype)
    return (eye[:, None, :, None] * w[:, :, None, :]).reshape(n * i, n * o)


def _rotary_tables(n_lat, n_ctx):
    rows = n_lat // GRID_W
    row = jnp.repeat(jnp.arange(rows, dtype=F32), GRID_W)
    col = (jnp.arange(n_lat) % GRID_W).astype(F32)
    n_freq = RET_DK // 4
    freqs = ROPE_BASE ** (-jnp.arange(n_freq, dtype=F32) / n_freq)
    ang = jnp.concatenate([row[:, None] * freqs, col[:, None] * freqs], axis=-1)
    cos, sin = jnp.cos(ang), jnp.sin(ang)
    cos_t = jnp.tile(jnp.concatenate([cos, cos], axis=-1), (1, RET_H))
    sin_t = jnp.tile(jnp.concatenate([-sin, sin], axis=-1), (1, RET_H))
    width = RET_H * RET_DK
    cos_t = jnp.concatenate([jnp.ones((n_ctx, width), F32), cos_t], axis=0)
    sin_t = jnp.concatenate([jnp.zeros((n_ctx, width), F32), sin_t], axis=0)
    return cos_t, sin_t


def kernel(x, c, ctx, c_ctx, w_mod, b_mod, norm_mix, norm_ffn, w_in, lru_conv_w, lru_conv_b, lru_wa, lru_ba, lru_wx, lru_bx, lru_lambda, ret_norm, gdn_conv_w, gdn_a_log, gdn_dt_bias, gdn_norm, ssd_conv_w, ssd_conv_b, ssd_a_log, ssd_dt_bias, ssd_d, ssd_norm, w_branch, w_out, ffn_wg, ffn_wu, ffn_wd, moe_router, moe_wg, moe_wu, moe_wd, final_norm):
    assert x.shape[0] == 1 and c.shape[0] == 1 and ctx.shape[0] == 1
    depth = w_mod.shape[0]
    n_lat, d = x.shape[1], x.shape[2]
    n_ctx = ctx.shape[1]
    assert n_ctx % ROW_TILE == 0 and n_lat % ROW_TILE == 0 and n_ctx % MIX_ROWS == 0 and n_lat % MIX_ROWS == 0
    gate_cols = N_BRANCH * d
    order = _mix_column_order()
    order_idx = jnp.asarray(np.where(order < 0, 0, order) + gate_cols)
    order_keep = jnp.asarray((order >= 0).astype(np.float32))
    cos_t, sin_t = _rotary_tables(n_lat, n_ctx)
    ct = jnp.stack([c[0], c_ctx], axis=1)
    final_g = final_norm[None, :]

    x_all = jnp.concatenate([ctx[0], x[0]], axis=0)
    for layer in range(depth):
        ctx_out = layer < depth - 1
        last = layer == depth - 1
        mod = _mod_call(ct, w_mod[layer], b_mod[layer][None, :])
        w_mix = (w_in[layer][:, order_idx] * order_keep).astype(BF16)
        (p_conv, p_lg, p_ret, p_gz, p_sz, p_small) = _proj_call(
            x_all, mod, norm_mix[layer][None, :], w_mix, cos_t, sin_t, n_ctx)
        conv_w = jnp.concatenate([lru_conv_w[layer], gdn_conv_w[layer], ssd_conv_w[layer]], axis=1)
        conv_b = jnp.concatenate([lru_conv_b[layer], jnp.zeros((GDN_QKV,), F32), ssd_conv_b[layer]])[None, :]
        lru_u, gdn_qkv, ssd_xbc = _conv_call(p_conv, conv_w, conv_b, n_ctx)
        small_t = p_small[:, 0:32].T

        lru_w = jnp.stack([jnp.concatenate([_block_diag(lru_wa[layer, dd]), _block_diag(lru_wx[layer, dd])], axis=1)
                           for dd in range(2)]).astype(BF16)
        lru_b = jnp.concatenate([lru_ba[layer], lru_bx[layer]], axis=1)[:, None, :]
        lru_f, lru_b_ = _lru_call(lru_u, lru_w, lru_b, lru_lambda[layer][:, None, :], n_ctx)
        ret_f, ret_b = _ret_call(p_ret, n_ctx)
        gdn_f, gdn_b = _gdn_call(gdn_qkv, p_small, small_t, gdn_a_log[layer], gdn_dt_bias[layer], n_ctx)
        ssd_f, ssd_b = _ssd_call(ssd_xbc, p_small, small_t, ssd_dt_bias[layer], ssd_a_log[layer], ssd_d[layer], n_ctx)

        branches = (lru_f, lru_b_, p_lg, ret_f, ret_b, p_ret, gdn_f, gdn_b, p_gz, ssd_f, ssd_b, p_sz)
        norms = (ret_norm[layer][None, :], jnp.tile(gdn_norm[layer], GDN_H)[None, :], ssd_norm[layer][None, :])
        row_start = 0 if ctx_out else n_ctx
        x_rows = _merge_call(x_all, mod, norm_mix[layer][None, :], w_in[layer][:, :gate_cols].astype(BF16),
                             w_branch[layer].astype(BF16), w_out[layer].astype(BF16), branches, norms, n_ctx, row_start)
        n_ctx_rows = n_ctx - row_start
        j = layer // 2
        if layer % 2 == 0:
            x_rows = _ffn_call(x_rows, mod, norm_ffn[layer][None, :], ffn_wg[j].astype(BF16), ffn_wu[j].astype(BF16),
                               ffn_wd[j].astype(BF16), final_g, n_ctx_rows, last)
        else:
            assert not ctx_out, "expert layers that must also emit context tokens are not supported"
            assert x_rows.shape[0] % MOE_BLOCK == 0
            h2, gate_t, slot_t, cnt = _router_call(x_rows, mod, norm_ffn[layer][None, :], moe_router[j].T)
            ntiles = ((cnt[:, :, 0] + (MOE_SLOTS - 1)) // MOE_SLOTS).reshape(-1)
            x_rows = _moe_call(x_rows, h2, gate_t, slot_t, ntiles, mod, final_g, moe_wg[j].astype(BF16),
                               moe_wu[j].astype(BF16), moe_wd[j].astype(BF16), last)
        x_all = x_rows
    out = x_all[x_all.shape[0] - n_lat:]
    return out[None]
```

```python
import functools
import math

import numpy as np
import jax
import jax.numpy as jnp
from jax import lax
from jax.experimental import pallas as pl
from jax.experimental.pallas import tpu as pltpu

F32 = jnp.float32
BF16 = jnp.bfloat16
HI = lax.Precision.HIGHEST

EPS = 1e-6
GRID_W = 64
N_BRANCH = 4
BRANCH_W = 512
CONV_W = 4
LRU_W = 512
LRU_BLOCKS = 8
LRU_C = 8.0
RET_H, RET_DK, RET_DV, RET_CHUNK = 4, 64, 128, 128
ROPE_BASE = 10000.0
GDN_H, GDN_DK, GDN_DV, GDN_CHUNK = 4, 128, 128, 64
SSD_H, SSD_P, SSD_G, SSD_N, SSD_CHUNK = 8, 64, 2, 64, 128
N_EXPERTS = 8
GDN_QKV = 2 * GDN_H * GDN_DK + GDN_H * GDN_DV
SSD_XBC = SSD_H * SSD_P + 2 * SSD_G * SSD_N
MIX_SPLITS = (LRU_W, LRU_W, RET_H * RET_DK, RET_H * RET_DK, RET_H * RET_DV, RET_H * RET_DV,
              GDN_QKV, GDN_H * GDN_DV, 2 * GDN_H, 2 * GDN_H, SSD_H * SSD_P, SSD_XBC, 2 * SSD_H)

LANES = 128
SUBLANES = 8
VMEM_LIMIT = 56 * 1024 * 1024

ROW_TILE = 256
MIX_ROWS = 256
MOE_BLOCK = 1024
MOE_SLOTS = 256
MOE_FSPLIT = 2

CONV_COLS = LRU_W + GDN_QKV + SSD_XBC
RET_COLS = 2 * RET_H * RET_DK + 2 * RET_H * RET_DV
SMALL_COLS = LANES


def _cparams(sem):
    return pltpu.CompilerParams(dimension_semantics=sem, vmem_limit_bytes=VMEM_LIMIT)


def _dot(a, b, precision=None):
    return jnp.dot(a, b, preferred_element_type=F32, precision=precision)


def _dot_nt(a, b, precision=None):
    return lax.dot_general(a, b, (((1,), (1,)), ((), ())), preferred_element_type=F32, precision=precision)


def _dot_tn(a, b, precision=None):
    return lax.dot_general(a, b, (((0,), (0,)), ((), ())), preferred_element_type=F32, precision=precision)


def _sigmoid(x):
    return 0.5 * jnp.tanh(0.5 * x) + 0.5


def _silu(x):
    return x * _sigmoid(x)


def _softplus(x):
    return jnp.maximum(x, 0.0) + jnp.log1p(jnp.exp(-jnp.abs(x)))


def _gelu_tanh(x):
    return 0.5 * x * (1.0 + jnp.tanh(math.sqrt(2.0 / math.pi) * (x + 0.044715 * (x * x * x))))


def _rms_mod(x, g, shift, scale):
    ms = jnp.mean(x * x, axis=-1, keepdims=True)
    return (x * lax.rsqrt(ms + EPS) * g) * (1.0 + scale) + shift


def _pick_mod(mod_ref, is_ctx, k, d):
    return jnp.where(is_ctx, mod_ref[1:2, k * d:(k + 1) * d], mod_ref[0:1, k * d:(k + 1) * d])


def _full(shape):
    n = len(shape)
    return pl.BlockSpec(shape, lambda *_: (0,) * n)


def _mod_kernel(ct_ref, w_ref, b_ref, o_ref):
    s = _silu(ct_ref[...])
    w = w_ref[...]
    b = b_ref[...]
    o_ref[0:1, :] = jnp.sum(s[:, 0:1] * w, axis=0, keepdims=True) + b
    o_ref[1:2, :] = jnp.sum(s[:, 1:2] * w, axis=0, keepdims=True) + b


def _mod_call(ct, w, b):
    d, n = w.shape
    tn = 512
    return pl.pallas_call(
        _mod_kernel,
        grid=(n // tn,),
        in_specs=[_full((d, 2)), pl.BlockSpec((d, tn), lambda j: (0, j)), pl.BlockSpec((1, tn), lambda j: (0, j))],
        out_specs=pl.BlockSpec((2, tn), lambda j: (0, j)),
        out_shape=jax.ShapeDtypeStruct((2, n), F32),
        compiler_params=_cparams(("arbitrary",)),
        name="adaln_mod",
    )(ct, w, b)


def _proj_kernel(x_ref, xp_ref, xn_ref, mod_ref, g_ref, w_ref, cos_ref, sin_ref, cw_ref, cb_ref,
                 lru_o, gdn_o, ssd_o, lg_o, ret_o, gz_o, sz_o, sm_o, *, n_ctx_tiles):
    i = pl.program_id(0)
    is_ctx = i < n_ctx_tiles
    d = x_ref.shape[1]
    tm = x_ref.shape[0]
    shift = _pick_mod(mod_ref, is_ctx, 0, d)
    scale = _pick_mod(mod_ref, is_ctx, 1, d)
    hf = _rms_mod(x_ref[...], g_ref[...], shift, scale)
    h = hf.astype(BF16)

    def mm(a, b):
        return _dot(h, w_ref[:, a:b])

    has_prev = jnp.logical_and(i != 0, i != n_ctx_tiles).astype(F32)
    has_next = jnp.logical_and(i != n_ctx_tiles - 1, i != pl.num_programs(0) - 1).astype(F32)
    h_ext = jnp.concatenate([_rms_mod(xp_ref[...], g_ref[...], shift, scale), hf,
                             _rms_mod(xn_ref[...], g_ref[...], shift, scale)], axis=0).astype(BF16)
    sub = lax.broadcasted_iota(jnp.int32, (1, SUBLANES, 1), 1)
    g = tm // SUBLANES

    def conv(c0, c1):
        wd = c1 - c0
        pe = _dot(h_ext, w_ref[:, c0:c1]).reshape(g + 2, SUBLANES, wd)
        u = pe[1:g + 1]
        ext = jnp.concatenate([pe[0:1] * has_prev, u, pe[g + 1:g + 2] * has_next], axis=0)
        r1 = pltpu.roll(ext, 1, axis=1)
        r7 = pltpu.roll(ext, SUBLANES - 1, axis=1)
        r6 = pltpu.roll(ext, SUBLANES - 2, axis=1)
        um1 = jnp.where(sub >= 1, r1[1:g + 1], r1[0:g])
        up1 = jnp.where(sub < SUBLANES - 1, r7[1:g + 1], r7[2:g + 2])
        up2 = jnp.where(sub < SUBLANES - 2, r6[1:g + 1], r6[2:g + 2])
        w = cw_ref[:, c0:c1]
        y = w[0:1] * um1 + w[1:2] * u + w[2:3] * up1 + w[3:4] * up2 + cb_ref[:, c0:c1]
        return y.reshape(tm, wd)

    step = 2 * LANES
    for c0 in range(0, LRU_W, step):
        lru_o[:, c0:c0 + step] = conv(c0, c0 + step)
    base = LRU_W
    for c0 in range(0, 2 * GDN_H * GDN_DK, step):
        t2 = _silu(conv(base + c0, base + c0 + step))
        for k in range(step // GDN_DK):
            t = t2[:, k * GDN_DK:(k + 1) * GDN_DK]
            t = t * lax.rsqrt(jnp.sum(t * t, axis=-1, keepdims=True) + EPS)
            if c0 < GDN_H * GDN_DK:
                t = t * (GDN_DK ** -0.5)
            gdn_o[:, c0 + k * GDN_DK:c0 + (k + 1) * GDN_DK] = t
    for c0 in range(2 * GDN_H * GDN_DK, GDN_QKV, step):
        gdn_o[:, c0:c0 + step] = _silu(conv(base + c0, base + c0 + step))
    base = LRU_W + GDN_QKV
    for c0 in range(0, SSD_XBC, step):
        ssd_o[:, c0:c0 + step] = _silu(conv(base + c0, base + c0 + step))

    c = CONV_COLS
    lg_o[...] = mm(c, c + LRU_W)
    c += LRU_W
    qk_w = 2 * RET_H * RET_DK
    half = RET_DK // 2
    lane = lax.broadcasted_iota(jnp.int32, (tm, RET_H * RET_DK), 1)
    first = (lane % RET_DK) < half
    cos = cos_ref[...]
    sin = sin_ref[...]

    def rot(t):
        partner = jnp.where(first, pltpu.roll(t, RET_H * RET_DK - half, axis=1), pltpu.roll(t, half, axis=1))
        return t * cos + partner * sin

    ret_o[:, 0:qk_w // 2] = rot(mm(c, c + qk_w // 2)) * (RET_DK ** -0.5)
    ret_o[:, qk_w // 2:qk_w] = rot(mm(c + qk_w // 2, c + qk_w))
    ret_o[:, qk_w:RET_COLS] = mm(c + qk_w, c + RET_COLS)
    c += RET_COLS
    gz_o[...] = mm(c, c + GDN_H * GDN_DV)
    c += GDN_H * GDN_DV
    sz_o[...] = mm(c, c + SSD_H * SSD_P)
    c += SSD_H * SSD_P
    sm_o[...] = mm(c, c + SMALL_COLS)


def _proj_call(x_all, mod, g, w, cos_t, sin_t, conv_w, conv_b, n_ctx):
    lt, d = x_all.shape
    tm = ROW_TILE
    nt = lt // tm
    hb = tm // SUBLANES
    widths = (LRU_W, GDN_QKV, SSD_XBC, LRU_W, RET_COLS, GDN_H * GDN_DV, SSD_H * SSD_P, SMALL_COLS)
    row = lambda wd: pl.BlockSpec((tm, wd), lambda i: (i, 0))
    return pl.pallas_call(
        functools.partial(_proj_kernel, n_ctx_tiles=n_ctx // tm),
        grid=(nt,),
        in_specs=[row(d),
                  pl.BlockSpec((SUBLANES, d), lambda i: (jnp.maximum(i * hb - 1, 0), 0)),
                  pl.BlockSpec((SUBLANES, d), lambda i: (jnp.minimum((i + 1) * hb, nt * hb - 1), 0)),
                  _full(mod.shape), _full(g.shape), _full(w.shape), row(cos_t.shape[1]), row(sin_t.shape[1]),
                  _full(conv_w.shape), _full(conv_b.shape)],
        out_specs=[row(wd) for wd in widths],
        out_shape=[jax.ShapeDtypeStruct((lt, wd), F32) for wd in widths],
        compiler_params=_cparams(("parallel",)),
        name="mix_proj",
    )(x_all, x_all, x_all, mod, g, w, cos_t, sin_t, conv_w, conv_b)


def _bwd_block(i, n_ctx_blocks, n_blocks):
    return jnp.where(i < n_ctx_blocks, n_ctx_blocks - 1 - i, n_blocks + n_ctx_blocks - 1 - i)


def _dir_specs(r, width, ncb, nb):
    return (pl.BlockSpec((r, width), lambda i: (i, 0)),
            pl.BlockSpec((r, width), lambda i: (_bwd_block(i, ncb, nb), 0)))


def _dir_specs_t(rows, r, ncb, nb):
    return (pl.BlockSpec((rows, r), lambda i: (0, i)),
            pl.BlockSpec((rows, r), lambda i: (0, _bwd_block(i, ncb, nb))))


def _chunk_order(d, n):
    return range(n) if d == 0 else range(n - 1, -1, -1)


def _lru_kernel(uf_ref, ub_ref, w_ref, b_ref, lam_ref, yf_o, yb_o, carry_ref):
    @pl.when(pl.program_id(0) == 0)
    def _():
        carry_ref[...] = jnp.zeros_like(carry_ref)

    r = uf_ref.shape[0]
    sub = lax.broadcasted_iota(jnp.int32, (1, SUBLANES, 1), 1)

    def run(d, u_ref, o_ref):
        u = u_ref[...]
        gates = _sigmoid(_dot(u.astype(BF16), w_ref[d]) + b_ref[d])
        rg = gates[:, 0:LRU_W]
        ig = gates[:, LRU_W:2 * LRU_W]
        log_a = (-LRU_C) * rg * _softplus(-lam_ref[d])
        a = jnp.exp(log_a)
        b = jnp.sqrt(1.0 - a * a) * (ig * u)
        a = a.reshape(r // SUBLANES, SUBLANES, LRU_W)
        b = b.reshape(r // SUBLANES, SUBLANES, LRU_W)
        sh = 1
        while sh < SUBLANES:
            valid = sub >= sh if d == 0 else sub < SUBLANES - sh
            shift = sh if d == 0 else SUBLANES - sh
            a_s = pltpu.roll(a, shift, axis=1)
            b_s = pltpu.roll(b, shift, axis=1)
            b = jnp.where(valid, a * b_s + b, b)
            a = jnp.where(valid, a * a_s, a)
            sh *= 2
        carry = carry_ref[d, 0:1, :]
        for g in _chunk_order(d, r // SUBLANES):
            gs = slice(g * SUBLANES, (g + 1) * SUBLANES)
            h = a[g] * carry + b[g]
            o_ref[gs, :] = h
            carry = h[SUBLANES - 1:SUBLANES, :] if d == 0 else h[0:1, :]
        carry_ref[d, 0:1, :] = carry

    run(0, uf_ref, yf_o)
    run(1, ub_ref, yb_o)


def _lru_call(u, w, b, lam, n_ctx):
    lt = u.shape[0]
    r = MIX_ROWS
    nb, ncb = lt // r, n_ctx // r
    fs, bs = _dir_specs(r, LRU_W, ncb, nb)
    return pl.pallas_call(
        _lru_kernel,
        grid=(nb,),
        in_specs=[fs, bs, _full(w.shape), _full(b.shape), _full(lam.shape)],
        out_specs=[fs, bs],
        out_shape=[jax.ShapeDtypeStruct((lt, LRU_W), F32)] * 2,
        scratch_shapes=[pltpu.VMEM((2, 8, LRU_W), F32)],
        compiler_params=_cparams(("arbitrary",)),
        name="rglru_scan",
    )(u, u, w, b, lam)


def _ret_kernel(xf_ref, xb_ref, dmat_ref, qd_ref, kd_ref, sdec_ref, bd_ref, yf_o, yb_o, s_ref):
    @pl.when(pl.program_id(0) == 0)
    def _():
        s_ref[...] = jnp.zeros_like(s_ref)

    c = RET_CHUNK
    qw = RET_H * RET_DK
    lane_head = lax.broadcasted_iota(jnp.int32, (c, qw), 1) // RET_DK

    def run(d, x_ref, o_ref):
        for ci in _chunk_order(d, x_ref.shape[0] // c):
            rs = slice(ci * c, (ci + 1) * c)
            q = x_ref[rs, 0:qw]
            k = x_ref[rs, qw:2 * qw]
            kb = k.astype(BF16)
            vb = x_ref[rs, 2 * qw:2 * qw + RET_H * RET_DV].astype(BF16)
            s = s_ref[d]
            y_inter = _dot((q * qd_ref[d]).astype(BF16), s.astype(BF16))
            upd = _dot_tn((k * kd_ref[d]).astype(BF16), vb)
            s_ref[d] = sdec_ref[...] * s + bd_ref[...] * upd
            for hd in range(RET_H):
                qh = jnp.where(lane_head == hd, q, 0.0).astype(BF16)
                sc = _dot_nt(qh, kb) * dmat_ref[d, hd]
                vs = slice(hd * RET_DV, (hd + 1) * RET_DV)
                o_ref[rs, vs] = _dot(sc.astype(BF16), vb[:, vs]) + y_inter[:, vs]

    run(0, xf_ref, yf_o)
    run(1, xb_ref, yb_o)


def _ret_tables():
    c = RET_CHUNK
    lg = jnp.log(1.0 - 2.0 ** (-5.0 - jnp.arange(RET_H, dtype=F32)))
    pos = jnp.arange(c, dtype=F32)
    dist = pos[:, None] - pos[None, :]
    d_f = jnp.where(dist >= 0, jnp.exp(jnp.maximum(dist, 0.0)[None] * lg[:, None, None]), 0.0)
    d_b = jnp.where(dist < 0, jnp.exp(jnp.maximum(-dist, 0.0)[None] * lg[:, None, None]), 0.0)
    dmat = jnp.stack([d_f, d_b])
    rep = lambda t: jnp.repeat(t, RET_DK, axis=1)
    qd = jnp.stack([rep(jnp.exp((pos + 1.0)[:, None] * lg)), rep(jnp.exp((c - pos)[:, None] * lg))])
    kd = jnp.stack([rep(jnp.exp((c - 1.0 - pos)[:, None] * lg)), rep(jnp.exp(pos[:, None] * lg))])
    hk = jnp.repeat(jnp.arange(RET_H), RET_DK)
    hv = jnp.repeat(jnp.arange(RET_H), RET_DV)
    bd = (hk[:, None] == hv[None, :]).astype(F32)
    sdec = jnp.broadcast_to(jnp.repeat(jnp.exp(c * lg), RET_DK)[:, None], bd.shape)
    return dmat, qd, kd, sdec, bd


def _ret_call(x, n_ctx):
    lt = x.shape[0]
    r = MIX_ROWS
    nb, ncb = lt // r, n_ctx // r
    tabs = _ret_tables()
    fs, bs = _dir_specs(r, 2 * RET_H * RET_DK + RET_H * RET_DV, ncb, nb)
    os_f, os_b = _dir_specs(r, RET_H * RET_DV, ncb, nb)
    return pl.pallas_call(
        _ret_kernel,
        grid=(nb,),
        in_specs=[fs, bs] + [_full(t.shape) for t in tabs],
        out_specs=[os_f, os_b],
        out_shape=[jax.ShapeDtypeStruct((lt, RET_H * RET_DV), F32)] * 2,
        scratch_shapes=[pltpu.VMEM((2, RET_H * RET_DK, RET_H * RET_DV), F32)],
        compiler_params=_cparams(("arbitrary",)),
        name="retention_scan",
    )(x, x, *tabs)


def _ssd_kernel(xf_ref, xb_ref, smf_ref, smb_ref, stf_ref, stb_ref, tri_ref, trit_ref,
                dtb_c_ref, dtb_r_ref, alog_c_ref, alog_r_ref, dskip_ref, yf_o, yb_o, s_ref):
    @pl.when(pl.program_id(0) == 0)
    def _():
        s_ref[...] = jnp.zeros_like(s_ref)

    c = SSD_CHUNK
    xw = SSD_H * SSD_P
    gw = SSD_G * SSD_N
    lo = lax.broadcasted_iota(jnp.int32, (c, LANES), 1) < SSD_N
    row_lo = lax.broadcasted_iota(jnp.int32, (LANES, 1), 0) < SSD_N
    lane_lo = lax.broadcasted_iota(jnp.int32, (1, LANES), 1) < SSD_N
    bd = row_lo == lane_lo
    dt0 = 2 * 2 * GDN_H

    nch = xf_ref.shape[0] // c
    n_pair = SSD_H // 2
    refs = ((xf_ref, smf_ref, stf_ref, yf_o), (xb_ref, smb_ref, stb_ref, yb_o))
    mask = [tri_ref[d] > 0.0 for d in range(2)]

    def rows(ck):
        return slice(ck * c, (ck + 1) * c)

    sc = {}
    for d in range(2):
        _, sm_ref, st_ref, _ = refs[d]
        a_c = -jnp.exp(alog_c_ref[d])
        a_r = -jnp.exp(alog_r_ref[d])
        for ck in range(nch):
            rs = rows(ck)
            dtc = _softplus(sm_ref[rs, dt0 + d * SSD_H:dt0 + (d + 1) * SSD_H] + dtb_c_ref[d])
            dtr = _softplus(st_ref[dt0 + d * SSD_H:dt0 + (d + 1) * SSD_H, rs] + dtb_r_ref[d])
            cs_c = _dot(tri_ref[d], dtc * a_c, HI)
            cs_r = _dot(dtr * a_r, trit_ref[d], HI)
            tot = cs_c[c - 1:c, :] if d == 0 else cs_c[0:1, :]
            sc[d, ck] = dict(dtr=dtr, cs_c=cs_c, cs_r=cs_r, e_c=jnp.exp(cs_c), dec_c=jnp.exp(tot - cs_c) * dtc,
                             e_tot=jnp.exp(tot))

    cb, c_dup, b_dup = {}, {}, {}
    for d in range(2):
        x_ref = refs[d][0]
        for ck in range(nch):
            rs = rows(ck)
            bm = x_ref[rs, xw:xw + gw]
            cm = x_ref[rs, xw + gw:xw + 2 * gw]
            bmb = bm.astype(BF16)
            b_roll = pltpu.roll(bm, SSD_N, axis=1)
            c_roll = pltpu.roll(cm, SSD_N, axis=1)
            for g in range(SSD_G):
                keep = lo if g == 0 else jnp.logical_not(lo)
                cb[d, ck, g] = _dot_nt(jnp.where(keep, cm, 0.0).astype(BF16), bmb)
                c_dup[d, ck, g] = jnp.where(keep, cm, c_roll)
                b_dup[d, ck, g] = jnp.where(keep, bm, b_roll)

    items = [(d, ck, m) for ck in range(nch) for d in range(2) for m in range(n_pair)]
    y_intra, upd, cq = {}, {}, {}
    for it in items:
        d, ck, m = it
        s_ = sc[d, ck]
        g = m // (n_pair // SSD_G)
        h0, h1 = 2 * m, 2 * m + 1
        scores = []
        for hd in (h0, h1):
            seg = s_["cs_c"][:, hd:hd + 1] - s_["cs_r"][hd:hd + 1, :]
            lmat = jnp.where(mask[d], jnp.exp(jnp.where(mask[d], seg, 0.0)), 0.0)
            scores.append(cb[d, ck, g] * lmat * s_["dtr"][hd:hd + 1, :])
        scb = jnp.concatenate(scores, axis=1).astype(BF16)
        ls = slice(m * LANES, (m + 1) * LANES)
        xp = refs[d][0][rows(ck), ls]
        xs = jnp.concatenate([jnp.where(lo, xp, 0.0), jnp.where(lo, 0.0, xp)], axis=0).astype(BF16)
        y = _dot(scb, xs)
        if d == 0:
            y = y + dskip_ref[:, ls] * xp
        y_intra[it] = y
        dec_pair = jnp.where(lo, s_["dec_c"][:, h0:h0 + 1], s_["dec_c"][:, h1:h1 + 1])
        upd[it] = jnp.where(bd, _dot_tn((b_dup[d, ck, g] * dec_pair).astype(BF16), xp.astype(BF16)), 0.0)
        e_pair = jnp.where(lo, s_["e_c"][:, h0:h0 + 1], s_["e_c"][:, h1:h1 + 1])
        cq[it] = (c_dup[d, ck, g] * e_pair).astype(BF16)

    state = {(d, m): s_ref[d, m] for d in range(2) for m in range(n_pair)}
    for pos in range(nch):
        for d in range(2):
            ck = pos if d == 0 else nch - 1 - pos
            e_tot = sc[d, ck]["e_tot"]
            for m in range(n_pair):
                it = (d, ck, m)
                st = state[d, m]
                refs[d][3][rows(ck), m * LANES:(m + 1) * LANES] = y_intra[it] + _dot(cq[it], st.astype(BF16))
                sdec = jnp.where(row_lo, e_tot[:, 2 * m:2 * m + 1], e_tot[:, 2 * m + 1:2 * m + 2])
                state[d, m] = sdec * st + upd[it]
    for (d, m), st in state.items():
        s_ref[d, m] = st


def _tri_tables(c):
    pos = jnp.arange(c)
    lower = (pos[:, None] >= pos[None, :]).astype(F32)
    tri = jnp.stack([lower, lower.T])
    trit = jnp.stack([lower.T, lower])
    return tri, trit


def _ssd_call(xbc, small, small_t, dt_bias, a_log, d_skip, n_ctx):
    lt = xbc.shape[0]
    r = MIX_ROWS
    nb, ncb = lt // r, n_ctx // r
    tri, trit = _tri_tables(SSD_CHUNK)
    params = (dt_bias[:, None, :], dt_bias[:, :, None], a_log[:, None, :], a_log[:, :, None],
              jnp.repeat(d_skip, SSD_P)[None, :])
    xs = _dir_specs(r, SSD_XBC, ncb, nb)
    ss = _dir_specs(r, SMALL_COLS, ncb, nb)
    ts = _dir_specs_t(small_t.shape[0], r, ncb, nb)
    os_ = _dir_specs(r, SSD_H * SSD_P, ncb, nb)
    return pl.pallas_call(
        _ssd_kernel,
        grid=(nb,),
        in_specs=[*xs, *ss, *ts, _full(tri.shape), _full(trit.shape)] + [_full(p.shape) for p in params],
        out_specs=list(os_),
        out_shape=[jax.ShapeDtypeStruct((lt, SSD_H * SSD_P), F32)] * 2,
        scratch_shapes=[pltpu.VMEM((2, SSD_H // 2, 2 * SSD_N, 2 * SSD_P), F32)],
        compiler_params=_cparams(("arbitrary",)),
        name="ssd_scan",
    )(xbc, xbc, small, small, small_t, small_t, tri, trit, *params)


def _gdn_kernel(xf_ref, xb_ref, smf_ref, smb_ref, stf_ref, stb_ref, tri_ref, trit_ref,
                dtb_c_ref, dtb_r_ref, alog_c_ref, alog_r_ref, yf_o, yb_o, s_ref):
    @pl.when(pl.program_id(0) == 0)
    def _():
        s_ref[...] = jnp.zeros_like(s_ref)

    c = GDN_CHUNK
    kw = GDN_H * GDN_DK
    ri = lax.broadcasted_iota(jnp.int32, (c, c), 0)
    ci_ = lax.broadcasted_iota(jnp.int32, (c, c), 1)
    eye = (ri == ci_).astype(F32)

    nch = xf_ref.shape[0] // c
    refs = ((xf_ref, smf_ref, stf_ref, yf_o), (xb_ref, smb_ref, stb_ref, yb_o))
    incl = [tri_ref[d] > 0.0 for d in range(2)]
    strict = [jnp.logical_and(incl[d], ri != ci_) for d in range(2)]

    def rows(ck):
        return slice(ck * c, (ck + 1) * c)

    def q_of(d, ck, hd):
        return refs[d][0][rows(ck), hd * GDN_DK:(hd + 1) * GDN_DK]

    def k_of(d, ck, hd):
        return refs[d][0][rows(ck), kw + hd * GDN_DK:kw + (hd + 1) * GDN_DK]

    def v_of(d, ck, hd):
        return refs[d][0][rows(ck), 2 * kw + hd * GDN_DV:2 * kw + (hd + 1) * GDN_DV]

    sc = {}
    for d in range(2):
        _, sm_ref, st_ref, _ = refs[d]
        for ck in range(nch):
            rs = rows(ck)
            a_c = sm_ref[rs, d * GDN_H:(d + 1) * GDN_H]
            b_c = sm_ref[rs, 2 * GDN_H + d * GDN_H:2 * GDN_H + (d + 1) * GDN_H]
            a_r = st_ref[d * GDN_H:(d + 1) * GDN_H, rs]
            g_c = -jnp.exp(alog_c_ref[d]) * _softplus(a_c + dtb_c_ref[d])
            g_r = -jnp.exp(alog_r_ref[d]) * _softplus(a_r + dtb_r_ref[d])
            gcs_c = _dot(tri_ref[d], g_c, HI)
            gcs_r = _dot(g_r, trit_ref[d], HI)
            g_last = gcs_c[c - 1:c, :] if d == 0 else gcs_c[0:1, :]
            sc[d, ck] = dict(beta=_sigmoid(b_c), gcs_c=gcs_c, gcs_r=gcs_r, e_c=jnp.exp(gcs_c),
                             kdec=jnp.exp(g_last - gcs_c), e_last=jnp.exp(g_last))

    items = [(d, ck, hd) for ck in range(nch) for d in range(2) for hd in range(GDN_H)]

    lm, attn = {}, {}
    for it in items:
        d, ck, hd = it
        s_ = sc[d, ck]
        seg = s_["gcs_c"][:, hd:hd + 1] - s_["gcs_r"][hd:hd + 1, :]
        dmat = jnp.where(incl[d], jnp.exp(jnp.where(incl[d], seg, 0.0)), 0.0)
        kh = k_of(*it)
        khb = kh.astype(BF16)
        lm[it] = jnp.where(strict[d], _dot_nt((kh * s_["beta"][:, hd:hd + 1]).astype(BF16), khb) * dmat, 0.0)
        attn[it] = (_dot_nt(q_of(*it).astype(BF16), khb) * dmat).astype(BF16)

    inv = {it: eye - lm[it] for it in items}
    pw = {it: lm[it].astype(BF16) for it in items}
    for _ in range(int(math.log2(c)) - 1):
        pw = {it: _dot(pw[it], pw[it]).astype(BF16) for it in items}
        inv = {it: inv[it] + _dot(inv[it].astype(BF16), pw[it]) for it in items}

    u, w = {}, {}
    for it in items:
        d, ck, hd = it
        s_ = sc[d, ck]
        bc = s_["beta"][:, hd:hd + 1]
        rhs = jnp.concatenate([v_of(*it) * bc, k_of(*it) * (bc * s_["e_c"][:, hd:hd + 1])], axis=1)
        sol = _dot(inv[it].astype(BF16), rhs.astype(BF16))
        u[it] = sol[:, 0:GDN_DV]
        w[it] = sol[:, GDN_DV:GDN_DV + GDN_DK].astype(BF16)

    state = {(d, hd): s_ref[d, hd] for d in range(2) for hd in range(GDN_H)}
    for pos in range(nch):
        cur = [(d, pos if d == 0 else nch - 1 - pos, hd) for d in range(2) for hd in range(GDN_H)]
        sb = {it: state[it[0], it[2]].astype(BF16) for it in cur}
        ws = {it: _dot(w[it], sb[it]) for it in cur}
        yq = {it: _dot((q_of(*it) * sc[it[0], it[1]]["e_c"][:, it[2]:it[2] + 1]).astype(BF16), sb[it]) for it in cur}
        for it in cur:
            d, ck, hd = it
            s_ = sc[d, ck]
            vpb = (u[it] - ws[it]).astype(BF16)
            refs[d][3][rows(ck), hd * GDN_DV:(hd + 1) * GDN_DV] = yq[it] + _dot(attn[it], vpb)
            state[d, hd] = (s_["e_last"][:, hd:hd + 1] * state[d, hd]
                            + _dot_tn((k_of(*it) * s_["kdec"][:, hd:hd + 1]).astype(BF16), vpb))
    for (d, hd), s in state.items():
        s_ref[d, hd] = s


def _gdn_call(qkv, small, small_t, a_log, dt_bias, n_ctx):
    lt = qkv.shape[0]
    r = MIX_ROWS
    nb, ncb = lt // r, n_ctx // r
    tri, trit = _tri_tables(GDN_CHUNK)
    params = (dt_bias[:, None, :], dt_bias[:, :, None], a_log[:, None, :], a_log[:, :, None])
    xs = _dir_specs(r, GDN_QKV, ncb, nb)
    ss = _dir_specs(r, SMALL_COLS, ncb, nb)
    ts = _dir_specs_t(small_t.shape[0], r, ncb, nb)
    os_ = _dir_specs(r, GDN_H * GDN_DV, ncb, nb)
    return pl.pallas_call(
        _gdn_kernel,
        grid=(nb,),
        in_specs=[*xs, *ss, *ts, _full(tri.shape), _full(trit.shape)] + [_full(p.shape) for p in params],
        out_specs=list(os_),
        out_shape=[jax.ShapeDtypeStruct((lt, GDN_H * GDN_DV), F32)] * 2,
        scratch_shapes=[pltpu.VMEM((2, GDN_H, GDN_DK, GDN_DV), F32)],
        compiler_params=_cparams(("arbitrary",)),
        name="gdn_scan",
    )(qkv, qkv, small, small, small_t, small_t, tri, trit, *params)


def _head_rms(y, n_heads, width):
    parts = []
    for hd in range(n_heads):
        t = y[:, hd * width:(hd + 1) * width]
        parts.append(t * lax.rsqrt(jnp.mean(t * t, axis=-1, keepdims=True) + EPS))
    return jnp.concatenate(parts, axis=1)


def _merge_kernel(x_ref, mod_ref, g_ref, wg_ref, wb_ref, wo_ref,
                  lf_ref, lb_ref, lg_ref, rf_ref, rb_ref, rg_ref, gf_ref, gb_ref, gz_ref, sf_ref, sb_ref, sz_ref,
                  rn_ref, gn_ref, sn_ref, o_ref, *, n_ctx_tiles, tile0):
    is_ctx = pl.program_id(0) + tile0 < n_ctx_tiles
    d = x_ref.shape[1]
    x = x_ref[...]
    h = _rms_mod(x, g_ref[...], _pick_mod(mod_ref, is_ctx, 0, d), _pick_mod(mod_ref, is_ctx, 1, d)).astype(BF16)
    ys = (
        (lf_ref[...] + lb_ref[...]) * _gelu_tanh(lg_ref[...]),
        _head_rms(rf_ref[...] + rb_ref[...], RET_H, RET_DV) * rn_ref[...] * _silu(rg_ref[:, 2 * RET_H * RET_DK + RET_H * RET_DV:]),
        _head_rms(gf_ref[...] + gb_ref[...], GDN_H, GDN_DV) * gn_ref[...] * _silu(gz_ref[...]),
    )
    ssd = (sf_ref[...] + sb_ref[...]) * _silu(sz_ref[...])
    ys = ys + (ssd * lax.rsqrt(jnp.mean(ssd * ssd, axis=-1, keepdims=True) + EPS) * sn_ref[...],)
    merged = None
    for nb_, y in enumerate(ys):
        gate = _sigmoid(_dot(h, wg_ref[:, nb_ * d:(nb_ + 1) * d]))
        t = gate * _dot(y.astype(BF16), wb_ref[nb_])
        merged = t if merged is None else merged + t
    out = _dot(merged.astype(BF16), wo_ref[...])
    o_ref[...] = x + _pick_mod(mod_ref, is_ctx, 2, d) * out


def _merge_call(x_all, mod, g, w_gate, w_branch, w_out, branches, norms, n_ctx, row_start):
    lt, d = x_all.shape
    tm = ROW_TILE
    t0 = row_start // tm
    nt = lt // tm - t0
    row = lambda wd: pl.BlockSpec((tm, wd), lambda i: (i + t0, 0))
    ret_g = branches[5]
    in_specs = [row(d), _full(mod.shape), _full(g.shape), _full(w_gate.shape), _full(w_branch.shape), _full(w_out.shape)]
    in_specs += [row(b.shape[1]) for b in branches]
    in_specs += [_full(n.shape) for n in norms]
    del ret_g
    return pl.pallas_call(
        functools.partial(_merge_kernel, n_ctx_tiles=n_ctx // tm, tile0=t0),
        grid=(nt,),
        in_specs=in_specs,
        out_specs=pl.BlockSpec((tm, d), lambda i: (i, 0)),
        out_shape=jax.ShapeDtypeStruct((nt * tm, d), F32),
        compiler_params=_cparams(("parallel",)),
        name="merge_out",
    )(x_all, mod, g, w_gate, w_branch, w_out, *branches, *norms)


def _ffn_kernel(x_ref, mod_ref, g_ref, wg_ref, wu_ref, wd_ref, fn_ref, o_ref, *, n_ctx_tiles, final):
    is_ctx = pl.program_id(0) < n_ctx_tiles
    d = x_ref.shape[1]
    x = x_ref[...]
    h = _rms_mod(x, g_ref[...], _pick_mod(mod_ref, is_ctx, 3, d), _pick_mod(mod_ref, is_ctx, 4, d)).astype(BF16)
    act = (_silu(_dot(h, wg_ref[...])) * _dot(h, wu_ref[...])).astype(BF16)
    y = x + _pick_mod(mod_ref, is_ctx, 5, d) * _dot(act, wd_ref[...])
    if final:
        y = y * lax.rsqrt(jnp.mean(y * y, axis=-1, keepdims=True) + EPS) * fn_ref[...]
    o_ref[...] = y


def _ffn_call(x_rows, mod, g, wg, wu, wd, final_g, n_ctx_rows, final):
    n, d = x_rows.shape
    tm = ROW_TILE
    row = pl.BlockSpec((tm, d), lambda i: (i, 0))
    return pl.pallas_call(
        functools.partial(_ffn_kernel, n_ctx_tiles=n_ctx_rows // tm, final=final),
        grid=(n // tm,),
        in_specs=[row, _full(mod.shape), _full(g.shape), _full(wg.shape), _full(wu.shape), _full(wd.shape), _full(final_g.shape)],
        out_specs=row,
        out_shape=jax.ShapeDtypeStruct((n, d), F32),
        compiler_params=_cparams(("parallel",)),
        name="dense_swiglu",
    )(x_rows, mod, g, wg, wu, wd, final_g)


def _router_kernel(x_ref, mod_ref, g_ref, rt_ref, tri_ref, h_o, gate_o, slot_o, cnt_o):
    d = x_ref.shape[1]
    h = _rms_mod(x_ref[...], g_ref[...], mod_ref[0:1, 3 * d:4 * d], mod_ref[0:1, 4 * d:5 * d])
    h_o[...] = h.astype(BF16)
    logits = _dot_nt(rt_ref[...], h, HI)
    e, b = logits.shape
    eid = lax.broadcasted_iota(jnp.int32, (e, b), 0)
    m1 = jnp.max(logits, axis=0, keepdims=True)
    i1 = jnp.min(jnp.where(logits == m1, eid, e), axis=0, keepdims=True)
    rest = jnp.where(eid == i1, -jnp.inf, logits)
    m2 = jnp.max(rest, axis=0, keepdims=True)
    i2 = jnp.min(jnp.where(rest == m2, eid, e), axis=0, keepdims=True)
    t = jnp.exp(m2 - m1)
    p1 = 1.0 / (1.0 + t)
    p2 = t / (1.0 + t)
    sel1 = eid == i1
    sel2 = eid == i2
    gate_o[...] = jnp.where(sel1, p1, jnp.where(sel2, p2, 0.0))
    sel = jnp.logical_or(sel1, sel2)
    rank = _dot(sel.astype(BF16), tri_ref[...])
    slot_o[...] = jnp.where(sel, rank, -1.0).astype(jnp.int32)
    cnt = jnp.sum(sel.astype(F32), axis=1, keepdims=True)
    cnt_o[...] = jnp.broadcast_to(cnt, (e, LANES))[None].astype(jnp.int32)


def _router_call(xl, mod, g, router_t):
    n, d = xl.shape
    b = MOE_BLOCK
    nblk = n // b
    e = router_t.shape[0]
    pos = jnp.arange(b)
    tri = (pos[:, None] < pos[None, :]).astype(BF16)
    return pl.pallas_call(
        _router_kernel,
        grid=(nblk,),
        in_specs=[pl.BlockSpec((b, d), lambda i: (i, 0)), _full(mod.shape), _full(g.shape), _full(router_t.shape), _full(tri.shape)],
        out_specs=[pl.BlockSpec((b, d), lambda i: (i, 0)), pl.BlockSpec((e, b), lambda i: (0, i)),
                   pl.BlockSpec((e, b), lambda i: (0, i)), pl.BlockSpec((1, e, LANES), lambda i: (i, 0, 0))],
        out_shape=[jax.ShapeDtypeStruct((n, d), BF16), jax.ShapeDtypeStruct((e, n), F32),
                   jax.ShapeDtypeStruct((e, n), jnp.int32), jax.ShapeDtypeStruct((nblk, e, LANES), jnp.int32)],
        compiler_params=_cparams(("parallel",)),
        name="moe_router",
    )(xl, mod, g, router_t, tri)


def _moe_kernel(nt_ref, x_ref, h_ref, gate_ref, slot_ref, mod_ref, fn_ref, wg_ref, wu_ref, wd_ref, o_ref,
                hs_ref, ys_ref, *, final):
    bi, ei, fi = pl.program_id(0), pl.program_id(1), pl.program_id(2)
    n_e, n_f = pl.num_programs(1), pl.num_programs(2)
    ts = MOE_SLOTS
    b, d = x_ref.shape
    nt = nt_ref[bi * n_e + ei]
    sid = lax.broadcasted_iota(jnp.int32, (ts, b), 0)

    def onehot(j):
        return slot_ref[pl.ds(ei, 1), :] == sid + j * ts

    @pl.when(jnp.logical_and(ei == 0, fi == 0))
    def _():
        o_ref[...] = jnp.zeros_like(o_ref)

    @pl.when(fi == 0)
    def _():
        def gather(j, carry):
            oh = onehot(j).astype(BF16)
            hs_ref[j] = _dot(oh, h_ref[...]).astype(BF16)
            return carry
        lax.fori_loop(0, nt, gather, 0)

    def expert(j, carry):
        hs = hs_ref[j]
        act = (_silu(_dot(hs, wg_ref[0])) * _dot(hs, wu_ref[0])).astype(BF16)
        y = _dot(act, wd_ref[0])

        @pl.when(fi == 0)
        def _():
            ys_ref[j] = y

        @pl.when(fi != 0)
        def _():
            ys_ref[j] = ys_ref[j] + y
        return carry
    lax.fori_loop(0, nt, expert, 0)

    @pl.when(fi == n_f - 1)
    def _():
        def scatter(j, carry):
            oh = onehot(j)
            gs = jnp.sum(jnp.where(oh, gate_ref[pl.ds(ei, 1), :], 0.0), axis=1, keepdims=True)
            o_ref[...] += _dot_tn(oh.astype(BF16), (ys_ref[j] * gs).astype(BF16))
            return carry
        lax.fori_loop(0, nt, scatter, 0)

    @pl.when(jnp.logical_and(ei == n_e - 1, fi == n_f - 1))
    def _():
        y = x_ref[...] + mod_ref[0:1, 5 * d:6 * d] * o_ref[...]
        if final:
            y = y * lax.rsqrt(jnp.mean(y * y, axis=-1, keepdims=True) + EPS) * fn_ref[...]
        o_ref[...] = y


def _moe_call(xl, h2, gate_t, slot_t, ntiles, mod, final_g, wg, wu, wd, final):
    n, d = xl.shape
    b = MOE_BLOCK
    e, _, f = wg.shape
    fs = f // MOE_FSPLIT
    n_tiles_max = pl.cdiv(b, MOE_SLOTS)
    once = pl.Buffered(1)
    grid_spec = pltpu.PrefetchScalarGridSpec(
        num_scalar_prefetch=1,
        grid=(n // b, e, MOE_FSPLIT),
        in_specs=[pl.BlockSpec((b, d), lambda i, j, k, nt: (i, 0), pipeline_mode=once),
                  pl.BlockSpec((b, d), lambda i, j, k, nt: (i, 0), pipeline_mode=once),
                  pl.BlockSpec((e, b), lambda i, j, k, nt: (0, i)),
                  pl.BlockSpec((e, b), lambda i, j, k, nt: (0, i)),
                  pl.BlockSpec(mod.shape, lambda i, j, k, nt: (0, 0)),
                  pl.BlockSpec(final_g.shape, lambda i, j, k, nt: (0, 0)),
                  pl.BlockSpec((1, d, fs), lambda i, j, k, nt: (j, 0, k)),
                  pl.BlockSpec((1, d, fs), lambda i, j, k, nt: (j, 0, k)),
                  pl.BlockSpec((1, fs, d), lambda i, j, k, nt: (j, k, 0))],
        out_specs=pl.BlockSpec((b, d), lambda i, j, k, nt: (i, 0)),
        scratch_shapes=[pltpu.VMEM((n_tiles_max, MOE_SLOTS, d), BF16), pltpu.VMEM((n_tiles_max, MOE_SLOTS, d), F32)],
    )
    return pl.pallas_call(
        functools.partial(_moe_kernel, final=final),
        grid_spec=grid_spec,
        out_shape=jax.ShapeDtypeStruct((n, d), F32),
        compiler_params=_cparams(("parallel", "arbitrary", "arbitrary")),
        name="moe_experts",
    )(ntiles, xl, h2, gate_t, slot_t, mod, final_g, wg, wu, wd)


def _mix_column_order():
    off = np.concatenate([[0], np.cumsum(MIX_SPLITS)])
    seg = lambda k: np.arange(off[k], off[k + 1])
    half = RET_DK // 2
    deint = np.concatenate([np.arange(half) * 2, np.arange(half) * 2 + 1])
    qk_perm = (np.arange(RET_H)[:, None] * RET_DK + deint[None, :]).reshape(-1)
    small = np.concatenate([seg(8), seg(9), seg(12)])
    return np.concatenate([
        seg(0), seg(6), seg(11),
        seg(1),
        seg(2)[qk_perm], seg(3)[qk_perm], seg(4), seg(5),
        seg(7), seg(10),
        small, np.full(SMALL_COLS - small.size, -1),
    ])


def _block_diag(w):
    n, i, o = w.shape
    eye = jnp.eye(n, dtype=w.dtype)
    return (eye[:, None, :, None] * w[:, :, None, :]).reshape(n * i, n * o)


def _rotary_tables(n_lat, n_ctx):
    rows = n_lat // GRID_W
    row = jnp.repeat(jnp.arange(rows, dtype=F32), GRID_W)
    col = (jnp.arange(n_lat) % GRID_W).astype(F32)
    n_freq = RET_DK // 4
    freqs = ROPE_BASE ** (-jnp.arange(n_freq, dtype=F32) / n_freq)
    ang = jnp.concatenate([row[:, None] * freqs, col[:, None] * freqs], axis=-1)
    cos, sin = jnp.cos(ang), jnp.sin(ang)
    cos_t = jnp.tile(jnp.concatenate([cos, cos], axis=-1), (1, RET_H))
    sin_t = jnp.tile(jnp.concatenate([-sin, sin], axis=-1), (1, RET_H))
    width = RET_H * RET_DK
    cos_t = jnp.concatenate([jnp.ones((n_ctx, width), F32), cos_t], axis=0)
    sin_t = jnp.concatenate([jnp.zeros((n_ctx, width), F32), sin_t], axis=0)
    return cos_t, sin_t


def kernel(x, c, ctx, c_ctx, w_mod, b_mod, norm_mix, norm_ffn, w_in, lru_conv_w, lru_conv_b, lru_wa, lru_ba, lru_wx, lru_bx, lru_lambda, ret_norm, gdn_conv_w, gdn_a_log, gdn_dt_bias, gdn_norm, ssd_conv_w, ssd_conv_b, ssd_a_log, ssd_dt_bias, ssd_d, ssd_norm, w_branch, w_out, ffn_wg, ffn_wu, ffn_wd, moe_router, moe_wg, moe_wu, moe_wd, final_norm):
    assert x.shape[0] == 1 and c.shape[0] == 1 and ctx.shape[0] == 1
    depth = w_mod.shape[0]
    n_lat, d = x.shape[1], x.shape[2]
    n_ctx = ctx.shape[1]
    assert n_ctx % ROW_TILE == 0 and n_lat % ROW_TILE == 0 and n_ctx % MIX_ROWS == 0 and n_lat % MIX_ROWS == 0
    gate_cols = N_BRANCH * d
    order = _mix_column_order()
    order_idx = jnp.asarray(np.where(order < 0, 0, order) + gate_cols)
    order_keep = jnp.asarray((order >= 0).astype(np.float32))
    cos_t, sin_t = _rotary_tables(n_lat, n_ctx)
    ct = jnp.stack([c[0], c_ctx], axis=1)
    final_g = final_norm[None, :]

    x_all = jnp.concatenate([ctx[0], x[0]], axis=0)
    for layer in range(depth):
        ctx_out = layer < depth - 1
        last = layer == depth - 1
        mod = _mod_call(ct, w_mod[layer], b_mod[layer][None, :])
        w_mix = (w_in[layer][:, order_idx] * order_keep).astype(BF16)
        conv_w = jnp.concatenate([lru_conv_w[layer], gdn_conv_w[layer], ssd_conv_w[layer]], axis=1)
        conv_b = jnp.concatenate([lru_conv_b[layer], jnp.zeros((GDN_QKV,), F32), ssd_conv_b[layer]])[None, :]
        (lru_u, gdn_qkv, ssd_xbc, p_lg, p_ret, p_gz, p_sz, p_small) = _proj_call(
            x_all, mod, norm_mix[layer][None, :], w_mix, cos_t, sin_t, conv_w, conv_b, n_ctx)
        small_t = p_small[:, 0:32].T

        lru_w = jnp.stack([jnp.concatenate([_block_diag(lru_wa[layer, dd]), _block_diag(lru_wx[layer, dd])], axis=1)
                           for dd in range(2)]).astype(BF16)
        lru_b = jnp.concatenate([lru_ba[layer], lru_bx[layer]], axis=1)[:, None, :]
        lru_f, lru_b_ = _lru_call(lru_u, lru_w, lru_b, lru_lambda[layer][:, None, :], n_ctx)
        ret_f, ret_b = _ret_call(p_ret, n_ctx)
        gdn_f, gdn_b = _gdn_call(gdn_qkv, p_small, small_t, gdn_a_log[layer], gdn_dt_bias[layer], n_ctx)
        ssd_f, ssd_b = _ssd_call(ssd_xbc, p_small, small_t, ssd_dt_bias[layer], ssd_a_log[layer], ssd_d[layer], n_ctx)

        branches = (lru_f, lru_b_, p_lg, ret_f, ret_b, p_ret, gdn_f, gdn_b, p_gz, ssd_f, ssd_b, p_sz)
        norms = (ret_norm[layer][None, :], jnp.tile(gdn_norm[layer], GDN_H)[None, :], ssd_norm[layer][None, :])
        row_start = 0 if ctx_out else n_ctx
        x_rows = _merge_call(x_all, mod, norm_mix[layer][None, :], w_in[layer][:, :gate_cols].astype(BF16),
                             w_branch[layer].astype(BF16), w_out[layer].astype(BF16), branches, norms, n_ctx, row_start)
        n_ctx_rows = n_ctx - row_start
        j = layer // 2
        if layer % 2 == 0:
            x_rows = _ffn_call(x_rows, mod, norm_ffn[layer][None, :], ffn_wg[j].astype(BF16), ffn_wu[j].astype(BF16),
                               ffn_wd[j].astype(BF16), final_g, n_ctx_rows, last)
        else:
            assert not ctx_out, "expert layers that must also emit context tokens are not supported"
            assert x_rows.shape[0] % MOE_BLOCK == 0
            h2, gate_t, slot_t, cnt = _router_call(x_rows, mod, norm_ffn[layer][None, :], moe_router[j].T)
            ntiles = ((cnt[:, :, 0] + (MOE_SLOTS - 1)) // MOE_SLOTS).reshape(-1)
            x_rows = _moe_call(x_rows, h2, gate_t, slot_t, ntiles, mod, final_g, moe_wg[j].astype(BF16),
                               moe_wu[j].astype(BF16), moe_wd[j].astype(BF16), last)
        x_all = x_rows
    out = x_all[x_all.shape[0] - n_lat:]
    return out[None]
```

```python
import functools
import math

import numpy as np
import jax
import jax.numpy as jnp
from jax import lax
from jax.experimental import pallas as pl
from jax.experimental.pallas import tpu as pltpu

F32 = jnp.float32
BF16 = jnp.bfloat16
HI = lax.Precision.HIGHEST

EPS = 1e-6
GRID_W = 64
N_BRANCH = 4
BRANCH_W = 512
CONV_W = 4
LRU_W = 512
LRU_BLOCKS = 8
LRU_C = 8.0
RET_H, RET_DK, RET_DV, RET_CHUNK = 4, 64, 128, 128
ROPE_BASE = 10000.0
GDN_H, GDN_DK, GDN_DV, GDN_CHUNK = 4, 128, 128, 64
SSD_H, SSD_P, SSD_G, SSD_N, SSD_CHUNK = 8, 64, 2, 64, 128
N_EXPERTS = 8
GDN_QKV = 2 * GDN_H * GDN_DK + GDN_H * GDN_DV
SSD_XBC = SSD_H * SSD_P + 2 * SSD_G * SSD_N
MIX_SPLITS = (LRU_W, LRU_W, RET_H * RET_DK, RET_H * RET_DK, RET_H * RET_DV, RET_H * RET_DV,
              GDN_QKV, GDN_H * GDN_DV, 2 * GDN_H, 2 * GDN_H, SSD_H * SSD_P, SSD_XBC, 2 * SSD_H)

LANES = 128
SUBLANES = 8
VMEM_LIMIT = 56 * 1024 * 1024

ROW_TILE = 256
MIX_ROWS = 256
MOE_BLOCK = 1024
MOE_SLOTS = 256
MOE_TAIL = 64
MOE_FSPLIT = 2

CONV_COLS = LRU_W + GDN_QKV + SSD_XBC
RET_COLS = 2 * RET_H * RET_DK + 2 * RET_H * RET_DV
SMALL_COLS = LANES


def _cparams(sem):
    return pltpu.CompilerParams(dimension_semantics=sem, vmem_limit_bytes=VMEM_LIMIT)


def _dot(a, b, precision=None):
    return jnp.dot(a, b, preferred_element_type=F32, precision=precision)


def _dot_nt(a, b, precision=None):
    return lax.dot_general(a, b, (((1,), (1,)), ((), ())), preferred_element_type=F32, precision=precision)


def _dot_tn(a, b, precision=None):
    return lax.dot_general(a, b, (((0,), (0,)), ((), ())), preferred_element_type=F32, precision=precision)


def _sigmoid(x):
    return 0.5 * jnp.tanh(0.5 * x) + 0.5


def _silu(x):
    return x * _sigmoid(x)


def _softplus(x):
    return jnp.maximum(x, 0.0) + jnp.log1p(jnp.exp(-jnp.abs(x)))


def _gelu_tanh(x):
    return 0.5 * x * (1.0 + jnp.tanh(math.sqrt(2.0 / math.pi) * (x + 0.044715 * (x * x * x))))


def _rms_mod(x, g, shift, scale):
    ms = jnp.mean(x * x, axis=-1, keepdims=True)
    return (x * lax.rsqrt(ms + EPS) * g) * (1.0 + scale) + shift


def _pick_mod(mod_ref, is_ctx, k, d):
    return jnp.where(is_ctx, mod_ref[1:2, k * d:(k + 1) * d], mod_ref[0:1, k * d:(k + 1) * d])


def _full(shape):
    n = len(shape)
    return pl.BlockSpec(shape, lambda *_: (0,) * n)


def _mod_kernel(ct_ref, w_ref, b_ref, o_ref):
    s = _silu(ct_ref[...])
    w = w_ref[0]
    b = b_ref[0]
    o_ref[0:1, :] = jnp.sum(s[:, 0:1] * w, axis=0, keepdims=True) + b
    o_ref[1:2, :] = jnp.sum(s[:, 1:2] * w, axis=0, keepdims=True) + b


def _mod_call(ct, w, b, layer):
    _, d, n = w.shape
    tn = 512
    return pl.pallas_call(
        _mod_kernel,
        grid=(n // tn,),
        in_specs=[_full((d, 2)), pl.BlockSpec((1, d, tn), lambda j: (layer, 0, j)),
                  pl.BlockSpec((1, 1, tn), lambda j: (layer, 0, j))],
        out_specs=pl.BlockSpec((2, tn), lambda j: (0, j)),
        out_shape=jax.ShapeDtypeStruct((2, n), F32),
        compiler_params=_cparams(("arbitrary",)),
        name="adaln_mod",
    )(ct, w, b)


def _proj_kernel(x_ref, xp_ref, xn_ref, mod_ref, g_ref, w_ref, cos_ref, sin_ref, cw_ref, cb_ref,
                 lru_o, gdn_o, ssd_o, lg_o, ret_o, gz_o, sz_o, sm_o, *, n_ctx_tiles):
    i = pl.program_id(0)
    is_ctx = i < n_ctx_tiles
    d = x_ref.shape[1]
    tm = x_ref.shape[0]
    shift = _pick_mod(mod_ref, is_ctx, 0, d)
    scale = _pick_mod(mod_ref, is_ctx, 1, d)
    hf = _rms_mod(x_ref[...], g_ref[...], shift, scale)
    h = hf.astype(BF16)

    def mm(a, b):
        return _dot(h, w_ref[:, a:b])

    has_prev = jnp.logical_and(i != 0, i != n_ctx_tiles).astype(F32)
    has_next = jnp.logical_and(i != n_ctx_tiles - 1, i != pl.num_programs(0) - 1).astype(F32)
    h_ext = jnp.concatenate([_rms_mod(xp_ref[...], g_ref[...], shift, scale), hf,
                             _rms_mod(xn_ref[...], g_ref[...], shift, scale)], axis=0).astype(BF16)
    sub = lax.broadcasted_iota(jnp.int32, (1, SUBLANES, 1), 1)
    g = tm // SUBLANES

    def conv(c0, c1):
        wd = c1 - c0
        pe = _dot(h_ext, w_ref[:, c0:c1]).reshape(g + 2, SUBLANES, wd)
        u = pe[1:g + 1]
        ext = jnp.concatenate([pe[0:1] * has_prev, u, pe[g + 1:g + 2] * has_next], axis=0)
        r1 = pltpu.roll(ext, 1, axis=1)
        r7 = pltpu.roll(ext, SUBLANES - 1, axis=1)
        r6 = pltpu.roll(ext, SUBLANES - 2, axis=1)
        um1 = jnp.where(sub >= 1, r1[1:g + 1], r1[0:g])
        up1 = jnp.where(sub < SUBLANES - 1, r7[1:g + 1], r7[2:g + 2])
        up2 = jnp.where(sub < SUBLANES - 2, r6[1:g + 1], r6[2:g + 2])
        w = cw_ref[:, c0:c1]
        y = w[0:1] * um1 + w[1:2] * u + w[2:3] * up1 + w[3:4] * up2 + cb_ref[:, c0:c1]
        return y.reshape(tm, wd)

    step = 2 * LANES
    conv_jobs, plain_jobs = [], []

    def lru_job(c0):
        lru_o[:, c0:c0 + step] = conv(c0, c0 + step)

    def gdn_qk_job(c0):
        t2 = _silu(conv(LRU_W + c0, LRU_W + c0 + step))
        for k in range(step // GDN_DK):
            t = t2[:, k * GDN_DK:(k + 1) * GDN_DK]
            t = t * lax.rsqrt(jnp.sum(t * t, axis=-1, keepdims=True) + EPS)
            if c0 < GDN_H * GDN_DK:
                t = t * (GDN_DK ** -0.5)
            gdn_o[:, c0 + k * GDN_DK:c0 + (k + 1) * GDN_DK] = t

    def gdn_v_job(c0):
        gdn_o[:, c0:c0 + step] = _silu(conv(LRU_W + c0, LRU_W + c0 + step))

    def ssd_job(c0):
        ssd_o[:, c0:c0 + step] = _silu(conv(LRU_W + GDN_QKV + c0, LRU_W + GDN_QKV + c0 + step))

    conv_jobs += [functools.partial(lru_job, c0) for c0 in range(0, LRU_W, step)]
    conv_jobs += [functools.partial(gdn_qk_job, c0) for c0 in range(0, 2 * GDN_H * GDN_DK, step)]
    conv_jobs += [functools.partial(gdn_v_job, c0) for c0 in range(2 * GDN_H * GDN_DK, GDN_QKV, step)]
    conv_jobs += [functools.partial(ssd_job, c0) for c0 in range(0, SSD_XBC, step)]

    qk_w = 2 * RET_H * RET_DK
    half = RET_DK // 2
    lane = lax.broadcasted_iota(jnp.int32, (tm, RET_H * RET_DK), 1)
    first = (lane % RET_DK) < half

    def rot(t):
        partner = jnp.where(first, pltpu.roll(t, RET_H * RET_DK - half, axis=1), pltpu.roll(t, half, axis=1))
        return t * cos_ref[...] + partner * sin_ref[...]

    c_lg = CONV_COLS
    c_ret = c_lg + LRU_W
    c_gz = c_ret + RET_COLS
    c_sz = c_gz + GDN_H * GDN_DV
    c_sm = c_sz + SSD_H * SSD_P

    def plain(o_ref, o0, c0, width):
        def job():
            o_ref[:, o0:o0 + width] = mm(c0, c0 + width)
        return job

    def ret_q_job():
        ret_o[:, 0:qk_w // 2] = rot(mm(c_ret, c_ret + qk_w // 2)) * (RET_DK ** -0.5)

    def ret_k_job():
        ret_o[:, qk_w // 2:qk_w] = rot(mm(c_ret + qk_w // 2, c_ret + qk_w))

    plain_jobs += [plain(lg_o, 0, c_lg, LRU_W), ret_q_job, ret_k_job,
                   plain(ret_o, qk_w, c_ret + qk_w, RET_H * RET_DV),
                   plain(ret_o, qk_w + RET_H * RET_DV, c_ret + qk_w + RET_H * RET_DV, RET_H * RET_DV),
                   plain(gz_o, 0, c_gz, GDN_H * GDN_DV), plain(sz_o, 0, c_sz, SSD_H * SSD_P),
                   plain(sm_o, 0, c_sm, SMALL_COLS)]
    for k in range(max(len(conv_jobs), len(plain_jobs))):
        if k < len(conv_jobs):
            conv_jobs[k]()
        if k < len(plain_jobs):
            plain_jobs[k]()


def _proj_call(x_all, mod, g, w, cos_t, sin_t, conv_w, conv_b, n_ctx):
    lt, d = x_all.shape
    tm = ROW_TILE
    nt = lt // tm
    hb = tm // SUBLANES
    widths = (LRU_W, GDN_QKV, SSD_XBC, LRU_W, RET_COLS, GDN_H * GDN_DV, SSD_H * SSD_P, SMALL_COLS)
    row = lambda wd: pl.BlockSpec((tm, wd), lambda i: (i, 0))
    return pl.pallas_call(
        functools.partial(_proj_kernel, n_ctx_tiles=n_ctx // tm),
        grid=(nt,),
        in_specs=[row(d),
                  pl.BlockSpec((SUBLANES, d), lambda i: (jnp.maximum(i * hb - 1, 0), 0)),
                  pl.BlockSpec((SUBLANES, d), lambda i: (jnp.minimum((i + 1) * hb, nt * hb - 1), 0)),
                  _full(mod.shape), _full(g.shape), _full(w.shape), row(cos_t.shape[1]), row(sin_t.shape[1]),
                  _full(conv_w.shape), _full(conv_b.shape)],
        out_specs=[row(wd) for wd in widths],
        out_shape=[jax.ShapeDtypeStruct((lt, wd), F32) for wd in widths],
        compiler_params=_cparams(("parallel",)),
        name="mix_proj",
    )(x_all, x_all, x_all, mod, g, w, cos_t, sin_t, conv_w, conv_b)


def _bwd_block(i, n_ctx_blocks, n_blocks):
    return jnp.where(i < n_ctx_blocks, n_ctx_blocks - 1 - i, n_blocks + n_ctx_blocks - 1 - i)


def _dir_specs(r, width, ncb, nb):
    return (pl.BlockSpec((r, width), lambda i: (i, 0)),
            pl.BlockSpec((r, width), lambda i: (_bwd_block(i, ncb, nb), 0)))


def _dir_specs_t(rows, r, ncb, nb):
    return (pl.BlockSpec((rows, r), lambda i: (0, i)),
            pl.BlockSpec((rows, r), lambda i: (0, _bwd_block(i, ncb, nb))))


def _chunk_order(d, n):
    return range(n) if d == 0 else range(n - 1, -1, -1)


def _lru_kernel(uf_ref, ub_ref, w_ref, b_ref, lam_ref, yf_o, yb_o, carry_ref):
    @pl.when(pl.program_id(0) == 0)
    def _():
        carry_ref[...] = jnp.zeros_like(carry_ref)

    r = uf_ref.shape[0]
    sub = lax.broadcasted_iota(jnp.int32, (1, SUBLANES, 1), 1)

    def run(d, u_ref, o_ref):
        u = u_ref[...]
        gates = _sigmoid(_dot(u.astype(BF16), w_ref[d]) + b_ref[d])
        rg = gates[:, 0:LRU_W]
        ig = gates[:, LRU_W:2 * LRU_W]
        log_a = (-LRU_C) * rg * _softplus(-lam_ref[d])
        a = jnp.exp(log_a)
        b = jnp.sqrt(1.0 - a * a) * (ig * u)
        a = a.reshape(r // SUBLANES, SUBLANES, LRU_W)
        b = b.reshape(r // SUBLANES, SUBLANES, LRU_W)
        sh = 1
        while sh < SUBLANES:
            valid = sub >= sh if d == 0 else sub < SUBLANES - sh
            shift = sh if d == 0 else SUBLANES - sh
            a_s = pltpu.roll(a, shift, axis=1)
            b_s = pltpu.roll(b, shift, axis=1)
            b = jnp.where(valid, a * b_s + b, b)
            a = jnp.where(valid, a * a_s, a)
            sh *= 2
        carry = carry_ref[d, 0:1, :]
        for g in _chunk_order(d, r // SUBLANES):
            gs = slice(g * SUBLANES, (g + 1) * SUBLANES)
            h = a[g] * carry + b[g]
            o_ref[gs, :] = h
            carry = h[SUBLANES - 1:SUBLANES, :] if d == 0 else h[0:1, :]
        carry_ref[d, 0:1, :] = carry

    run(0, uf_ref, yf_o)
    run(1, ub_ref, yb_o)


def _lru_call(u, w, b, lam, n_ctx):
    lt = u.shape[0]
    r = MIX_ROWS
    nb, ncb = lt // r, n_ctx // r
    fs, bs = _dir_specs(r, LRU_W, ncb, nb)
    return pl.pallas_call(
        _lru_kernel,
        grid=(nb,),
        in_specs=[fs, bs, _full(w.shape), _full(b.shape), _full(lam.shape)],
        out_specs=[fs, bs],
        out_shape=[jax.ShapeDtypeStruct((lt, LRU_W), F32)] * 2,
        scratch_shapes=[pltpu.VMEM((2, 8, LRU_W), F32)],
        compiler_params=_cparams(("arbitrary",)),
        name="rglru_scan",
    )(u, u, w, b, lam)


def _ret_kernel(xf_ref, xb_ref, dmat_ref, qd_ref, kd_ref, sdec_ref, bd_ref, yf_o, yb_o, s_ref):
    @pl.when(pl.program_id(0) == 0)
    def _():
        s_ref[...] = jnp.zeros_like(s_ref)

    c = RET_CHUNK
    qw = RET_H * RET_DK
    lane_head = lax.broadcasted_iota(jnp.int32, (c, qw), 1) // RET_DK

    def run(d, x_ref, o_ref):
        for ci in _chunk_order(d, x_ref.shape[0] // c):
            rs = slice(ci * c, (ci + 1) * c)
            q = x_ref[rs, 0:qw]
            k = x_ref[rs, qw:2 * qw]
            kb = k.astype(BF16)
            vb = x_ref[rs, 2 * qw:2 * qw + RET_H * RET_DV].astype(BF16)
            s = s_ref[d]
            y_inter = _dot((q * qd_ref[d]).astype(BF16), s.astype(BF16))
            upd = _dot_tn((k * kd_ref[d]).astype(BF16), vb)
            s_ref[d] = sdec_ref[...] * s + bd_ref[...] * upd
            for hd in range(RET_H):
                qh = jnp.where(lane_head == hd, q, 0.0).astype(BF16)
                sc = _dot_nt(qh, kb) * dmat_ref[d, hd]
                vs = slice(hd * RET_DV, (hd + 1) * RET_DV)
                o_ref[rs, vs] = _dot(sc.astype(BF16), vb[:, vs]) + y_inter[:, vs]

    run(0, xf_ref, yf_o)
    run(1, xb_ref, yb_o)


def _ret_tables():
    c = RET_CHUNK
    lg = jnp.log(1.0 - 2.0 ** (-5.0 - jnp.arange(RET_H, dtype=F32)))
    pos = jnp.arange(c, dtype=F32)
    dist = pos[:, None] - pos[None, :]
    d_f = jnp.where(dist >= 0, jnp.exp(jnp.maximum(dist, 0.0)[None] * lg[:, None, None]), 0.0)
    d_b = jnp.where(dist < 0, jnp.exp(jnp.maximum(-dist, 0.0)[None] * lg[:, None, None]), 0.0)
    dmat = jnp.stack([d_f, d_b])
    rep = lambda t: jnp.repeat(t, RET_DK, axis=1)
    qd = jnp.stack([rep(jnp.exp((pos + 1.0)[:, None] * lg)), rep(jnp.exp((c - pos)[:, None] * lg))])
    kd = jnp.stack([rep(jnp.exp((c - 1.0 - pos)[:, None] * lg)), rep(jnp.exp(pos[:, None] * lg))])
    hk = jnp.repeat(jnp.arange(RET_H), RET_DK)
    hv = jnp.repeat(jnp.arange(RET_H), RET_DV)
    bd = (hk[:, None] == hv[None, :]).astype(F32)
    sdec = jnp.broadcast_to(jnp.repeat(jnp.exp(c * lg), RET_DK)[:, None], bd.shape)
    return dmat, qd, kd, sdec, bd


def _ret_call(x, n_ctx):
    lt = x.shape[0]
    r = MIX_ROWS
    nb, ncb = lt // r, n_ctx // r
    tabs = _ret_tables()
    fs, bs = _dir_specs(r, 2 * RET_H * RET_DK + RET_H * RET_DV, ncb, nb)
    os_f, os_b = _dir_specs(r, RET_H * RET_DV, ncb, nb)
    return pl.pallas_call(
        _ret_kernel,
        grid=(nb,),
        in_specs=[fs, bs] + [_full(t.shape) for t in tabs],
        out_specs=[os_f, os_b],
        out_shape=[jax.ShapeDtypeStruct((lt, RET_H * RET_DV), F32)] * 2,
        scratch_shapes=[pltpu.VMEM((2, RET_H * RET_DK, RET_H * RET_DV), F32)],
        compiler_params=_cparams(("arbitrary",)),
        name="retention_scan",
    )(x, x, *tabs)


def _ssd_kernel(xf_ref, xb_ref, smf_ref, smb_ref, stf_ref, stb_ref, tri_ref, trit_ref,
                dtb_c_ref, dtb_r_ref, alog_c_ref, alog_r_ref, dskip_ref, yf_o, yb_o, s_ref):
    @pl.when(pl.program_id(0) == 0)
    def _():
        s_ref[...] = jnp.zeros_like(s_ref)

    c = SSD_CHUNK
    xw = SSD_H * SSD_P
    gw = SSD_G * SSD_N
    lo = lax.broadcasted_iota(jnp.int32, (c, LANES), 1) < SSD_N
    row_lo = lax.broadcasted_iota(jnp.int32, (LANES, 1), 0) < SSD_N
    lane_lo = lax.broadcasted_iota(jnp.int32, (1, LANES), 1) < SSD_N
    bd = row_lo == lane_lo
    dt0 = 2 * 2 * GDN_H

    nch = xf_ref.shape[0] // c
    n_pair = SSD_H // 2
    refs = ((xf_ref, smf_ref, stf_ref, yf_o), (xb_ref, smb_ref, stb_ref, yb_o))
    mask = [tri_ref[d] > 0.0 for d in range(2)]

    def rows(ck):
        return slice(ck * c, (ck + 1) * c)

    sc = {}
    for d in range(2):
        _, sm_ref, st_ref, _ = refs[d]
        a_c = -jnp.exp(alog_c_ref[d])
        a_r = -jnp.exp(alog_r_ref[d])
        for ck in range(nch):
            rs = rows(ck)
            dtc = _softplus(sm_ref[rs, dt0 + d * SSD_H:dt0 + (d + 1) * SSD_H] + dtb_c_ref[d])
            dtr = _softplus(st_ref[dt0 + d * SSD_H:dt0 + (d + 1) * SSD_H, rs] + dtb_r_ref[d])
            cs_c = _dot(tri_ref[d], dtc * a_c, HI)
            cs_r = _dot(dtr * a_r, trit_ref[d], HI)
            tot = cs_c[c - 1:c, :] if d == 0 else cs_c[0:1, :]
            sc[d, ck] = dict(dtr=dtr, cs_c=cs_c, cs_r=cs_r, e_c=jnp.exp(cs_c), dec_c=jnp.exp(tot - cs_c) * dtc,
                             e_tot=jnp.exp(tot))

    cb, c_dup, b_dup = {}, {}, {}
    for d in range(2):
        x_ref = refs[d][0]
        for ck in range(nch):
            rs = rows(ck)
            bm = x_ref[rs, xw:xw + gw]
            cm = x_ref[rs, xw + gw:xw + 2 * gw]
            bmb = bm.astype(BF16)
            b_roll = pltpu.roll(bm, SSD_N, axis=1)
            c_roll = pltpu.roll(cm, SSD_N, axis=1)
            for g in range(SSD_G):
                keep = lo if g == 0 else jnp.logical_not(lo)
                cb[d, ck, g] = _dot_nt(jnp.where(keep, cm, 0.0).astype(BF16), bmb)
                c_dup[d, ck, g] = jnp.where(keep, cm, c_roll)
                b_dup[d, ck, g] = jnp.where(keep, bm, b_roll)

    items = [(d, ck, m) for ck in range(nch) for d in range(2) for m in range(n_pair)]
    y_intra, upd, cq = {}, {}, {}
    for it in items:
        d, ck, m = it
        s_ = sc[d, ck]
        g = m // (n_pair // SSD_G)
        h0, h1 = 2 * m, 2 * m + 1
        scores = []
        for hd in (h0, h1):
            seg = s_["cs_c"][:, hd:hd + 1] - s_["cs_r"][hd:hd + 1, :]
            lmat = jnp.where(mask[d], jnp.exp(jnp.where(mask[d], seg, 0.0)), 0.0)
            scores.append(cb[d, ck, g] * lmat * s_["dtr"][hd:hd + 1, :])
        scb = jnp.concatenate(scores, axis=1).astype(BF16)
        ls = slice(m * LANES, (m + 1) * LANES)
        xp = refs[d][0][rows(ck), ls]
        xs = jnp.concatenate([jnp.where(lo, xp, 0.0), jnp.where(lo, 0.0, xp)], axis=0).astype(BF16)
        y = _dot(scb, xs)
        if d == 0:
            y = y + dskip_ref[:, ls] * xp
        y_intra[it] = y
        dec_pair = jnp.where(lo, s_["dec_c"][:, h0:h0 + 1], s_["dec_c"][:, h1:h1 + 1])
        upd[it] = jnp.where(bd, _dot_tn((b_dup[d, ck, g] * dec_pair).astype(BF16), xp.astype(BF16)), 0.0)
        e_pair = jnp.where(lo, s_["e_c"][:, h0:h0 + 1], s_["e_c"][:, h1:h1 + 1])
        cq[it] = (c_dup[d, ck, g] * e_pair).astype(BF16)

    state = {(d, m): s_ref[d, m] for d in range(2) for m in range(n_pair)}
    for pos in range(nch):
        for d in range(2):
            ck = pos if d == 0 else nch - 1 - pos
            e_tot = sc[d, ck]["e_tot"]
            for m in range(n_pair):
                it = (d, ck, m)
                st = state[d, m]
                refs[d][3][rows(ck), m * LANES:(m + 1) * LANES] = y_intra[it] + _dot(cq[it], st.astype(BF16))
                sdec = jnp.where(row_lo, e_tot[:, 2 * m:2 * m + 1], e_tot[:, 2 * m + 1:2 * m + 2])
                state[d, m] = sdec * st + upd[it]
    for (d, m), st in state.items():
        s_ref[d, m] = st


def _tri_tables(c):
    pos = jnp.arange(c)
    lower = (pos[:, None] >= pos[None, :]).astype(F32)
    tri = jnp.stack([lower, lower.T])
    trit = jnp.stack([lower.T, lower])
    return tri, trit


def _ssd_call(xbc, small, small_t, dt_bias, a_log, d_skip, n_ctx):
    lt = xbc.shape[0]
    r = MIX_ROWS
    nb, ncb = lt // r, n_ctx // r
    tri, trit = _tri_tables(SSD_CHUNK)
    params = (dt_bias[:, None, :], dt_bias[:, :, None], a_log[:, None, :], a_log[:, :, None],
              jnp.repeat(d_skip, SSD_P)[None, :])
    xs = _dir_specs(r, SSD_XBC, ncb, nb)
    ss = _dir_specs(r, SMALL_COLS, ncb, nb)
    ts = _dir_specs_t(small_t.shape[0], r, ncb, nb)
    os_ = _dir_specs(r, SSD_H * SSD_P, ncb, nb)
    return pl.pallas_call(
        _ssd_kernel,
        grid=(nb,),
        in_specs=[*xs, *ss, *ts, _full(tri.shape), _full(trit.shape)] + [_full(p.shape) for p in params],
        out_specs=list(os_),
        out_shape=[jax.ShapeDtypeStruct((lt, SSD_H * SSD_P), F32)] * 2,
        scratch_shapes=[pltpu.VMEM((2, SSD_H // 2, 2 * SSD_N, 2 * SSD_P), F32)],
        compiler_params=_cparams(("arbitrary",)),
        name="ssd_scan",
    )(xbc, xbc, small, small, small_t, small_t, tri, trit, *params)


def _gdn_kernel(xf_ref, xb_ref, smf_ref, smb_ref, stf_ref, stb_ref, tri_ref, trit_ref,
                dtb_c_ref, dtb_r_ref, alog_c_ref, alog_r_ref, yf_o, yb_o, s_ref):
    @pl.when(pl.program_id(0) == 0)
    def _():
        s_ref[...] = jnp.zeros_like(s_ref)

    c = GDN_CHUNK
    kw = GDN_H * GDN_DK
    ri = lax.broadcasted_iota(jnp.int32, (c, c), 0)
    ci_ = lax.broadcasted_iota(jnp.int32, (c, c), 1)
    eye = (ri == ci_).astype(F32)

    nch = xf_ref.shape[0] // c
    refs = ((xf_ref, smf_ref, stf_ref, yf_o), (xb_ref, smb_ref, stb_ref, yb_o))
    incl = [tri_ref[d] > 0.0 for d in range(2)]
    strict = [jnp.logical_and(incl[d], ri != ci_) for d in range(2)]

    def rows(ck):
        return slice(ck * c, (ck + 1) * c)

    def q_of(d, ck, hd):
        return refs[d][0][rows(ck), hd * GDN_DK:(hd + 1) * GDN_DK]

    def k_of(d, ck, hd):
        return refs[d][0][rows(ck), kw + hd * GDN_DK:kw + (hd + 1) * GDN_DK]

    def v_of(d, ck, hd):
        return refs[d][0][rows(ck), 2 * kw + hd * GDN_DV:2 * kw + (hd + 1) * GDN_DV]

    sc = {}
    for d in range(2):
        _, sm_ref, st_ref, _ = refs[d]
        for ck in range(nch):
            rs = rows(ck)
            a_c = sm_ref[rs, d * GDN_H:(d + 1) * GDN_H]
            b_c = sm_ref[rs, 2 * GDN_H + d * GDN_H:2 * GDN_H + (d + 1) * GDN_H]
            a_r = st_ref[d * GDN_H:(d + 1) * GDN_H, rs]
            g_c = -jnp.exp(alog_c_ref[d]) * _softplus(a_c + dtb_c_ref[d])
            g_r = -jnp.exp(alog_r_ref[d]) * _softplus(a_r + dtb_r_ref[d])
            gcs_c = _dot(tri_ref[d], g_c, HI)
            gcs_r = _dot(g_r, trit_ref[d], HI)
            g_last = gcs_c[c - 1:c, :] if d == 0 else gcs_c[0:1, :]
            sc[d, ck] = dict(beta=_sigmoid(b_c), gcs_c=gcs_c, gcs_r=gcs_r, e_c=jnp.exp(gcs_c),
                             kdec=jnp.exp(g_last - gcs_c), e_last=jnp.exp(g_last))

    items = [(d, ck, hd) for ck in range(nch) for d in range(2) for hd in range(GDN_H)]

    lm, attn = {}, {}
    for it in items:
        d, ck, hd = it
        s_ = sc[d, ck]
        seg = s_["gcs_c"][:, hd:hd + 1] - s_["gcs_r"][hd:hd + 1, :]
        dmat = jnp.where(incl[d], jnp.exp(jnp.where(incl[d], seg, 0.0)), 0.0)
        kh = k_of(*it)
        khb = kh.astype(BF16)
        lm[it] = jnp.where(strict[d], _dot_nt((kh * s_["beta"][:, hd:hd + 1]).astype(BF16), khb) * dmat, 0.0)
        attn[it] = (_dot_nt(q_of(*it).astype(BF16), khb) * dmat).astype(BF16)

    inv = {it: eye - lm[it] for it in items}
    pw = {it: lm[it].astype(BF16) for it in items}
    for _ in range(int(math.log2(c)) - 1):
        pw = {it: _dot(pw[it], pw[it]).astype(BF16) for it in items}
        inv = {it: inv[it] + _dot(inv[it].astype(BF16), pw[it]) for it in items}

    u, w = {}, {}
    for it in items:
        d, ck, hd = it
        s_ = sc[d, ck]
        bc = s_["beta"][:, hd:hd + 1]
        rhs = jnp.concatenate([v_of(*it) * bc, k_of(*it) * (bc * s_["e_c"][:, hd:hd + 1])], axis=1)
        sol = _dot(inv[it].astype(BF16), rhs.astype(BF16))
        u[it] = sol[:, 0:GDN_DV]
        w[it] = sol[:, GDN_DV:GDN_DV + GDN_DK].astype(BF16)

    state = {(d, hd): s_ref[d, hd] for d in range(2) for hd in range(GDN_H)}
    for pos in range(nch):
        cur = [(d, pos if d == 0 else nch - 1 - pos, hd) for d in range(2) for hd in range(GDN_H)]
        sb = {it: state[it[0], it[2]].astype(BF16) for it in cur}
        ws = {it: _dot(w[it], sb[it]) for it in cur}
        yq = {it: _dot((q_of(*it) * sc[it[0], it[1]]["e_c"][:, it[2]:it[2] + 1]).astype(BF16), sb[it]) for it in cur}
        for it in cur:
            d, ck, hd = it
            s_ = sc[d, ck]
            vpb = (u[it] - ws[it]).astype(BF16)
            refs[d][3][rows(ck), hd * GDN_DV:(hd + 1) * GDN_DV] = yq[it] + _dot(attn[it], vpb)
            state[d, hd] = (s_["e_last"][:, hd:hd + 1] * state[d, hd]
                            + _dot_tn((k_of(*it) * s_["kdec"][:, hd:hd + 1]).astype(BF16), vpb))
    for (d, hd), s in state.items():
        s_ref[d, hd] = s


def _gdn_call(qkv, small, small_t, a_log, dt_bias, n_ctx):
    lt = qkv.shape[0]
    r = MIX_ROWS
    nb, ncb = lt // r, n_ctx // r
    tri, trit = _tri_tables(GDN_CHUNK)
    params = (dt_bias[:, None, :], dt_bias[:, :, None], a_log[:, None, :], a_log[:, :, None])
    xs = _dir_specs(r, GDN_QKV, ncb, nb)
    ss = _dir_specs(r, SMALL_COLS, ncb, nb)
    ts = _dir_specs_t(small_t.shape[0], r, ncb, nb)
    os_ = _dir_specs(r, GDN_H * GDN_DV, ncb, nb)
    return pl.pallas_call(
        _gdn_kernel,
        grid=(nb,),
        in_specs=[*xs, *ss, *ts, _full(tri.shape), _full(trit.shape)] + [_full(p.shape) for p in params],
        out_specs=list(os_),
        out_shape=[jax.ShapeDtypeStruct((lt, GDN_H * GDN_DV), F32)] * 2,
        scratch_shapes=[pltpu.VMEM((2, GDN_H, GDN_DK, GDN_DV), F32)],
        compiler_params=_cparams(("arbitrary",)),
        name="gdn_scan",
    )(qkv, qkv, small, small, small_t, small_t, tri, trit, *params)


def _head_rms(y, n_heads, width):
    parts = []
    for hd in range(n_heads):
        t = y[:, hd * width:(hd + 1) * width]
        parts.append(t * lax.rsqrt(jnp.mean(t * t, axis=-1, keepdims=True) + EPS))
    return jnp.concatenate(parts, axis=1)


def _merge_kernel(x_ref, mod_ref, g_ref, wg_ref, wb_ref, wo_ref,
                  lf_ref, lb_ref, lg_ref, rf_ref, rb_ref, rg_ref, gf_ref, gb_ref, gz_ref, sf_ref, sb_ref, sz_ref,
                  rn_ref, gn_ref, sn_ref, o_ref, *, n_ctx_tiles, tile0):
    is_ctx = pl.program_id(0) + tile0 < n_ctx_tiles
    d = x_ref.shape[1]
    x = x_ref[...]
    h = _rms_mod(x, g_ref[...], _pick_mod(mod_ref, is_ctx, 0, d), _pick_mod(mod_ref, is_ctx, 1, d)).astype(BF16)
    ys = (
        (lf_ref[...] + lb_ref[...]) * _gelu_tanh(lg_ref[...]),
        _head_rms(rf_ref[...] + rb_ref[...], RET_H, RET_DV) * rn_ref[...] * _silu(rg_ref[:, 2 * RET_H * RET_DK + RET_H * RET_DV:]),
        _head_rms(gf_ref[...] + gb_ref[...], GDN_H, GDN_DV) * gn_ref[...] * _silu(gz_ref[...]),
    )
    ssd = (sf_ref[...] + sb_ref[...]) * _silu(sz_ref[...])
    ys = ys + (ssd * lax.rsqrt(jnp.mean(ssd * ssd, axis=-1, keepdims=True) + EPS) * sn_ref[...],)
    merged = None
    for nb_, y in enumerate(ys):
        gate = _sigmoid(_dot(h, wg_ref[:, nb_ * d:(nb_ + 1) * d]))
        t = gate * _dot(y.astype(BF16), wb_ref[nb_])
        merged = t if merged is None else merged + t
    out = _dot(merged.astype(BF16), wo_ref[...])
    o_ref[...] = x + _pick_mod(mod_ref, is_ctx, 2, d) * out


def _merge_call(x_all, mod, g, w_gate, w_branch, w_out, branches, norms, n_ctx, row_start):
    lt, d = x_all.shape
    tm = ROW_TILE
    t0 = row_start // tm
    nt = lt // tm - t0
    row = lambda wd: pl.BlockSpec((tm, wd), lambda i: (i + t0, 0))
    ret_g = branches[5]
    in_specs = [row(d), _full(mod.shape), _full(g.shape), _full(w_gate.shape), _full(w_branch.shape), _full(w_out.shape)]
    in_specs += [row(b.shape[1]) for b in branches]
    in_specs += [_full(n.shape) for n in norms]
    del ret_g
    return pl.pallas_call(
        functools.partial(_merge_kernel, n_ctx_tiles=n_ctx // tm, tile0=t0),
        grid=(nt,),
        in_specs=in_specs,
        out_specs=pl.BlockSpec((tm, d), lambda i: (i, 0)),
        out_shape=jax.ShapeDtypeStruct((nt * tm, d), F32),
        compiler_params=_cparams(("parallel",)),
        name="merge_out",
    )(x_all, mod, g, w_gate, w_branch, w_out, *branches, *norms)


def _ffn_kernel(x_ref, mod_ref, g_ref, wg_ref, wu_ref, wd_ref, fn_ref, o_ref, *, n_ctx_tiles, final):
    is_ctx = pl.program_id(0) < n_ctx_tiles
    d = x_ref.shape[1]
    x = x_ref[...]
    h = _rms_mod(x, g_ref[...], _pick_mod(mod_ref, is_ctx, 3, d), _pick_mod(mod_ref, is_ctx, 4, d)).astype(BF16)
    act = (_silu(_dot(h, wg_ref[...])) * _dot(h, wu_ref[...])).astype(BF16)
    y = x + _pick_mod(mod_ref, is_ctx, 5, d) * _dot(act, wd_ref[...])
    if final:
        y = y * lax.rsqrt(jnp.mean(y * y, axis=-1, keepdims=True) + EPS) * fn_ref[...]
    o_ref[...] = y


def _ffn_call(x_rows, mod, g, wg, wu, wd, final_g, n_ctx_rows, final):
    n, d = x_rows.shape
    tm = ROW_TILE
    row = pl.BlockSpec((tm, d), lambda i: (i, 0))
    return pl.pallas_call(
        functools.partial(_ffn_kernel, n_ctx_tiles=n_ctx_rows // tm, final=final),
        grid=(n // tm,),
        in_specs=[row, _full(mod.shape), _full(g.shape), _full(wg.shape), _full(wu.shape), _full(wd.shape), _full(final_g.shape)],
        out_specs=row,
        out_shape=jax.ShapeDtypeStruct((n, d), F32),
        compiler_params=_cparams(("parallel",)),
        name="dense_swiglu",
    )(x_rows, mod, g, wg, wu, wd, final_g)


def _router_kernel(x_ref, mod_ref, g_ref, rt_ref, tri_ref, h_o, gate_o, slot_o, cnt_o):
    d = x_ref.shape[1]
    h = _rms_mod(x_ref[...], g_ref[...], mod_ref[0:1, 3 * d:4 * d], mod_ref[0:1, 4 * d:5 * d])
    h_o[...] = h.astype(BF16)
    logits = _dot_nt(rt_ref[...], h, HI)
    e, b = logits.shape
    eid = lax.broadcasted_iota(jnp.int32, (e, b), 0)
    m1 = jnp.max(logits, axis=0, keepdims=True)
    i1 = jnp.min(jnp.where(logits == m1, eid, e), axis=0, keepdims=True)
    rest = jnp.where(eid == i1, -jnp.inf, logits)
    m2 = jnp.max(rest, axis=0, keepdims=True)
    i2 = jnp.min(jnp.where(rest == m2, eid, e), axis=0, keepdims=True)
    t = jnp.exp(m2 - m1)
    p1 = 1.0 / (1.0 + t)
    p2 = t / (1.0 + t)
    sel1 = eid == i1
    sel2 = eid == i2
    gate_o[...] = jnp.where(sel1, p1, jnp.where(sel2, p2, 0.0))
    sel = jnp.logical_or(sel1, sel2)
    rank = _dot(sel.astype(BF16), tri_ref[...])
    slot_o[...] = jnp.where(sel, rank, -1.0).astype(jnp.int32)
    cnt = jnp.sum(sel.astype(F32), axis=1, keepdims=True)
    cnt_o[...] = jnp.broadcast_to(cnt, (e, LANES))[None].astype(jnp.int32)


def _router_call(xl, mod, g, router_t):
    n, d = xl.shape
    b = MOE_BLOCK
    nblk = n // b
    e = router_t.shape[0]
    pos = jnp.arange(b)
    tri = (pos[:, None] < pos[None, :]).astype(BF16)
    return pl.pallas_call(
        _router_kernel,
        grid=(nblk,),
        in_specs=[pl.BlockSpec((b, d), lambda i: (i, 0)), _full(mod.shape), _full(g.shape), _full(router_t.shape), _full(tri.shape)],
        out_specs=[pl.BlockSpec((b, d), lambda i: (i, 0)), pl.BlockSpec((e, b), lambda i: (0, i)),
                   pl.BlockSpec((e, b), lambda i: (0, i)), pl.BlockSpec((1, e, LANES), lambda i: (i, 0, 0))],
        out_shape=[jax.ShapeDtypeStruct((n, d), BF16), jax.ShapeDtypeStruct((e, n), F32),
                   jax.ShapeDtypeStruct((e, n), jnp.int32), jax.ShapeDtypeStruct((nblk, e, LANES), jnp.int32)],
        compiler_params=_cparams(("parallel",)),
        name="moe_router",
    )(xl, mod, g, router_t, tri)


def _moe_kernel(nfull_ref, tail_ref, x_ref, h_ref, gate_ref, slot_ref, mod_ref, fn_ref, wg_ref, wu_ref, wd_ref, o_ref,
                hs_ref, ys_ref, *, final):
    bi, ei, fi = pl.program_id(0), pl.program_id(1), pl.program_id(2)
    n_e, n_f = pl.num_programs(1), pl.num_programs(2)
    b, d = x_ref.shape
    nfull = nfull_ref[bi * n_e + ei]
    tail = tail_ref[bi * n_e + ei]
    tail_start = pl.multiple_of(nfull * MOE_SLOTS, MOE_SLOTS)
    tail_sizes = tuple(range(MOE_TAIL, MOE_SLOTS, MOE_TAIL))

    def for_tiles(fn):
        def body(j, carry):
            fn(pl.multiple_of(j * MOE_SLOTS, MOE_SLOTS), MOE_SLOTS)
            return carry
        lax.fori_loop(0, nfull, body, 0)
        for k, size in enumerate(tail_sizes):
            pl.when(tail == k + 1)(functools.partial(fn, tail_start, size))

    def onehot(start, size):
        return slot_ref[pl.ds(ei, 1), :] == lax.broadcasted_iota(jnp.int32, (size, b), 0) + start

    @pl.when(jnp.logical_and(ei == 0, fi == 0))
    def _():
        o_ref[...] = jnp.zeros_like(o_ref)

    def gather(start, size):
        hs_ref[pl.ds(start, size), :] = _dot(onehot(start, size).astype(BF16), h_ref[...]).astype(BF16)

    pl.when(fi == 0)(functools.partial(for_tiles, gather))

    def expert(start, size):
        hs = hs_ref[pl.ds(start, size), :]
        act = (_silu(_dot(hs, wg_ref[0])) * _dot(hs, wu_ref[0])).astype(BF16)
        y = _dot(act, wd_ref[0])

        @pl.when(fi == 0)
        def _():
            ys_ref[pl.ds(start, size), :] = y

        @pl.when(fi != 0)
        def _():
            ys_ref[pl.ds(start, size), :] = ys_ref[pl.ds(start, size), :] + y

    for_tiles(expert)

    def scatter(start, size):
        oh = onehot(start, size)
        gs = jnp.sum(jnp.where(oh, gate_ref[pl.ds(ei, 1), :], 0.0), axis=1, keepdims=True)
        o_ref[...] += _dot_tn(oh.astype(BF16), (ys_ref[pl.ds(start, size), :] * gs).astype(BF16))

    pl.when(fi == n_f - 1)(functools.partial(for_tiles, scatter))

    @pl.when(jnp.logical_and(ei == n_e - 1, fi == n_f - 1))
    def _():
        y = x_ref[...] + mod_ref[0:1, 5 * d:6 * d] * o_ref[...]
        if final:
            y = y * lax.rsqrt(jnp.mean(y * y, axis=-1, keepdims=True) + EPS) * fn_ref[...]
        o_ref[...] = y


def _moe_call(xl, h2, gate_t, slot_t, counts, mod, final_g, wg, wu, wd, final):
    n, d = xl.shape
    b = MOE_BLOCK
    e, _, f = wg.shape
    fs = f // MOE_FSPLIT
    counts = counts.reshape(-1)
    tail = (counts % MOE_SLOTS + (MOE_TAIL - 1)) // MOE_TAIL
    nfull = counts // MOE_SLOTS + tail // (MOE_SLOTS // MOE_TAIL)
    tail = tail % (MOE_SLOTS // MOE_TAIL)
    once = pl.Buffered(1)
    grid_spec = pltpu.PrefetchScalarGridSpec(
        num_scalar_prefetch=2,
        grid=(n // b, e, MOE_FSPLIT),
        in_specs=[pl.BlockSpec((b, d), lambda i, j, k, *_: (i, 0), pipeline_mode=once),
                  pl.BlockSpec((b, d), lambda i, j, k, *_: (i, 0), pipeline_mode=once),
                  pl.BlockSpec((e, b), lambda i, j, k, *_: (0, i)),
                  pl.BlockSpec((e, b), lambda i, j, k, *_: (0, i)),
                  pl.BlockSpec(mod.shape, lambda i, j, k, *_: (0, 0)),
                  pl.BlockSpec(final_g.shape, lambda i, j, k, *_: (0, 0)),
                  pl.BlockSpec((1, d, fs), lambda i, j, k, *_: (j, 0, k)),
                  pl.BlockSpec((1, d, fs), lambda i, j, k, *_: (j, 0, k)),
                  pl.BlockSpec((1, fs, d), lambda i, j, k, *_: (j, k, 0))],
        out_specs=pl.BlockSpec((b, d), lambda i, j, k, *_: (i, 0)),
        scratch_shapes=[pltpu.VMEM((b, d), BF16), pltpu.VMEM((b, d), F32)],
    )
    return pl.pallas_call(
        functools.partial(_moe_kernel, final=final),
        grid_spec=grid_spec,
        out_shape=jax.ShapeDtypeStruct((n, d), F32),
        compiler_params=_cparams(("parallel", "arbitrary", "arbitrary")),
        name="moe_experts",
    )(nfull, tail, xl, h2, gate_t, slot_t, mod, final_g, wg, wu, wd)


def _mix_weight(w_mix):
    d = w_mix.shape[0]
    off = np.concatenate([[0], np.cumsum(MIX_SPLITS)])
    seg = lambda k: w_mix[:, off[k]:off[k + 1]]

    def halves(t):
        t = t.reshape(d, RET_H, RET_DK // 2, 2)
        return jnp.concatenate([t[..., 0], t[..., 1]], axis=-1).reshape(d, RET_H * RET_DK)

    n_small = MIX_SPLITS[8] + MIX_SPLITS[9] + MIX_SPLITS[12]
    return jnp.concatenate([
        seg(0), seg(6), seg(11),
        seg(1),
        halves(seg(2)), halves(seg(3)), seg(4), seg(5),
        seg(7), seg(10),
        seg(8), seg(9), seg(12), jnp.zeros((d, SMALL_COLS - n_small), w_mix.dtype),
    ], axis=1).astype(BF16)


def _block_diag(w):
    n, i, o = w.shape
    eye = jnp.eye(n, dtype=w.dtype)
    return (eye[:, None, :, None] * w[:, :, None, :]).reshape(n * i, n * o)


def _rotary_tables(n_lat, n_ctx):
    t = jnp.arange(n_ctx + n_lat) - n_ctx
    lat = t >= 0
    row = jnp.where(lat, t // GRID_W, 0).astype(F32)
    col = jnp.where(lat, t % GRID_W, 0).astype(F32)
    n_freq = RET_DK // 4
    freqs = ROPE_BASE ** (-jnp.arange(n_freq, dtype=F32) / n_freq)
    ang = jnp.concatenate([row[:, None] * freqs, col[:, None] * freqs], axis=-1)
    cos, sin = jnp.cos(ang), jnp.sin(ang)
    cos_t = jnp.tile(jnp.concatenate([cos, cos], axis=-1), (1, RET_H))
    sin_t = jnp.tile(jnp.concatenate([-sin, sin], axis=-1), (1, RET_H))
    return cos_t, sin_t


def kernel(x, c, ctx, c_ctx, w_mod, b_mod, norm_mix, norm_ffn, w_in, lru_conv_w, lru_conv_b, lru_wa, lru_ba, lru_wx, lru_bx, lru_lambda, ret_norm, gdn_conv_w, gdn_a_log, gdn_dt_bias, gdn_norm, ssd_conv_w, ssd_conv_b, ssd_a_log, ssd_dt_bias, ssd_d, ssd_norm, w_branch, w_out, ffn_wg, ffn_wu, ffn_wd, moe_router, moe_wg, moe_wu, moe_wd, final_norm):
    assert x.shape[0] == 1 and c.shape[0] == 1 and ctx.shape[0] == 1
    depth = w_mod.shape[0]
    n_lat, d = x.shape[1], x.shape[2]
    n_ctx = ctx.shape[1]
    assert n_ctx % ROW_TILE == 0 and n_lat % ROW_TILE == 0 and n_ctx % MIX_ROWS == 0 and n_lat % MIX_ROWS == 0
    gate_cols = N_BRANCH * d
    cos_t, sin_t = _rotary_tables(n_lat, n_ctx)
    b_mod3 = b_mod[:, None, :]
    ct = jnp.stack([c[0], c_ctx], axis=1)
    final_g = final_norm[None, :]

    x_all = jnp.concatenate([ctx[0], x[0]], axis=0)
    for layer in range(depth):
        ctx_out = layer < depth - 1
        last = layer == depth - 1
        mod = _mod_call(ct, w_mod, b_mod3, layer)
        w_mix = _mix_weight(w_in[layer][:, gate_cols:])
        conv_w = jnp.concatenate([lru_conv_w[layer], gdn_conv_w[layer], ssd_conv_w[layer]], axis=1)
        conv_b = jnp.concatenate([lru_conv_b[layer], jnp.zeros((GDN_QKV,), F32), ssd_conv_b[layer]])[None, :]
        (lru_u, gdn_qkv, ssd_xbc, p_lg, p_ret, p_gz, p_sz, p_small) = _proj_call(
            x_all, mod, norm_mix[layer][None, :], w_mix, cos_t, sin_t, conv_w, conv_b, n_ctx)
        small_t = p_small[:, 0:32].T

        lru_w = jnp.stack([jnp.concatenate([_block_diag(lru_wa[layer, dd]), _block_diag(lru_wx[layer, dd])], axis=1)
                           for dd in range(2)]).astype(BF16)
        lru_b = jnp.concatenate([lru_ba[layer], lru_bx[layer]], axis=1)[:, None, :]
        lru_f, lru_b_ = _lru_call(lru_u, lru_w, lru_b, lru_lambda[layer][:, None, :], n_ctx)
        ret_f, ret_b = _ret_call(p_ret, n_ctx)
        gdn_f, gdn_b = _gdn_call(gdn_qkv, p_small, small_t, gdn_a_log[layer], gdn_dt_bias[layer], n_ctx)
        ssd_f, ssd_b = _ssd_call(ssd_xbc, p_small, small_t, ssd_dt_bias[layer], ssd_a_log[layer], ssd_d[layer], n_ctx)

        branches = (lru_f, lru_b_, p_lg, ret_f, ret_b, p_ret, gdn_f, gdn_b, p_gz, ssd_f, ssd_b, p_sz)
        norms = (ret_norm[layer][None, :], jnp.tile(gdn_norm[layer], GDN_H)[None, :], ssd_norm[layer][None, :])
        row_start = 0 if ctx_out else n_ctx
        x_rows = _merge_call(x_all, mod, norm_mix[layer][None, :], w_in[layer][:, :gate_cols].astype(BF16),
                             w_branch[layer].astype(BF16), w_out[layer].astype(BF16), branches, norms, n_ctx, row_start)
        n_ctx_rows = n_ctx - row_start
        j = layer // 2
        if layer % 2 == 0:
            x_rows = _ffn_call(x_rows, mod, norm_ffn[layer][None, :], ffn_wg[j].astype(BF16), ffn_wu[j].astype(BF16),
                               ffn_wd[j].astype(BF16), final_g, n_ctx_rows, last)
        else:
            assert not ctx_out, "expert layers that must also emit context tokens are not supported"
            assert x_rows.shape[0] % MOE_BLOCK == 0 and MOE_BLOCK % MOE_SLOTS == 0 and MOE_SLOTS % MOE_TAIL == 0
            h2, gate_t, slot_t, cnt = _router_call(x_rows, mod, norm_ffn[layer][None, :], moe_router[j].T)
            x_rows = _moe_call(x_rows, h2, gate_t, slot_t, cnt[:, :, 0], mod, final_g, moe_wg[j].astype(BF16),
                               moe_wu[j].astype(BF16), moe_wd[j].astype(BF16), last)
        x_all = x_rows
    out = x_all[x_all.shape[0] - n_lat:]
    return out[None]
```

```python
import functools
import math

import numpy as np
import jax
import jax.numpy as jnp
from jax import lax
from jax.experimental import pallas as pl
from jax.experimental.pallas import tpu as pltpu

F32 = jnp.float32
BF16 = jnp.bfloat16
HI = lax.Precision.HIGHEST

EPS = 1e-6
GRID_W = 64
N_BRANCH = 4
BRANCH_W = 512
CONV_W = 4
LRU_W = 512
LRU_BLOCKS = 8
LRU_C = 8.0
RET_H, RET_DK, RET_DV, RET_CHUNK = 4, 64, 128, 128
ROPE_BASE = 10000.0
GDN_H, GDN_DK, GDN_DV, GDN_CHUNK = 4, 128, 128, 64
SSD_H, SSD_P, SSD_G, SSD_N, SSD_CHUNK = 8, 64, 2, 64, 128
N_EXPERTS = 8
GDN_QKV = 2 * GDN_H * GDN_DK + GDN_H * GDN_DV
SSD_XBC = SSD_H * SSD_P + 2 * SSD_G * SSD_N
MIX_SPLITS = (LRU_W, LRU_W, RET_H * RET_DK, RET_H * RET_DK, RET_H * RET_DV, RET_H * RET_DV,
              GDN_QKV, GDN_H * GDN_DV, 2 * GDN_H, 2 * GDN_H, SSD_H * SSD_P, SSD_XBC, 2 * SSD_H)

LANES = 128
SUBLANES = 8
VMEM_LIMIT = 56 * 1024 * 1024

ROW_TILE = 256
MIX_ROWS = 256
MOE_BLOCK = 1024
MOE_SLOTS = 512
MOE_TAIL = 64
MOE_FSPLIT = 2

CONV_COLS = LRU_W + GDN_QKV + SSD_XBC
RET_COLS = 2 * RET_H * RET_DK + 2 * RET_H * RET_DV
SMALL_COLS = LANES


def _cparams(sem):
    return pltpu.CompilerParams(dimension_semantics=sem, vmem_limit_bytes=VMEM_LIMIT)


def _dot(a, b, precision=None):
    return jnp.dot(a, b, preferred_element_type=F32, precision=precision)


def _dot_nt(a, b, precision=None):
    return lax.dot_general(a, b, (((1,), (1,)), ((), ())), preferred_element_type=F32, precision=precision)


def _dot_tn(a, b, precision=None):
    return lax.dot_general(a, b, (((0,), (0,)), ((), ())), preferred_element_type=F32, precision=precision)


def _sigmoid(x):
    return 0.5 * jnp.tanh(0.5 * x) + 0.5


def _silu(x):
    return x * _sigmoid(x)


def _softplus(x):
    return jnp.maximum(x, 0.0) + jnp.log1p(jnp.exp(-jnp.abs(x)))


def _gelu_tanh(x):
    return 0.5 * x * (1.0 + jnp.tanh(math.sqrt(2.0 / math.pi) * (x + 0.044715 * (x * x * x))))


def _rms_mod(x, g, shift, scale):
    ms = jnp.mean(x * x, axis=-1, keepdims=True)
    return (x * lax.rsqrt(ms + EPS) * g) * (1.0 + scale) + shift


def _pick_mod(mod_ref, is_ctx, k, d):
    return jnp.where(is_ctx, mod_ref[1:2, k * d:(k + 1) * d], mod_ref[0:1, k * d:(k + 1) * d])


def _full(shape):
    n = len(shape)
    return pl.BlockSpec(shape, lambda *_: (0,) * n)


def _mod_kernel(ct_ref, w_ref, b_ref, o_ref):
    s = _silu(ct_ref[...])
    w = w_ref[0]
    b = b_ref[0]
    o_ref[0:1, :] = jnp.sum(s[:, 0:1] * w, axis=0, keepdims=True) + b
    o_ref[1:2, :] = jnp.sum(s[:, 1:2] * w, axis=0, keepdims=True) + b


def _mod_call(ct, w, b, layer):
    _, d, n = w.shape
    tn = 512
    return pl.pallas_call(
        _mod_kernel,
        grid=(n // tn,),
        in_specs=[_full((d, 2)), pl.BlockSpec((1, d, tn), lambda j: (layer, 0, j)),
                  pl.BlockSpec((1, 1, tn), lambda j: (layer, 0, j))],
        out_specs=pl.BlockSpec((2, tn), lambda j: (0, j)),
        out_shape=jax.ShapeDtypeStruct((2, n), F32),
        compiler_params=_cparams(("arbitrary",)),
        name="adaln_mod",
    )(ct, w, b)


def _proj_kernel(x_ref, xp_ref, xn_ref, mod_ref, g_ref, w_ref, cos_ref, sin_ref, cw_ref, cb_ref,
                 lru_o, gdn_o, ssd_o, lg_o, ret_o, gz_o, sz_o, sm_o, *, n_ctx_tiles):
    i = pl.program_id(0)
    is_ctx = i < n_ctx_tiles
    d = x_ref.shape[1]
    tm = x_ref.shape[0]
    shift = _pick_mod(mod_ref, is_ctx, 0, d)
    scale = _pick_mod(mod_ref, is_ctx, 1, d)
    hf = _rms_mod(x_ref[...], g_ref[...], shift, scale)
    h = hf.astype(BF16)

    def mm(a, b):
        return _dot(h, w_ref[:, a:b])

    has_prev = jnp.logical_and(i != 0, i != n_ctx_tiles).astype(F32)
    has_next = jnp.logical_and(i != n_ctx_tiles - 1, i != pl.num_programs(0) - 1).astype(F32)
    h_ext = jnp.concatenate([_rms_mod(xp_ref[...], g_ref[...], shift, scale), hf,
                             _rms_mod(xn_ref[...], g_ref[...], shift, scale)], axis=0).astype(BF16)
    sub = lax.broadcasted_iota(jnp.int32, (1, SUBLANES, 1), 1)
    g = tm // SUBLANES

    def conv(c0, c1):
        wd = c1 - c0
        pe = _dot(h_ext, w_ref[:, c0:c1]).reshape(g + 2, SUBLANES, wd)
        u = pe[1:g + 1]
        ext = jnp.concatenate([pe[0:1] * has_prev, u, pe[g + 1:g + 2] * has_next], axis=0)
        r1 = pltpu.roll(ext, 1, axis=1)
        r7 = pltpu.roll(ext, SUBLANES - 1, axis=1)
        r6 = pltpu.roll(ext, SUBLANES - 2, axis=1)
        um1 = jnp.where(sub >= 1, r1[1:g + 1], r1[0:g])
        up1 = jnp.where(sub < SUBLANES - 1, r7[1:g + 1], r7[2:g + 2])
        up2 = jnp.where(sub < SUBLANES - 2, r6[1:g + 1], r6[2:g + 2])
        w = cw_ref[:, c0:c1]
        y = w[0:1] * um1 + w[1:2] * u + w[2:3] * up1 + w[3:4] * up2 + cb_ref[:, c0:c1]
        return y.reshape(tm, wd)

    step = 2 * LANES
    conv_jobs, plain_jobs = [], []

    def lru_job(c0):
        lru_o[:, c0:c0 + step] = conv(c0, c0 + step)

    def gdn_qk_job(c0):
        t2 = _silu(conv(LRU_W + c0, LRU_W + c0 + step))
        for k in range(step // GDN_DK):
            t = t2[:, k * GDN_DK:(k + 1) * GDN_DK]
            t = t * lax.rsqrt(jnp.sum(t * t, axis=-1, keepdims=True) + EPS)
            if c0 < GDN_H * GDN_DK:
                t = t * (GDN_DK ** -0.5)
            gdn_o[:, c0 + k * GDN_DK:c0 + (k + 1) * GDN_DK] = t

    def gdn_v_job(c0):
        gdn_o[:, c0:c0 + step] = _silu(conv(LRU_W + c0, LRU_W + c0 + step))

    def ssd_job(c0):
        ssd_o[:, c0:c0 + step] = _silu(conv(LRU_W + GDN_QKV + c0, LRU_W + GDN_QKV + c0 + step))

    conv_jobs += [functools.partial(lru_job, c0) for c0 in range(0, LRU_W, step)]
    conv_jobs += [functools.partial(gdn_qk_job, c0) for c0 in range(0, 2 * GDN_H * GDN_DK, step)]
    conv_jobs += [functools.partial(gdn_v_job, c0) for c0 in range(2 * GDN_H * GDN_DK, GDN_QKV, step)]
    conv_jobs += [functools.partial(ssd_job, c0) for c0 in range(0, SSD_XBC, step)]

    qk_w = 2 * RET_H * RET_DK
    half = RET_DK // 2
    lane = lax.broadcasted_iota(jnp.int32, (tm, RET_H * RET_DK), 1)
    first = (lane % RET_DK) < half

    def rot(t):
        partner = jnp.where(first, pltpu.roll(t, RET_H * RET_DK - half, axis=1), pltpu.roll(t, half, axis=1))
        return t * cos_ref[...] + partner * sin_ref[...]

    c_lg = CONV_COLS
    c_ret = c_lg + LRU_W
    c_gz = c_ret + RET_COLS
    c_sz = c_gz + GDN_H * GDN_DV
    c_sm = c_sz + SSD_H * SSD_P

    def plain(o_ref, o0, c0, width):
        def job():
            o_ref[:, o0:o0 + width] = mm(c0, c0 + width)
        return job

    def ret_q_job():
        ret_o[:, 0:qk_w // 2] = rot(mm(c_ret, c_ret + qk_w // 2)) * (RET_DK ** -0.5)

    def ret_k_job():
        ret_o[:, qk_w // 2:qk_w] = rot(mm(c_ret + qk_w // 2, c_ret + qk_w))

    plain_jobs += [plain(lg_o, 0, c_lg, LRU_W), ret_q_job, ret_k_job,
                   plain(ret_o, qk_w, c_ret + qk_w, RET_H * RET_DV),
                   plain(ret_o, qk_w + RET_H * RET_DV, c_ret + qk_w + RET_H * RET_DV, RET_H * RET_DV),
                   plain(gz_o, 0, c_gz, GDN_H * GDN_DV), plain(sz_o, 0, c_sz, SSD_H * SSD_P),
                   plain(sm_o, 0, c_sm, SMALL_COLS)]
    for k in range(max(len(conv_jobs), len(plain_jobs))):
        if k < len(conv_jobs):
            conv_jobs[k]()
        if k < len(plain_jobs):
            plain_jobs[k]()


def _proj_call(x_all, mod, g, w, cos_t, sin_t, conv_w, conv_b, n_ctx):
    lt, d = x_all.shape
    tm = ROW_TILE
    nt = lt // tm
    hb = tm // SUBLANES
    widths = (LRU_W, GDN_QKV, SSD_XBC, LRU_W, RET_COLS, GDN_H * GDN_DV, SSD_H * SSD_P, SMALL_COLS)
    row = lambda wd: pl.BlockSpec((tm, wd), lambda i: (i, 0))
    return pl.pallas_call(
        functools.partial(_proj_kernel, n_ctx_tiles=n_ctx // tm),
        grid=(nt,),
        in_specs=[row(d),
                  pl.BlockSpec((SUBLANES, d), lambda i: (jnp.maximum(i * hb - 1, 0), 0)),
                  pl.BlockSpec((SUBLANES, d), lambda i: (jnp.minimum((i + 1) * hb, nt * hb - 1), 0)),
                  _full(mod.shape), _full(g.shape), _full(w.shape), row(cos_t.shape[1]), row(sin_t.shape[1]),
                  _full(conv_w.shape), _full(conv_b.shape)],
        out_specs=[row(wd) for wd in widths],
        out_shape=[jax.ShapeDtypeStruct((lt, wd), F32) for wd in widths],
        compiler_params=_cparams(("parallel",)),
        name="mix_proj",
    )(x_all, x_all, x_all, mod, g, w, cos_t, sin_t, conv_w, conv_b)


def _bwd_block(i, n_ctx_blocks, n_blocks):
    return jnp.where(i < n_ctx_blocks, n_ctx_blocks - 1 - i, n_blocks + n_ctx_blocks - 1 - i)


def _dir_specs(r, width, ncb, nb):
    return (pl.BlockSpec((r, width), lambda i: (i, 0)),
            pl.BlockSpec((r, width), lambda i: (_bwd_block(i, ncb, nb), 0)))


def _dir_specs_t(rows, r, ncb, nb):
    return (pl.BlockSpec((rows, r), lambda i: (0, i)),
            pl.BlockSpec((rows, r), lambda i: (0, _bwd_block(i, ncb, nb))))


def _chunk_order(d, n):
    return range(n) if d == 0 else range(n - 1, -1, -1)


def _lru_kernel(uf_ref, ub_ref, w_ref, b_ref, lam_ref, yf_o, yb_o, carry_ref):
    @pl.when(pl.program_id(0) == 0)
    def _():
        carry_ref[...] = jnp.zeros_like(carry_ref)

    r = uf_ref.shape[0]
    sub = lax.broadcasted_iota(jnp.int32, (1, SUBLANES, 1), 1)

    def run(d, u_ref, o_ref):
        u = u_ref[...]
        gates = _sigmoid(_dot(u.astype(BF16), w_ref[d]) + b_ref[d])
        rg = gates[:, 0:LRU_W]
        ig = gates[:, LRU_W:2 * LRU_W]
        log_a = (-LRU_C) * rg * _softplus(-lam_ref[d])
        a = jnp.exp(log_a)
        b = jnp.sqrt(1.0 - a * a) * (ig * u)
        a = a.reshape(r // SUBLANES, SUBLANES, LRU_W)
        b = b.reshape(r // SUBLANES, SUBLANES, LRU_W)
        sh = 1
        while sh < SUBLANES:
            valid = sub >= sh if d == 0 else sub < SUBLANES - sh
            shift = sh if d == 0 else SUBLANES - sh
            a_s = pltpu.roll(a, shift, axis=1)
            b_s = pltpu.roll(b, shift, axis=1)
            b = jnp.where(valid, a * b_s + b, b)
            a = jnp.where(valid, a * a_s, a)
            sh *= 2
        carry = carry_ref[d, 0:1, :]
        for g in _chunk_order(d, r // SUBLANES):
            gs = slice(g * SUBLANES, (g + 1) * SUBLANES)
            h = a[g] * carry + b[g]
            o_ref[gs, :] = h
            carry = h[SUBLANES - 1:SUBLANES, :] if d == 0 else h[0:1, :]
        carry_ref[d, 0:1, :] = carry

    run(0, uf_ref, yf_o)
    run(1, ub_ref, yb_o)


def _lru_call(u, w, b, lam, n_ctx):
    lt = u.shape[0]
    r = MIX_ROWS
    nb, ncb = lt // r, n_ctx // r
    fs, bs = _dir_specs(r, LRU_W, ncb, nb)
    return pl.pallas_call(
        _lru_kernel,
        grid=(nb,),
        in_specs=[fs, bs, _full(w.shape), _full(b.shape), _full(lam.shape)],
        out_specs=[fs, bs],
        out_shape=[jax.ShapeDtypeStruct((lt, LRU_W), F32)] * 2,
        scratch_shapes=[pltpu.VMEM((2, 8, LRU_W), F32)],
        compiler_params=_cparams(("arbitrary",)),
        name="rglru_scan",
    )(u, u, w, b, lam)


def _ret_kernel(xf_ref, xb_ref, dmat_ref, qd_ref, kd_ref, sdec_ref, bd_ref, yf_o, yb_o, s_ref):
    @pl.when(pl.program_id(0) == 0)
    def _():
        s_ref[...] = jnp.zeros_like(s_ref)

    c = RET_CHUNK
    qw = RET_H * RET_DK
    lane_head = lax.broadcasted_iota(jnp.int32, (c, qw), 1) // RET_DK

    nch = xf_ref.shape[0] // c
    refs = ((xf_ref, yf_o), (xb_ref, yb_o))
    items = [(d, ck) for ck in range(nch) for d in range(2)]

    def rows(ck):
        return slice(ck * c, (ck + 1) * c)

    y_intra, upd, qdec = {}, {}, {}
    for it in items:
        d, ck = it
        x_ref = refs[d][0]
        rs = rows(ck)
        q = x_ref[rs, 0:qw]
        k = x_ref[rs, qw:2 * qw]
        kb = k.astype(BF16)
        vb = x_ref[rs, 2 * qw:2 * qw + RET_H * RET_DV].astype(BF16)
        upd[it] = bd_ref[...] * _dot_tn((k * kd_ref[d]).astype(BF16), vb)
        qdec[it] = (q * qd_ref[d]).astype(BF16)
        parts = []
        for hd in range(RET_H):
            qh = jnp.where(lane_head == hd, q, 0.0).astype(BF16)
            sc = _dot_nt(qh, kb) * dmat_ref[d, hd]
            parts.append(_dot(sc.astype(BF16), vb[:, hd * RET_DV:(hd + 1) * RET_DV]))
        y_intra[it] = parts

    state = [s_ref[0], s_ref[1]]
    for pos in range(nch):
        for d in range(2):
            ck = pos if d == 0 else nch - 1 - pos
            y_inter = _dot(qdec[d, ck], state[d].astype(BF16))
            for hd in range(RET_H):
                vs = slice(hd * RET_DV, (hd + 1) * RET_DV)
                refs[d][1][rows(ck), vs] = y_intra[d, ck][hd] + y_inter[:, vs]
            state[d] = sdec_ref[...] * state[d] + upd[d, ck]
    s_ref[0] = state[0]
    s_ref[1] = state[1]


def _ret_tables():
    c = RET_CHUNK
    lg = jnp.log(1.0 - 2.0 ** (-5.0 - jnp.arange(RET_H, dtype=F32)))
    pos = jnp.arange(c, dtype=F32)
    dist = pos[:, None] - pos[None, :]
    d_f = jnp.where(dist >= 0, jnp.exp(jnp.maximum(dist, 0.0)[None] * lg[:, None, None]), 0.0)
    d_b = jnp.where(dist < 0, jnp.exp(jnp.maximum(-dist, 0.0)[None] * lg[:, None, None]), 0.0)
    dmat = jnp.stack([d_f, d_b])
    rep = lambda t: jnp.repeat(t, RET_DK, axis=1)
    qd = jnp.stack([rep(jnp.exp((pos + 1.0)[:, None] * lg)), rep(jnp.exp((c - pos)[:, None] * lg))])
    kd = jnp.stack([rep(jnp.exp((c - 1.0 - pos)[:, None] * lg)), rep(jnp.exp(pos[:, None] * lg))])
    hk = jnp.repeat(jnp.arange(RET_H), RET_DK)
    hv = jnp.repeat(jnp.arange(RET_H), RET_DV)
    bd = (hk[:, None] == hv[None, :]).astype(F32)
    sdec = jnp.broadcast_to(jnp.repeat(jnp.exp(c * lg), RET_DK)[:, None], bd.shape)
    return dmat, qd, kd, sdec, bd


def _ret_call(x, n_ctx):
    lt = x.shape[0]
    r = MIX_ROWS
    nb, ncb = lt // r, n_ctx // r
    tabs = _ret_tables()
    fs, bs = _dir_specs(r, 2 * RET_H * RET_DK + RET_H * RET_DV, ncb, nb)
    os_f, os_b = _dir_specs(r, RET_H * RET_DV, ncb, nb)
    return pl.pallas_call(
        _ret_kernel,
        grid=(nb,),
        in_specs=[fs, bs] + [_full(t.shape) for t in tabs],
        out_specs=[os_f, os_b],
        out_shape=[jax.ShapeDtypeStruct((lt, RET_H * RET_DV), F32)] * 2,
        scratch_shapes=[pltpu.VMEM((2, RET_H * RET_DK, RET_H * RET_DV), F32)],
        compiler_params=_cparams(("arbitrary",)),
        name="retention_scan",
    )(x, x, *tabs)


def _ssd_kernel(xf_ref, xb_ref, smf_ref, smb_ref, stf_ref, stb_ref, tri_ref, trit_ref,
                dtb_c_ref, dtb_r_ref, alog_c_ref, alog_r_ref, dskip_ref, yf_o, yb_o, s_ref):
    @pl.when(pl.program_id(0) == 0)
    def _():
        s_ref[...] = jnp.zeros_like(s_ref)

    c = SSD_CHUNK
    xw = SSD_H * SSD_P
    gw = SSD_G * SSD_N
    lo = lax.broadcasted_iota(jnp.int32, (c, LANES), 1) < SSD_N
    row_lo = lax.broadcasted_iota(jnp.int32, (LANES, 1), 0) < SSD_N
    lane_lo = lax.broadcasted_iota(jnp.int32, (1, LANES), 1) < SSD_N
    bd = row_lo == lane_lo
    dt0 = 2 * 2 * GDN_H

    nch = xf_ref.shape[0] // c
    n_pair = SSD_H // 2
    refs = ((xf_ref, smf_ref, stf_ref, yf_o), (xb_ref, smb_ref, stb_ref, yb_o))
    mask = [tri_ref[d] > 0.0 for d in range(2)]

    def rows(ck):
        return slice(ck * c, (ck + 1) * c)

    sc = {}
    for d in range(2):
        _, sm_ref, st_ref, _ = refs[d]
        a_c = -jnp.exp(alog_c_ref[d])
        a_r = -jnp.exp(alog_r_ref[d])
        for ck in range(nch):
            rs = rows(ck)
            dtc = _softplus(sm_ref[rs, dt0 + d * SSD_H:dt0 + (d + 1) * SSD_H] + dtb_c_ref[d])
            dtr = _softplus(st_ref[dt0 + d * SSD_H:dt0 + (d + 1) * SSD_H, rs] + dtb_r_ref[d])
            cs_c = _dot(tri_ref[d], dtc * a_c, HI)
            cs_r = _dot(dtr * a_r, trit_ref[d], HI)
            tot = cs_c[c - 1:c, :] if d == 0 else cs_c[0:1, :]
            sc[d, ck] = dict(dtr=dtr, cs_c=cs_c, cs_r=cs_r, e_c=jnp.exp(cs_c), dec_c=jnp.exp(tot - cs_c) * dtc,
                             e_tot=jnp.exp(tot))

    cb, c_dup, b_dup = {}, {}, {}
    for d in range(2):
        x_ref = refs[d][0]
        for ck in range(nch):
            rs = rows(ck)
            bm = x_ref[rs, xw:xw + gw]
            cm = x_ref[rs, xw + gw:xw + 2 * gw]
            bmb = bm.astype(BF16)
            b_roll = pltpu.roll(bm, SSD_N, axis=1)
            c_roll = pltpu.roll(cm, SSD_N, axis=1)
            for g in range(SSD_G):
                keep = lo if g == 0 else jnp.logical_not(lo)
                cb[d, ck, g] = _dot_nt(jnp.where(keep, cm, 0.0).astype(BF16), bmb)
                c_dup[d, ck, g] = jnp.where(keep, cm, c_roll)
                b_dup[d, ck, g] = jnp.where(keep, bm, b_roll)

    items = [(d, ck, m) for ck in range(nch) for d in range(2) for m in range(n_pair)]
    y_intra, upd, cq = {}, {}, {}
    for it in items:
        d, ck, m = it
        s_ = sc[d, ck]
        g = m // (n_pair // SSD_G)
        h0, h1 = 2 * m, 2 * m + 1
        scores = []
        for hd in (h0, h1):
            seg = s_["cs_c"][:, hd:hd + 1] - s_["cs_r"][hd:hd + 1, :]
            lmat = jnp.where(mask[d], jnp.exp(jnp.where(mask[d], seg, 0.0)), 0.0)
            scores.append(cb[d, ck, g] * lmat * s_["dtr"][hd:hd + 1, :])
        scb = jnp.concatenate(scores, axis=1).astype(BF16)
        ls = slice(m * LANES, (m + 1) * LANES)
        xp = refs[d][0][rows(ck), ls]
        xs = jnp.concatenate([jnp.where(lo, xp, 0.0), jnp.where(lo, 0.0, xp)], axis=0).astype(BF16)
        y = _dot(scb, xs)
        if d == 0:
            y = y + dskip_ref[:, ls] * xp
        y_intra[it] = y
        dec_pair = jnp.where(lo, s_["dec_c"][:, h0:h0 + 1], s_["dec_c"][:, h1:h1 + 1])
        upd[it] = jnp.where(bd, _dot_tn((b_dup[d, ck, g] * dec_pair).astype(BF16), xp.astype(BF16)), 0.0)
        e_pair = jnp.where(lo, s_["e_c"][:, h0:h0 + 1], s_["e_c"][:, h1:h1 + 1])
        cq[it] = (c_dup[d, ck, g] * e_pair).astype(BF16)

    state = {(d, m): s_ref[d, m] for d in range(2) for m in range(n_pair)}
    for pos in range(nch):
        for d in range(2):
            ck = pos if d == 0 else nch - 1 - pos
            e_tot = sc[d, ck]["e_tot"]
            for m in range(n_pair):
                it = (d, ck, m)
                st = state[d, m]
                refs[d][3][rows(ck), m * LANES:(m + 1) * LANES] = y_intra[it] + _dot(cq[it], st.astype(BF16))
                sdec = jnp.where(row_lo, e_tot[:, 2 * m:2 * m + 1], e_tot[:, 2 * m + 1:2 * m + 2])
                state[d, m] = sdec * st + upd[it]
    for (d, m), st in state.items():
        s_ref[d, m] = st


def _tri_tables(c):
    pos = jnp.arange(c)
    lower = (pos[:, None] >= pos[None, :]).astype(F32)
    tri = jnp.stack([lower, lower.T])
    trit = jnp.stack([lower.T, lower])
    return tri, trit


def _ssd_call(xbc, small, small_t, dt_bias, a_log, d_skip, n_ctx):
    lt = xbc.shape[0]
    r = MIX_ROWS
    nb, ncb = lt // r, n_ctx // r
    tri, trit = _tri_tables(SSD_CHUNK)
    params = (dt_bias[:, None, :], dt_bias[:, :, None], a_log[:, None, :], a_log[:, :, None],
              jnp.repeat(d_skip, SSD_P)[None, :])
    xs = _dir_specs(r, SSD_XBC, ncb, nb)
    ss = _dir_specs(r, SMALL_COLS, ncb, nb)
    ts = _dir_specs_t(small_t.shape[0], r, ncb, nb)
    os_ = _dir_specs(r, SSD_H * SSD_P, ncb, nb)
    return pl.pallas_call(
        _ssd_kernel,
        grid=(nb,),
        in_specs=[*xs, *ss, *ts, _full(tri.shape), _full(trit.shape)] + [_full(p.shape) for p in params],
        out_specs=list(os_),
        out_shape=[jax.ShapeDtypeStruct((lt, SSD_H * SSD_P), F32)] * 2,
        scratch_shapes=[pltpu.VMEM((2, SSD_H // 2, 2 * SSD_N, 2 * SSD_P), F32)],
        compiler_params=_cparams(("arbitrary",)),
        name="ssd_scan",
    )(xbc, xbc, small, small, small_t, small_t, tri, trit, *params)


def _gdn_kernel(xf_ref, xb_ref, smf_ref, smb_ref, stf_ref, stb_ref, tri_ref, trit_ref,
                dtb_c_ref, dtb_r_ref, alog_c_ref, alog_r_ref, yf_o, yb_o, s_ref):
    @pl.when(pl.program_id(0) == 0)
    def _():
        s_ref[...] = jnp.zeros_like(s_ref)

    c = GDN_CHUNK
    kw = GDN_H * GDN_DK
    ri = lax.broadcasted_iota(jnp.int32, (c, c), 0)
    ci_ = lax.broadcasted_iota(jnp.int32, (c, c), 1)
    eye = (ri == ci_).astype(F32)

    nch = xf_ref.shape[0] // c
    refs = ((xf_ref, smf_ref, stf_ref, yf_o), (xb_ref, smb_ref, stb_ref, yb_o))
    incl = [tri_ref[d] > 0.0 for d in range(2)]
    strict = [jnp.logical_and(incl[d], ri != ci_) for d in range(2)]

    def rows(ck):
        return slice(ck * c, (ck + 1) * c)

    def q_of(d, ck, hd):
        return refs[d][0][rows(ck), hd * GDN_DK:(hd + 1) * GDN_DK]

    def k_of(d, ck, hd):
        return refs[d][0][rows(ck), kw + hd * GDN_DK:kw + (hd + 1) * GDN_DK]

    def v_of(d, ck, hd):
        return refs[d][0][rows(ck), 2 * kw + hd * GDN_DV:2 * kw + (hd + 1) * GDN_DV]

    sc = {}
    for d in range(2):
        _, sm_ref, st_ref, _ = refs[d]
        for ck in range(nch):
            rs = rows(ck)
            a_c = sm_ref[rs, d * GDN_H:(d + 1) * GDN_H]
            b_c = sm_ref[rs, 2 * GDN_H + d * GDN_H:2 * GDN_H + (d + 1) * GDN_H]
            a_r = st_ref[d * GDN_H:(d + 1) * GDN_H, rs]
            g_c = -jnp.exp(alog_c_ref[d]) * _softplus(a_c + dtb_c_ref[d])
            g_r = -jnp.exp(alog_r_ref[d]) * _softplus(a_r + dtb_r_ref[d])
            gcs_c = _dot(tri_ref[d], g_c, HI)
            gcs_r = _dot(g_r, trit_ref[d], HI)
            g_last = gcs_c[c - 1:c, :] if d == 0 else gcs_c[0:1, :]
            sc[d, ck] = dict(beta=_sigmoid(b_c), gcs_c=gcs_c, gcs_r=gcs_r, e_c=jnp.exp(gcs_c),
                             kdec=jnp.exp(g_last - gcs_c), e_last=jnp.exp(g_last))

    items = [(d, ck, hd) for ck in range(nch) for d in range(2) for hd in range(GDN_H)]

    lm, attn = {}, {}
    for it in items:
        d, ck, hd = it
        s_ = sc[d, ck]
        seg = s_["gcs_c"][:, hd:hd + 1] - s_["gcs_r"][hd:hd + 1, :]
        dmat = jnp.where(incl[d], jnp.exp(jnp.where(incl[d], seg, 0.0)), 0.0)
        kh = k_of(*it)
        both = _dot_nt(jnp.concatenate([kh * s_["beta"][:, hd:hd + 1], q_of(*it)], axis=0).astype(BF16), kh.astype(BF16))
        lm[it] = jnp.where(strict[d], both[0:c] * dmat, 0.0)
        attn[it] = (both[c:2 * c] * dmat).astype(BF16)

    n_sq = int(math.log2(c)) - 1
    inv = {it: eye - lm[it] for it in items}
    pw = {it: lm[it].astype(BF16) for it in items}
    pw = {it: _dot(pw[it], pw[it]).astype(BF16) for it in items}
    for step in range(n_sq):
        if step < n_sq - 1:
            both = {it: _dot(jnp.concatenate([inv[it].astype(BF16), pw[it]], axis=0), pw[it]) for it in items}
            inv = {it: inv[it] + both[it][0:c] for it in items}
            pw = {it: both[it][c:2 * c].astype(BF16) for it in items}
        else:
            inv = {it: inv[it] + _dot(inv[it].astype(BF16), pw[it]) for it in items}

    u, w = {}, {}
    for it in items:
        d, ck, hd = it
        s_ = sc[d, ck]
        bc = s_["beta"][:, hd:hd + 1]
        rhs = jnp.concatenate([v_of(*it) * bc, k_of(*it) * (bc * s_["e_c"][:, hd:hd + 1])], axis=1)
        sol = _dot(inv[it].astype(BF16), rhs.astype(BF16))
        u[it] = sol[:, 0:GDN_DV]
        w[it] = sol[:, GDN_DV:GDN_DV + GDN_DK].astype(BF16)

    state = {(d, hd): s_ref[d, hd] for d in range(2) for hd in range(GDN_H)}
    for pos in range(nch):
        cur = [(d, pos if d == 0 else nch - 1 - pos, hd) for d in range(2) for hd in range(GDN_H)]
        sb = {it: state[it[0], it[2]].astype(BF16) for it in cur}
        wq = {it: _dot(jnp.concatenate(
            [w[it], (q_of(*it) * sc[it[0], it[1]]["e_c"][:, it[2]:it[2] + 1]).astype(BF16)], axis=0), sb[it]) for it in cur}
        for it in cur:
            d, ck, hd = it
            s_ = sc[d, ck]
            vpb = (u[it] - wq[it][0:c]).astype(BF16)
            refs[d][3][rows(ck), hd * GDN_DV:(hd + 1) * GDN_DV] = wq[it][c:2 * c] + _dot(attn[it], vpb)
            state[d, hd] = (s_["e_last"][:, hd:hd + 1] * state[d, hd]
                            + _dot_tn((k_of(*it) * s_["kdec"][:, hd:hd + 1]).astype(BF16), vpb))
    for (d, hd), s in state.items():
        s_ref[d, hd] = s


def _gdn_call(qkv, small, small_t, a_log, dt_bias, n_ctx):
    lt = qkv.shape[0]
    r = MIX_ROWS
    nb, ncb = lt // r, n_ctx // r
    tri, trit = _tri_tables(GDN_CHUNK)
    params = (dt_bias[:, None, :], dt_bias[:, :, None], a_log[:, None, :], a_log[:, :, None])
    xs = _dir_specs(r, GDN_QKV, ncb, nb)
    ss = _dir_specs(r, SMALL_COLS, ncb, nb)
    ts = _dir_specs_t(small_t.shape[0], r, ncb, nb)
    os_ = _dir_specs(r, GDN_H * GDN_DV, ncb, nb)
    return pl.pallas_call(
        _gdn_kernel,
        grid=(nb,),
        in_specs=[*xs, *ss, *ts, _full(tri.shape), _full(trit.shape)] + [_full(p.shape) for p in params],
        out_specs=list(os_),
        out_shape=[jax.ShapeDtypeStruct((lt, GDN_H * GDN_DV), F32)] * 2,
        scratch_shapes=[pltpu.VMEM((2, GDN_H, GDN_DK, GDN_DV), F32)],
        compiler_params=_cparams(("arbitrary",)),
        name="gdn_scan",
    )(qkv, qkv, small, small, small_t, small_t, tri, trit, *params)


def _head_rms(y, n_heads, width):
    parts = []
    for hd in range(n_heads):
        t = y[:, hd * width:(hd + 1) * width]
        parts.append(t * lax.rsqrt(jnp.mean(t * t, axis=-1, keepdims=True) + EPS))
    return jnp.concatenate(parts, axis=1)


def _merge_kernel(x_ref, mod_ref, g_ref, wg_ref, wb_ref, wo_ref,
                  lf_ref, lb_ref, lg_ref, rf_ref, rb_ref, rg_ref, gf_ref, gb_ref, gz_ref, sf_ref, sb_ref, sz_ref,
                  rn_ref, gn_ref, sn_ref, o_ref, *, n_ctx_tiles, tile0):
    is_ctx = pl.program_id(0) + tile0 < n_ctx_tiles
    d = x_ref.shape[1]
    x = x_ref[...]
    h = _rms_mod(x, g_ref[...], _pick_mod(mod_ref, is_ctx, 0, d), _pick_mod(mod_ref, is_ctx, 1, d)).astype(BF16)
    def lru_y():
        return (lf_ref[...] + lb_ref[...]) * _gelu_tanh(lg_ref[...])

    def ret_y():
        g_cols = slice(2 * RET_H * RET_DK + RET_H * RET_DV, RET_COLS)
        return _head_rms(rf_ref[...] + rb_ref[...], RET_H, RET_DV) * rn_ref[...] * _silu(rg_ref[:, g_cols])

    def gdn_y():
        return _head_rms(gf_ref[...] + gb_ref[...], GDN_H, GDN_DV) * gn_ref[...] * _silu(gz_ref[...])

    def ssd_y():
        ssd = (sf_ref[...] + sb_ref[...]) * _silu(sz_ref[...])
        return ssd * lax.rsqrt(jnp.mean(ssd * ssd, axis=-1, keepdims=True) + EPS) * sn_ref[...]

    merged = None
    for nb_, branch_y in enumerate((lru_y, ret_y, gdn_y, ssd_y)):
        pre = _dot(h, wg_ref[:, nb_ * d:(nb_ + 1) * d])
        t = _sigmoid(pre) * _dot(branch_y().astype(BF16), wb_ref[nb_])
        merged = t if merged is None else merged + t
    out = _dot(merged.astype(BF16), wo_ref[...])
    o_ref[...] = x + _pick_mod(mod_ref, is_ctx, 2, d) * out


def _merge_call(x_all, mod, g, w_gate, w_branch, w_out, branches, norms, n_ctx, row_start):
    lt, d = x_all.shape
    tm = ROW_TILE
    t0 = row_start // tm
    nt = lt // tm - t0
    row = lambda wd: pl.BlockSpec((tm, wd), lambda i: (i + t0, 0))
    ret_g = branches[5]
    in_specs = [row(d), _full(mod.shape), _full(g.shape), _full(w_gate.shape), _full(w_branch.shape), _full(w_out.shape)]
    in_specs += [row(b.shape[1]) for b in branches]
    in_specs += [_full(n.shape) for n in norms]
    del ret_g
    return pl.pallas_call(
        functools.partial(_merge_kernel, n_ctx_tiles=n_ctx // tm, tile0=t0),
        grid=(nt,),
        in_specs=in_specs,
        out_specs=pl.BlockSpec((tm, d), lambda i: (i, 0)),
        out_shape=jax.ShapeDtypeStruct((nt * tm, d), F32),
        compiler_params=_cparams(("parallel",)),
        name="merge_out",
    )(x_all, mod, g, w_gate, w_branch, w_out, *branches, *norms)


def _ffn_kernel(x_ref, mod_ref, g_ref, wg_ref, wu_ref, wd_ref, fn_ref, o_ref, *, n_ctx_tiles, final):
    is_ctx = pl.program_id(0) < n_ctx_tiles
    d = x_ref.shape[1]
    x = x_ref[...]
    h = _rms_mod(x, g_ref[...], _pick_mod(mod_ref, is_ctx, 3, d), _pick_mod(mod_ref, is_ctx, 4, d)).astype(BF16)
    act = (_silu(_dot(h, wg_ref[...])) * _dot(h, wu_ref[...])).astype(BF16)
    y = x + _pick_mod(mod_ref, is_ctx, 5, d) * _dot(act, wd_ref[...])
    if final:
        y = y * lax.rsqrt(jnp.mean(y * y, axis=-1, keepdims=True) + EPS) * fn_ref[...]
    o_ref[...] = y


def _ffn_call(x_rows, mod, g, wg, wu, wd, final_g, n_ctx_rows, final):
    n, d = x_rows.shape
    tm = ROW_TILE
    row = pl.BlockSpec((tm, d), lambda i: (i, 0))
    return pl.pallas_call(
        functools.partial(_ffn_kernel, n_ctx_tiles=n_ctx_rows // tm, final=final),
        grid=(n // tm,),
        in_specs=[row, _full(mod.shape), _full(g.shape), _full(wg.shape), _full(wu.shape), _full(wd.shape), _full(final_g.shape)],
        out_specs=row,
        out_shape=jax.ShapeDtypeStruct((n, d), F32),
        compiler_params=_cparams(("parallel",)),
        name="dense_swiglu",
    )(x_rows, mod, g, wg, wu, wd, final_g)


def _router_kernel(x_ref, mod_ref, g_ref, rt_ref, tri_ref, h_o, gate_o, slot_o, cnt_o):
    d = x_ref.shape[1]
    h = _rms_mod(x_ref[...], g_ref[...], mod_ref[0:1, 3 * d:4 * d], mod_ref[0:1, 4 * d:5 * d])
    h_o[...] = h.astype(BF16)
    logits = _dot_nt(rt_ref[...], h, HI)
    e, b = logits.shape
    eid = lax.broadcasted_iota(jnp.int32, (e, b), 0)
    m1 = jnp.max(logits, axis=0, keepdims=True)
    i1 = jnp.min(jnp.where(logits == m1, eid, e), axis=0, keepdims=True)
    rest = jnp.where(eid == i1, -jnp.inf, logits)
    m2 = jnp.max(rest, axis=0, keepdims=True)
    i2 = jnp.min(jnp.where(rest == m2, eid, e), axis=0, keepdims=True)
    t = jnp.exp(m2 - m1)
    p1 = 1.0 / (1.0 + t)
    p2 = t / (1.0 + t)
    sel1 = eid == i1
    sel2 = eid == i2
    gate_o[...] = jnp.where(sel1, p1, jnp.where(sel2, p2, 0.0))
    sel = jnp.logical_or(sel1, sel2)
    rank = _dot(sel.astype(BF16), tri_ref[...])
    slot_o[...] = jnp.where(sel, rank, -1.0).astype(jnp.int32)
    cnt = jnp.sum(sel.astype(F32), axis=1, keepdims=True)
    cnt_o[...] = jnp.broadcast_to(cnt, (e, LANES))[None].astype(jnp.int32)


def _router_call(xl, mod, g, router_t):
    n, d = xl.shape
    b = MOE_BLOCK
    nblk = n // b
    e = router_t.shape[0]
    pos = jnp.arange(b)
    tri = (pos[:, None] < pos[None, :]).astype(BF16)
    return pl.pallas_call(
        _router_kernel,
        grid=(nblk,),
        in_specs=[pl.BlockSpec((b, d), lambda i: (i, 0)), _full(mod.shape), _full(g.shape), _full(router_t.shape), _full(tri.shape)],
        out_specs=[pl.BlockSpec((b, d), lambda i: (i, 0)), pl.BlockSpec((e, b), lambda i: (0, i)),
                   pl.BlockSpec((e, b), lambda i: (0, i)), pl.BlockSpec((1, e, LANES), lambda i: (i, 0, 0))],
        out_shape=[jax.ShapeDtypeStruct((n, d), BF16), jax.ShapeDtypeStruct((e, n), F32),
                   jax.ShapeDtypeStruct((e, n), jnp.int32), jax.ShapeDtypeStruct((nblk, e, LANES), jnp.int32)],
        compiler_params=_cparams(("parallel",)),
        name="moe_router",
    )(xl, mod, g, router_t, tri)


def _moe_kernel(nfull_ref, tail_ref, x_ref, h_ref, gate_ref, slot_ref, mod_ref, fn_ref, wg_ref, wu_ref, wd_ref, o_ref,
                hs_ref, ys_ref, *, final):
    bi, ei, fi = pl.program_id(0), pl.program_id(1), pl.program_id(2)
    n_e, n_f = pl.num_programs(1), pl.num_programs(2)
    b, d = x_ref.shape
    nfull = nfull_ref[bi * n_e + ei]
    tail = tail_ref[bi * n_e + ei]
    tail_start = pl.multiple_of(nfull * MOE_SLOTS, MOE_SLOTS)
    tail_sizes = tuple(range(MOE_TAIL, MOE_SLOTS, MOE_TAIL))

    def for_tiles(fn):
        def body(j, carry):
            fn(pl.multiple_of(j * MOE_SLOTS, MOE_SLOTS), MOE_SLOTS)
            return carry
        lax.fori_loop(0, nfull, body, 0)
        for k, size in enumerate(tail_sizes):
            pl.when(tail == k + 1)(functools.partial(fn, tail_start, size))

    def onehot(start, size):
        return slot_ref[pl.ds(ei, 1), :] == lax.broadcasted_iota(jnp.int32, (size, b), 0) + start

    @pl.when(jnp.logical_and(ei == 0, fi == 0))
    def _():
        o_ref[...] = jnp.zeros_like(o_ref)

    def gather(start, size):
        hs_ref[pl.ds(start, size), :] = _dot(onehot(start, size).astype(BF16), h_ref[...]).astype(BF16)

    pl.when(fi == 0)(functools.partial(for_tiles, gather))

    def expert(start, size):
        hs = hs_ref[pl.ds(start, size), :]
        act = (_silu(_dot(hs, wg_ref[0])) * _dot(hs, wu_ref[0])).astype(BF16)
        y = _dot(act, wd_ref[0])

        @pl.when(fi == 0)
        def _():
            ys_ref[pl.ds(start, size), :] = y

        @pl.when(fi != 0)
        def _():
            ys_ref[pl.ds(start, size), :] = ys_ref[pl.ds(start, size), :] + y

    for_tiles(expert)

    def scatter(start, size):
        oh = onehot(start, size)
        gs = jnp.sum(jnp.where(oh, gate_ref[pl.ds(ei, 1), :], 0.0), axis=1, keepdims=True)
        o_ref[...] += _dot_tn(oh.astype(BF16), (ys_ref[pl.ds(start, size), :] * gs).astype(BF16))

    pl.when(fi == n_f - 1)(functools.partial(for_tiles, scatter))

    @pl.when(jnp.logical_and(ei == n_e - 1, fi == n_f - 1))
    def _():
        y = x_ref[...] + mod_ref[0:1, 5 * d:6 * d] * o_ref[...]
        if final:
            y = y * lax.rsqrt(jnp.mean(y * y, axis=-1, keepdims=True) + EPS) * fn_ref[...]
        o_ref[...] = y


def _moe_call(xl, h2, gate_t, slot_t, counts, mod, final_g, wg, wu, wd, final):
    n, d = xl.shape
    b = MOE_BLOCK
    e, _, f = wg.shape
    fs = f // MOE_FSPLIT
    counts = counts.reshape(-1)
    tail = (counts % MOE_SLOTS + (MOE_TAIL - 1)) // MOE_TAIL
    nfull = counts // MOE_SLOTS + tail // (MOE_SLOTS // MOE_TAIL)
    tail = tail % (MOE_SLOTS // MOE_TAIL)
    once = pl.Buffered(1)
    grid_spec = pltpu.PrefetchScalarGridSpec(
        num_scalar_prefetch=2,
        grid=(n // b, e, MOE_FSPLIT),
        in_specs=[pl.BlockSpec((b, d), lambda i, j, k, *_: (i, 0), pipeline_mode=once),
                  pl.BlockSpec((b, d), lambda i, j, k, *_: (i, 0), pipeline_mode=once),
                  pl.BlockSpec((e, b), lambda i, j, k, *_: (0, i)),
                  pl.BlockSpec((e, b), lambda i, j, k, *_: (0, i)),
                  pl.BlockSpec(mod.shape, lambda i, j, k, *_: (0, 0)),
                  pl.BlockSpec(final_g.shape, lambda i, j, k, *_: (0, 0)),
                  pl.BlockSpec((1, d, fs), lambda i, j, k, *_: (j, 0, k)),
                  pl.BlockSpec((1, d, fs), lambda i, j, k, *_: (j, 0, k)),
                  pl.BlockSpec((1, fs, d), lambda i, j, k, *_: (j, k, 0))],
        out_specs=pl.BlockSpec((b, d), lambda i, j, k, *_: (i, 0)),
        scratch_shapes=[pltpu.VMEM((b, d), BF16), pltpu.VMEM((b, d), F32)],
    )
    return pl.pallas_call(
        functools.partial(_moe_kernel, final=final),
        grid_spec=grid_spec,
        out_shape=jax.ShapeDtypeStruct((n, d), F32),
        compiler_params=_cparams(("parallel", "arbitrary", "arbitrary")),
        name="moe_experts",
    )(nfull, tail, xl, h2, gate_t, slot_t, mod, final_g, wg, wu, wd)


def _mix_weight(w_mix):
    d = w_mix.shape[0]
    off = np.concatenate([[0], np.cumsum(MIX_SPLITS)])
    seg = lambda k: w_mix[:, off[k]:off[k + 1]]

    def halves(t):
        t = t.reshape(d, RET_H, RET_DK // 2, 2)
        return jnp.concatenate([t[..., 0], t[..., 1]], axis=-1).reshape(d, RET_H * RET_DK)

    n_small = MIX_SPLITS[8] + MIX_SPLITS[9] + MIX_SPLITS[12]
    return jnp.concatenate([
        seg(0), seg(6), seg(11),
        seg(1),
        halves(seg(2)), halves(seg(3)), seg(4), seg(5),
        seg(7), seg(10),
        seg(8), seg(9), seg(12), jnp.zeros((d, SMALL_COLS - n_small), w_mix.dtype),
    ], axis=1).astype(BF16)


def _block_diag(w):
    n, i, o = w.shape
    eye = jnp.eye(n, dtype=w.dtype)
    return (eye[:, None, :, None] * w[:, :, None, :]).reshape(n * i, n * o)


def _rotary_tables(n_lat, n_ctx):
    t = jnp.arange(n_ctx + n_lat) - n_ctx
    lat = t >= 0
    row = jnp.where(lat, t // GRID_W, 0).astype(F32)
    col = jnp.where(lat, t % GRID_W, 0).astype(F32)
    n_freq = RET_DK // 4
    freqs = ROPE_BASE ** (-jnp.arange(n_freq, dtype=F32) / n_freq)
    ang = jnp.concatenate([row[:, None] * freqs, col[:, None] * freqs], axis=-1)
    cos, sin = jnp.cos(ang), jnp.sin(ang)
    cos_t = jnp.tile(jnp.concatenate([cos, cos], axis=-1), (1, RET_H))
    sin_t = jnp.tile(jnp.concatenate([-sin, sin], axis=-1), (1, RET_H))
    return cos_t, sin_t


def kernel(x, c, ctx, c_ctx, w_mod, b_mod, norm_mix, norm_ffn, w_in, lru_conv_w, lru_conv_b, lru_wa, lru_ba, lru_wx, lru_bx, lru_lambda, ret_norm, gdn_conv_w, gdn_a_log, gdn_dt_bias, gdn_norm, ssd_conv_w, ssd_conv_b, ssd_a_log, ssd_dt_bias, ssd_d, ssd_norm, w_branch, w_out, ffn_wg, ffn_wu, ffn_wd, moe_router, moe_wg, moe_wu, moe_wd, final_norm):
    assert x.shape[0] == 1 and c.shape[0] == 1 and ctx.shape[0] == 1
    depth = w_mod.shape[0]
    n_lat, d = x.shape[1], x.shape[2]
    n_ctx = ctx.shape[1]
    assert n_ctx % ROW_TILE == 0 and n_lat % ROW_TILE == 0 and n_ctx % MIX_ROWS == 0 and n_lat % MIX_ROWS == 0
    gate_cols = N_BRANCH * d
    cos_t, sin_t = _rotary_tables(n_lat, n_ctx)
    b_mod3 = b_mod[:, None, :]
    ct = jnp.stack([c[0], c_ctx], axis=1)
    final_g = final_norm[None, :]

    x_all = jnp.concatenate([ctx[0], x[0]], axis=0)
    for layer in range(depth):
        ctx_out = layer < depth - 1
        last = layer == depth - 1
        mod = _mod_call(ct, w_mod, b_mod3, layer)
        w_mix = _mix_weight(w_in[layer][:, gate_cols:])
        conv_w = jnp.concatenate([lru_conv_w[layer], gdn_conv_w[layer], ssd_conv_w[layer]], axis=1)
        conv_b = jnp.concatenate([lru_conv_b[layer], jnp.zeros((GDN_QKV,), F32), ssd_conv_b[layer]])[None, :]
        (lru_u, gdn_qkv, ssd_xbc, p_lg, p_ret, p_gz, p_sz, p_small) = _proj_call(
            x_all, mod, norm_mix[layer][None, :], w_mix, cos_t, sin_t, conv_w, conv_b, n_ctx)
        small_t = p_small[:, 0:32].T

        lru_w = jnp.stack([jnp.concatenate([_block_diag(lru_wa[layer, dd]), _block_diag(lru_wx[layer, dd])], axis=1)
                           for dd in range(2)]).astype(BF16)
        lru_b = jnp.concatenate([lru_ba[layer], lru_bx[layer]], axis=1)[:, None, :]
        lru_f, lru_b_ = _lru_call(lru_u, lru_w, lru_b, lru_lambda[layer][:, None, :], n_ctx)
        ret_f, ret_b = _ret_call(p_ret, n_ctx)
        gdn_f, gdn_b = _gdn_call(gdn_qkv, p_small, small_t, gdn_a_log[layer], gdn_dt_bias[layer], n_ctx)
        ssd_f, ssd_b = _ssd_call(ssd_xbc, p_small, small_t, ssd_dt_bias[layer], ssd_a_log[layer], ssd_d[layer], n_ctx)

        branches = (lru_f, lru_b_, p_lg, ret_f, ret_b, p_ret, gdn_f, gdn_b, p_gz, ssd_f, ssd_b, p_sz)
        norms = (ret_norm[layer][None, :], jnp.tile(gdn_norm[layer], GDN_H)[None, :], ssd_norm[layer][None, :])
        row_start = 0 if ctx_out else n_ctx
        x_rows = _merge_call(x_all, mod, norm_mix[layer][None, :], w_in[layer][:, :gate_cols].astype(BF16),
                             w_branch[layer].astype(BF16), w_out[layer].astype(BF16), branches, norms, n_ctx, row_start)
        n_ctx_rows = n_ctx - row_start
        j = layer // 2
        if layer % 2 == 0:
            x_rows = _ffn_call(x_rows, mod, norm_ffn[layer][None, :], ffn_wg[j].astype(BF16), ffn_wu[j].astype(BF16),
                               ffn_wd[j].astype(BF16), final_g, n_ctx_rows, last)
        else:
            assert not ctx_out, "expert layers that must also emit context tokens are not supported"
            assert x_rows.shape[0] % MOE_BLOCK == 0 and MOE_BLOCK % MOE_SLOTS == 0 and MOE_SLOTS % MOE_TAIL == 0
            h2, gate_t, slot_t, cnt = _router_call(x_rows, mod, norm_ffn[layer][None, :], moe_router[j].T)
            x_rows = _moe_call(x_rows, h2, gate_t, slot_t, cnt[:, :, 0], mod, final_g, moe_wg[j].astype(BF16),
                               moe_wu[j].astype(BF16), moe_wd[j].astype(BF16), last)
        x_all = x_rows
    out = x_all[x_all.shape[0] - n_lat:]
    return out[None]
```

```python
import functools
import math

import numpy as np
import jax
import jax.numpy as jnp
from jax import lax
from jax.experimental import pallas as pl
from jax.experimental.pallas import tpu as pltpu

F32 = jnp.float32
BF16 = jnp.bfloat16
HI = lax.Precision.HIGHEST

EPS = 1e-6
GRID_W = 64
N_BRANCH = 4
BRANCH_W = 512
CONV_W = 4
LRU_W = 512
LRU_BLOCKS = 8
LRU_C = 8.0
RET_H, RET_DK, RET_DV, RET_CHUNK = 4, 64, 128, 128
ROPE_BASE = 10000.0
GDN_H, GDN_DK, GDN_DV, GDN_CHUNK = 4, 128, 128, 64
SSD_H, SSD_P, SSD_G, SSD_N, SSD_CHUNK = 8, 64, 2, 64, 128
N_EXPERTS = 8
GDN_QKV = 2 * GDN_H * GDN_DK + GDN_H * GDN_DV
SSD_XBC = SSD_H * SSD_P + 2 * SSD_G * SSD_N
MIX_SPLITS = (LRU_W, LRU_W, RET_H * RET_DK, RET_H * RET_DK, RET_H * RET_DV, RET_H * RET_DV,
              GDN_QKV, GDN_H * GDN_DV, 2 * GDN_H, 2 * GDN_H, SSD_H * SSD_P, SSD_XBC, 2 * SSD_H)

LANES = 128
SUBLANES = 8
VMEM_LIMIT = 56 * 1024 * 1024

ROW_TILE = 256
MIX_ROWS = 256
MOE_BLOCK = 1024
MOE_SLOTS = 512
GDN_INV_BLOCK = 16
MOE_TAIL = 64
MOE_FSPLIT = 2

CONV_COLS = LRU_W + GDN_QKV + SSD_XBC
RET_COLS = 2 * RET_H * RET_DK + 2 * RET_H * RET_DV
SMALL_COLS = LANES


def _cparams(sem):
    return pltpu.CompilerParams(dimension_semantics=sem, vmem_limit_bytes=VMEM_LIMIT)


def _dot(a, b, precision=None):
    return jnp.dot(a, b, preferred_element_type=F32, precision=precision)


def _dot_nt(a, b, precision=None):
    return lax.dot_general(a, b, (((1,), (1,)), ((), ())), preferred_element_type=F32, precision=precision)


def _dot_tn(a, b, precision=None):
    return lax.dot_general(a, b, (((0,), (0,)), ((), ())), preferred_element_type=F32, precision=precision)


def _sigmoid(x):
    return 0.5 * jnp.tanh(0.5 * x) + 0.5


def _silu(x):
    return x * _sigmoid(x)


def _softplus(x):
    return jnp.maximum(x, 0.0) + jnp.log1p(jnp.exp(-jnp.abs(x)))


def _gelu_tanh(x):
    return 0.5 * x * (1.0 + jnp.tanh(math.sqrt(2.0 / math.pi) * (x + 0.044715 * (x * x * x))))


def _rms_mod(x, g, shift, scale):
    ms = jnp.mean(x * x, axis=-1, keepdims=True)
    return (x * lax.rsqrt(ms + EPS) * g) * (1.0 + scale) + shift


def _pick_mod(mod_ref, is_ctx, k, d):
    return jnp.where(is_ctx, mod_ref[1:2, k * d:(k + 1) * d], mod_ref[0:1, k * d:(k + 1) * d])


def _full(shape):
    n = len(shape)
    return pl.BlockSpec(shape, lambda *_: (0,) * n)


def _mod_kernel(ct_ref, w_ref, b_ref, o_ref):
    s = _silu(ct_ref[...])
    w = w_ref[0]
    b = b_ref[0]
    o_ref[0:1, :] = jnp.sum(s[:, 0:1] * w, axis=0, keepdims=True) + b
    o_ref[1:2, :] = jnp.sum(s[:, 1:2] * w, axis=0, keepdims=True) + b


def _mod_call(ct, w, b, layer):
    _, d, n = w.shape
    tn = 512
    return pl.pallas_call(
        _mod_kernel,
        grid=(n // tn,),
        in_specs=[_full((d, 2)), pl.BlockSpec((1, d, tn), lambda j: (layer, 0, j)),
                  pl.BlockSpec((1, 1, tn), lambda j: (layer, 0, j))],
        out_specs=pl.BlockSpec((2, tn), lambda j: (0, j)),
        out_shape=jax.ShapeDtypeStruct((2, n), F32),
        compiler_params=_cparams(("arbitrary",)),
        name="adaln_mod",
    )(ct, w, b)


def _proj_kernel(xc_ref, xl_ref, xcp_ref, xlp_ref, xcn_ref, xln_ref, mod_ref, g_ref, w_ref, cos_ref, sin_ref, cw_ref, cb_ref,
                 lru_o, gdn_o, ssd_o, lg_o, ret_o, gz_o, sz_o, sm_o, *, n_ctx_tiles):
    i = pl.program_id(0)
    is_ctx = i < n_ctx_tiles
    d = xc_ref.shape[1]
    tm = xc_ref.shape[0]
    shift = _pick_mod(mod_ref, is_ctx, 0, d)
    scale = _pick_mod(mod_ref, is_ctx, 1, d)
    hf = _rms_mod(jnp.where(is_ctx, xc_ref[...], xl_ref[...]), g_ref[...], shift, scale)
    h = hf.astype(BF16)

    def mm(a, b):
        return _dot(h, w_ref[:, a:b])

    has_prev = jnp.logical_and(i != 0, i != n_ctx_tiles).astype(F32)
    has_next = jnp.logical_and(i != n_ctx_tiles - 1, i != pl.num_programs(0) - 1).astype(F32)
    h_ext = jnp.concatenate([_rms_mod(jnp.where(is_ctx, xcp_ref[...], xlp_ref[...]), g_ref[...], shift, scale), hf,
                             _rms_mod(jnp.where(is_ctx, xcn_ref[...], xln_ref[...]), g_ref[...], shift, scale)],
                            axis=0).astype(BF16)
    sub = lax.broadcasted_iota(jnp.int32, (1, SUBLANES, 1), 1)
    g = tm // SUBLANES

    def conv(c0, c1):
        wd = c1 - c0
        pe = _dot(h_ext, w_ref[:, c0:c1]).reshape(g + 2, SUBLANES, wd)
        u = pe[1:g + 1]
        ext = jnp.concatenate([pe[0:1] * has_prev, u, pe[g + 1:g + 2] * has_next], axis=0)
        r1 = pltpu.roll(ext, 1, axis=1)
        r7 = pltpu.roll(ext, SUBLANES - 1, axis=1)
        r6 = pltpu.roll(ext, SUBLANES - 2, axis=1)
        um1 = jnp.where(sub >= 1, r1[1:g + 1], r1[0:g])
        up1 = jnp.where(sub < SUBLANES - 1, r7[1:g + 1], r7[2:g + 2])
        up2 = jnp.where(sub < SUBLANES - 2, r6[1:g + 1], r6[2:g + 2])
        w = cw_ref[:, c0:c1]
        y = w[0:1] * um1 + w[1:2] * u + w[2:3] * up1 + w[3:4] * up2 + cb_ref[:, c0:c1]
        return y.reshape(tm, wd)

    step = 2 * LANES
    conv_jobs, plain_jobs = [], []

    def lru_job(c0):
        lru_o[:, c0:c0 + step] = conv(c0, c0 + step)

    def gdn_qk_job(c0):
        t2 = _silu(conv(LRU_W + c0, LRU_W + c0 + step))
        for k in range(step // GDN_DK):
            t = t2[:, k * GDN_DK:(k + 1) * GDN_DK]
            t = t * lax.rsqrt(jnp.sum(t * t, axis=-1, keepdims=True) + EPS)
            if c0 < GDN_H * GDN_DK:
                t = t * (GDN_DK ** -0.5)
            gdn_o[:, c0 + k * GDN_DK:c0 + (k + 1) * GDN_DK] = t

    def gdn_v_job(c0):
        gdn_o[:, c0:c0 + step] = _silu(conv(LRU_W + c0, LRU_W + c0 + step))

    def ssd_job(c0):
        ssd_o[:, c0:c0 + step] = _silu(conv(LRU_W + GDN_QKV + c0, LRU_W + GDN_QKV + c0 + step))

    conv_jobs += [functools.partial(lru_job, c0) for c0 in range(0, LRU_W, step)]
    conv_jobs += [functools.partial(gdn_qk_job, c0) for c0 in range(0, 2 * GDN_H * GDN_DK, step)]
    conv_jobs += [functools.partial(gdn_v_job, c0) for c0 in range(2 * GDN_H * GDN_DK, GDN_QKV, step)]
    conv_jobs += [functools.partial(ssd_job, c0) for c0 in range(0, SSD_XBC, step)]

    qk_w = 2 * RET_H * RET_DK
    half = RET_DK // 2
    lane = lax.broadcasted_iota(jnp.int32, (tm, RET_H * RET_DK), 1)
    first = (lane % RET_DK) < half

    def rot(t):
        partner = jnp.where(first, pltpu.roll(t, RET_H * RET_DK - half, axis=1), pltpu.roll(t, half, axis=1))
        return t * cos_ref[...] + partner * sin_ref[...]

    c_lg = CONV_COLS
    c_ret = c_lg + LRU_W
    c_gz = c_ret + RET_COLS
    c_sz = c_gz + GDN_H * GDN_DV
    c_sm = c_sz + SSD_H * SSD_P

    def plain(o_ref, o0, c0, width):
        def job():
            o_ref[:, o0:o0 + width] = mm(c0, c0 + width)
        return job

    def ret_q_job():
        ret_o[:, 0:qk_w // 2] = rot(mm(c_ret, c_ret + qk_w // 2)) * (RET_DK ** -0.5)

    def ret_k_job():
        ret_o[:, qk_w // 2:qk_w] = rot(mm(c_ret + qk_w // 2, c_ret + qk_w))

    plain_jobs += [plain(lg_o, 0, c_lg, LRU_W), ret_q_job, ret_k_job,
                   plain(ret_o, qk_w, c_ret + qk_w, RET_H * RET_DV),
                   plain(ret_o, qk_w + RET_H * RET_DV, c_ret + qk_w + RET_H * RET_DV, RET_H * RET_DV),
                   plain(gz_o, 0, c_gz, GDN_H * GDN_DV), plain(sz_o, 0, c_sz, SSD_H * SSD_P),
                   plain(sm_o, 0, c_sm, SMALL_COLS)]
    for k in range(max(len(conv_jobs), len(plain_jobs))):
        if k < len(conv_jobs):
            conv_jobs[k]()
        if k < len(plain_jobs):
            plain_jobs[k]()


def _row_sources(x_ctx, x_lat, lat_row0, n_ctx, tm, tile0=0):
    d = x_ctx.shape[1]
    nct, ob = n_ctx // tm, lat_row0 // tm
    return (pl.BlockSpec((tm, d), lambda i: (jnp.minimum(i + tile0, nct - 1), 0)),
            pl.BlockSpec((tm, d), lambda i: (jnp.maximum(i + tile0 - nct, 0) + ob, 0)))


def _proj_call(x_ctx, x_lat, lat_row0, mod, g, w, cos_t, sin_t, conv_w, conv_b, n_ctx):
    d = x_ctx.shape[1]
    tm = ROW_TILE
    lt = n_ctx + x_lat.shape[0] - lat_row0
    nt, nct, ob = lt // tm, n_ctx // tm, lat_row0 // tm
    hb = tm // SUBLANES
    widths = (LRU_W, GDN_QKV, SSD_XBC, LRU_W, RET_COLS, GDN_H * GDN_DV, SSD_H * SSD_P, SMALL_COLS)
    row = lambda wd: pl.BlockSpec((tm, wd), lambda i: (i, 0))
    halo = lambda fn: pl.BlockSpec((SUBLANES, d), lambda i: (fn(i), 0))
    return pl.pallas_call(
        functools.partial(_proj_kernel, n_ctx_tiles=nct),
        grid=(nt,),
        in_specs=[*_row_sources(x_ctx, x_lat, lat_row0, n_ctx, tm),
                  halo(lambda i: jnp.clip(i * hb - 1, 0, nct * hb - 1)),
                  halo(lambda i: jnp.maximum((i - nct) * hb - 1, 0) + ob * hb),
                  halo(lambda i: jnp.minimum((i + 1) * hb, nct * hb - 1)),
                  halo(lambda i: jnp.clip((i + 1 - nct) * hb, 0, (nt - nct) * hb - 1) + ob * hb),
                  _full(mod.shape), _full(g.shape), _full(w.shape), row(cos_t.shape[1]), row(sin_t.shape[1]),
                  _full(conv_w.shape), _full(conv_b.shape)],
        out_specs=[row(wd) for wd in widths],
        out_shape=[jax.ShapeDtypeStruct((lt, wd), F32) for wd in widths],
        compiler_params=_cparams(("parallel",)),
        name="mix_proj",
    )(x_ctx, x_lat, x_ctx, x_lat, x_ctx, x_lat, mod, g, w, cos_t, sin_t, conv_w, conv_b)


def _bwd_block(i, n_ctx_blocks, n_blocks):
    return jnp.where(i < n_ctx_blocks, n_ctx_blocks - 1 - i, n_blocks + n_ctx_blocks - 1 - i)


def _dir_specs(r, width, ncb, nb):
    return (pl.BlockSpec((r, width), lambda i: (i, 0)),
            pl.BlockSpec((r, width), lambda i: (_bwd_block(i, ncb, nb), 0)))


def _dir_specs_t(rows, r, ncb, nb):
    return (pl.BlockSpec((rows, r), lambda i: (0, i)),
            pl.BlockSpec((rows, r), lambda i: (0, _bwd_block(i, ncb, nb))))


def _chunk_order(d, n):
    return range(n) if d == 0 else range(n - 1, -1, -1)


def _lru_kernel(uf_ref, ub_ref, w_ref, b_ref, lam_ref, yf_o, yb_o, carry_ref):
    @pl.when(pl.program_id(0) == 0)
    def _():
        carry_ref[...] = jnp.zeros_like(carry_ref)

    r = uf_ref.shape[0]
    sub = lax.broadcasted_iota(jnp.int32, (1, SUBLANES, 1), 1)

    def run(d, u_ref, o_ref):
        u = u_ref[...]
        gates = _sigmoid(_dot(u.astype(BF16), w_ref[d]) + b_ref[d])
        rg = gates[:, 0:LRU_W]
        ig = gates[:, LRU_W:2 * LRU_W]
        log_a = (-LRU_C) * rg * _softplus(-lam_ref[d])
        a = jnp.exp(log_a)
        b = jnp.sqrt(1.0 - a * a) * (ig * u)
        a = a.reshape(r // SUBLANES, SUBLANES, LRU_W)
        b = b.reshape(r // SUBLANES, SUBLANES, LRU_W)
        sh = 1
        while sh < SUBLANES:
            valid = sub >= sh if d == 0 else sub < SUBLANES - sh
            shift = sh if d == 0 else SUBLANES - sh
            a_s = pltpu.roll(a, shift, axis=1)
            b_s = pltpu.roll(b, shift, axis=1)
            b = jnp.where(valid, a * b_s + b, b)
            a = jnp.where(valid, a * a_s, a)
            sh *= 2
        carry = carry_ref[d, 0:1, :]
        for g in _chunk_order(d, r // SUBLANES):
            gs = slice(g * SUBLANES, (g + 1) * SUBLANES)
            h = a[g] * carry + b[g]
            o_ref[gs, :] = h
            carry = h[SUBLANES - 1:SUBLANES, :] if d == 0 else h[0:1, :]
        carry_ref[d, 0:1, :] = carry

    run(0, uf_ref, yf_o)
    run(1, ub_ref, yb_o)


def _lru_call(u, w, b, lam, n_ctx):
    lt = u.shape[0]
    r = MIX_ROWS
    nb, ncb = lt // r, n_ctx // r
    fs, bs = _dir_specs(r, LRU_W, ncb, nb)
    return pl.pallas_call(
        _lru_kernel,
        grid=(nb,),
        in_specs=[fs, bs, _full(w.shape), _full(b.shape), _full(lam.shape)],
        out_specs=[fs, bs],
        out_shape=[jax.ShapeDtypeStruct((lt, LRU_W), F32)] * 2,
        scratch_shapes=[pltpu.VMEM((2, 8, LRU_W), F32)],
        compiler_params=_cparams(("arbitrary",)),
        name="rglru_scan",
    )(u, u, w, b, lam)


def _ret_kernel(xf_ref, xb_ref, dmat_ref, qd_ref, kd_ref, sdec_ref, bd_ref, yf_o, yb_o, s_ref):
    @pl.when(pl.program_id(0) == 0)
    def _():
        s_ref[...] = jnp.zeros_like(s_ref)

    c = RET_CHUNK
    qw = RET_H * RET_DK
    lane_head = lax.broadcasted_iota(jnp.int32, (c, qw), 1) // RET_DK

    nch = xf_ref.shape[0] // c
    refs = ((xf_ref, yf_o), (xb_ref, yb_o))
    items = [(d, ck) for ck in range(nch) for d in range(2)]

    def rows(ck):
        return slice(ck * c, (ck + 1) * c)

    y_intra, upd, qdec = {}, {}, {}
    for it in items:
        d, ck = it
        x_ref = refs[d][0]
        rs = rows(ck)
        q = x_ref[rs, 0:qw]
        k = x_ref[rs, qw:2 * qw]
        kb = k.astype(BF16)
        vb = x_ref[rs, 2 * qw:2 * qw + RET_H * RET_DV].astype(BF16)
        upd[it] = bd_ref[...] * _dot_tn((k * kd_ref[d]).astype(BF16), vb)
        qdec[it] = (q * qd_ref[d]).astype(BF16)
        parts = []
        for hd in range(RET_H):
            qh = jnp.where(lane_head == hd, q, 0.0).astype(BF16)
            sc = _dot_nt(qh, kb) * dmat_ref[d, hd]
            parts.append(_dot(sc.astype(BF16), vb[:, hd * RET_DV:(hd + 1) * RET_DV]))
        y_intra[it] = parts

    state = [s_ref[0], s_ref[1]]
    for pos in range(nch):
        for d in range(2):
            ck = pos if d == 0 else nch - 1 - pos
            y_inter = _dot(qdec[d, ck], state[d].astype(BF16))
            for hd in range(RET_H):
                vs = slice(hd * RET_DV, (hd + 1) * RET_DV)
                refs[d][1][rows(ck), vs] = y_intra[d, ck][hd] + y_inter[:, vs]
            state[d] = sdec_ref[...] * state[d] + upd[d, ck]
    s_ref[0] = state[0]
    s_ref[1] = state[1]


def _ret_tables():
    c = RET_CHUNK
    lg = jnp.log(1.0 - 2.0 ** (-5.0 - jnp.arange(RET_H, dtype=F32)))
    pos = jnp.arange(c, dtype=F32)
    dist = pos[:, None] - pos[None, :]
    d_f = jnp.where(dist >= 0, jnp.exp(jnp.maximum(dist, 0.0)[None] * lg[:, None, None]), 0.0)
    d_b = jnp.where(dist < 0, jnp.exp(jnp.maximum(-dist, 0.0)[None] * lg[:, None, None]), 0.0)
    dmat = jnp.stack([d_f, d_b])
    rep = lambda t: jnp.repeat(t, RET_DK, axis=1)
    qd = jnp.stack([rep(jnp.exp((pos + 1.0)[:, None] * lg)), rep(jnp.exp((c - pos)[:, None] * lg))])
    kd = jnp.stack([rep(jnp.exp((c - 1.0 - pos)[:, None] * lg)), rep(jnp.exp(pos[:, None] * lg))])
    hk = jnp.repeat(jnp.arange(RET_H), RET_DK)
    hv = jnp.repeat(jnp.arange(RET_H), RET_DV)
    bd = (hk[:, None] == hv[None, :]).astype(F32)
    sdec = jnp.broadcast_to(jnp.repeat(jnp.exp(c * lg), RET_DK)[:, None], bd.shape)
    return dmat, qd, kd, sdec, bd


def _ret_call(x, n_ctx):
    lt = x.shape[0]
    r = MIX_ROWS
    nb, ncb = lt // r, n_ctx // r
    tabs = _ret_tables()
    fs, bs = _dir_specs(r, 2 * RET_H * RET_DK + RET_H * RET_DV, ncb, nb)
    os_f, os_b = _dir_specs(r, RET_H * RET_DV, ncb, nb)
    return pl.pallas_call(
        _ret_kernel,
        grid=(nb,),
        in_specs=[fs, bs] + [_full(t.shape) for t in tabs],
        out_specs=[os_f, os_b],
        out_shape=[jax.ShapeDtypeStruct((lt, RET_H * RET_DV), F32)] * 2,
        scratch_shapes=[pltpu.VMEM((2, RET_H * RET_DK, RET_H * RET_DV), F32)],
        compiler_params=_cparams(("arbitrary",)),
        name="retention_scan",
    )(x, x, *tabs)


def _ssd_kernel(xf_ref, xb_ref, smf_ref, smb_ref, stf_ref, stb_ref, tri_ref, trit_ref,
                dtb_c_ref, dtb_r_ref, alog_c_ref, alog_r_ref, dskip_ref, yf_o, yb_o, s_ref):
    @pl.when(pl.program_id(0) == 0)
    def _():
        s_ref[...] = jnp.zeros_like(s_ref)

    c = SSD_CHUNK
    xw = SSD_H * SSD_P
    gw = SSD_G * SSD_N
    lo = lax.broadcasted_iota(jnp.int32, (c, LANES), 1) < SSD_N
    row_lo = lax.broadcasted_iota(jnp.int32, (LANES, 1), 0) < SSD_N
    lane_lo = lax.broadcasted_iota(jnp.int32, (1, LANES), 1) < SSD_N
    bd = row_lo == lane_lo
    dt0 = 2 * 2 * GDN_H

    nch = xf_ref.shape[0] // c
    n_pair = SSD_H // 2
    refs = ((xf_ref, smf_ref, stf_ref, yf_o), (xb_ref, smb_ref, stb_ref, yb_o))
    mask = [tri_ref[d] > 0.0 for d in range(2)]

    def rows(ck):
        return slice(ck * c, (ck + 1) * c)

    sc = {}
    for d in range(2):
        _, sm_ref, st_ref, _ = refs[d]
        a_c = -jnp.exp(alog_c_ref[d])
        a_r = -jnp.exp(alog_r_ref[d])
        for ck in range(nch):
            rs = rows(ck)
            dtc = _softplus(sm_ref[rs, dt0 + d * SSD_H:dt0 + (d + 1) * SSD_H] + dtb_c_ref[d])
            dtr = _softplus(st_ref[dt0 + d * SSD_H:dt0 + (d + 1) * SSD_H, rs] + dtb_r_ref[d])
            cs_c = _dot(tri_ref[d], dtc * a_c, HI)
            cs_r = _dot(dtr * a_r, trit_ref[d], HI)
            tot = cs_c[c - 1:c, :] if d == 0 else cs_c[0:1, :]
            sc[d, ck] = dict(dtr=dtr, cs_c=cs_c, cs_r=cs_r, e_c=jnp.exp(cs_c), dec_c=jnp.exp(tot - cs_c) * dtc,
                             e_tot=jnp.exp(tot))

    cb, c_dup, b_dup = {}, {}, {}
    for d in range(2):
        x_ref = refs[d][0]
        for ck in range(nch):
            rs = rows(ck)
            bm = x_ref[rs, xw:xw + gw]
            cm = x_ref[rs, xw + gw:xw + 2 * gw]
            bmb = bm.astype(BF16)
            b_roll = pltpu.roll(bm, SSD_N, axis=1)
            c_roll = pltpu.roll(cm, SSD_N, axis=1)
            for g in range(SSD_G):
                keep = lo if g == 0 else jnp.logical_not(lo)
                cb[d, ck, g] = _dot_nt(jnp.where(keep, cm, 0.0).astype(BF16), bmb)
                c_dup[d, ck, g] = jnp.where(keep, cm, c_roll)
                b_dup[d, ck, g] = jnp.where(keep, bm, b_roll)

    items = [(d, ck, m) for ck in range(nch) for d in range(2) for m in range(n_pair)]
    y_intra, upd, cq = {}, {}, {}
    for it in items:
        d, ck, m = it
        s_ = sc[d, ck]
        g = m // (n_pair // SSD_G)
        h0, h1 = 2 * m, 2 * m + 1
        scores = []
        for hd in (h0, h1):
            seg = s_["cs_c"][:, hd:hd + 1] - s_["cs_r"][hd:hd + 1, :]
            lmat = jnp.where(mask[d], jnp.exp(jnp.where(mask[d], seg, 0.0)), 0.0)
            scores.append(cb[d, ck, g] * lmat * s_["dtr"][hd:hd + 1, :])
        scb = jnp.concatenate(scores, axis=1).astype(BF16)
        ls = slice(m * LANES, (m + 1) * LANES)
        xp = refs[d][0][rows(ck), ls]
        xs = jnp.concatenate([jnp.where(lo, xp, 0.0), jnp.where(lo, 0.0, xp)], axis=0).astype(BF16)
        y = _dot(scb, xs)
        if d == 0:
            y = y + dskip_ref[:, ls] * xp
        y_intra[it] = y
        dec_pair = jnp.where(lo, s_["dec_c"][:, h0:h0 + 1], s_["dec_c"][:, h1:h1 + 1])
        upd[it] = jnp.where(bd, _dot_tn((b_dup[d, ck, g] * dec_pair).astype(BF16), xp.astype(BF16)), 0.0)
        e_pair = jnp.where(lo, s_["e_c"][:, h0:h0 + 1], s_["e_c"][:, h1:h1 + 1])
        cq[it] = (c_dup[d, ck, g] * e_pair).astype(BF16)

    state = {(d, m): s_ref[d, m] for d in range(2) for m in range(n_pair)}
    for pos in range(nch):
        for d in range(2):
            ck = pos if d == 0 else nch - 1 - pos
            e_tot = sc[d, ck]["e_tot"]
            for m in range(n_pair):
                it = (d, ck, m)
                st = state[d, m]
                refs[d][3][rows(ck), m * LANES:(m + 1) * LANES] = y_intra[it] + _dot(cq[it], st.astype(BF16))
                sdec = jnp.where(row_lo, e_tot[:, 2 * m:2 * m + 1], e_tot[:, 2 * m + 1:2 * m + 2])
                state[d, m] = sdec * st + upd[it]
    for (d, m), st in state.items():
        s_ref[d, m] = st


def _tri_tables(c):
    pos = jnp.arange(c)
    lower = (pos[:, None] >= pos[None, :]).astype(F32)
    tri = jnp.stack([lower, lower.T])
    trit = jnp.stack([lower.T, lower])
    return tri, trit


def _ssd_call(xbc, small, small_t, dt_bias, a_log, d_skip, n_ctx):
    lt = xbc.shape[0]
    r = MIX_ROWS
    nb, ncb = lt // r, n_ctx // r
    tri, trit = _tri_tables(SSD_CHUNK)
    params = (dt_bias[:, None, :], dt_bias[:, :, None], a_log[:, None, :], a_log[:, :, None],
              jnp.repeat(d_skip, SSD_P)[None, :])
    xs = _dir_specs(r, SSD_XBC, ncb, nb)
    ss = _dir_specs(r, SMALL_COLS, ncb, nb)
    ts = _dir_specs_t(small_t.shape[0], r, ncb, nb)
    os_ = _dir_specs(r, SSD_H * SSD_P, ncb, nb)
    return pl.pallas_call(
        _ssd_kernel,
        grid=(nb,),
        in_specs=[*xs, *ss, *ts, _full(tri.shape), _full(trit.shape)] + [_full(p.shape) for p in params],
        out_specs=list(os_),
        out_shape=[jax.ShapeDtypeStruct((lt, SSD_H * SSD_P), F32)] * 2,
        scratch_shapes=[pltpu.VMEM((2, SSD_H // 2, 2 * SSD_N, 2 * SSD_P), F32)],
        compiler_params=_cparams(("arbitrary",)),
        name="ssd_scan",
    )(xbc, xbc, small, small, small_t, small_t, tri, trit, *params)


def _gdn_kernel(xf_ref, xb_ref, smf_ref, smb_ref, stf_ref, stb_ref, tri_ref, trit_ref,
                dtb_c_ref, dtb_r_ref, alog_c_ref, alog_r_ref, yf_o, yb_o, s_ref):
    @pl.when(pl.program_id(0) == 0)
    def _():
        s_ref[...] = jnp.zeros_like(s_ref)

    c = GDN_CHUNK
    kw = GDN_H * GDN_DK
    ri = lax.broadcasted_iota(jnp.int32, (c, c), 0)
    ci_ = lax.broadcasted_iota(jnp.int32, (c, c), 1)
    eye = (ri == ci_).astype(F32)

    nch = xf_ref.shape[0] // c
    refs = ((xf_ref, smf_ref, stf_ref, yf_o), (xb_ref, smb_ref, stb_ref, yb_o))
    incl = [tri_ref[d] > 0.0 for d in range(2)]
    strict = [jnp.logical_and(incl[d], ri != ci_) for d in range(2)]

    def rows(ck):
        return slice(ck * c, (ck + 1) * c)

    def q_of(d, ck, hd):
        return refs[d][0][rows(ck), hd * GDN_DK:(hd + 1) * GDN_DK]

    def k_of(d, ck, hd):
        return refs[d][0][rows(ck), kw + hd * GDN_DK:kw + (hd + 1) * GDN_DK]

    def v_of(d, ck, hd):
        return refs[d][0][rows(ck), 2 * kw + hd * GDN_DV:2 * kw + (hd + 1) * GDN_DV]

    sc = {}
    for d in range(2):
        _, sm_ref, st_ref, _ = refs[d]
        for ck in range(nch):
            rs = rows(ck)
            a_c = sm_ref[rs, d * GDN_H:(d + 1) * GDN_H]
            b_c = sm_ref[rs, 2 * GDN_H + d * GDN_H:2 * GDN_H + (d + 1) * GDN_H]
            a_r = st_ref[d * GDN_H:(d + 1) * GDN_H, rs]
            g_c = -jnp.exp(alog_c_ref[d]) * _softplus(a_c + dtb_c_ref[d])
            g_r = -jnp.exp(alog_r_ref[d]) * _softplus(a_r + dtb_r_ref[d])
            gcs_c = _dot(tri_ref[d], g_c, HI)
            gcs_r = _dot(g_r, trit_ref[d], HI)
            g_last = gcs_c[c - 1:c, :] if d == 0 else gcs_c[0:1, :]
            sc[d, ck] = dict(beta=_sigmoid(b_c), gcs_c=gcs_c, gcs_r=gcs_r, e_c=jnp.exp(gcs_c),
                             kdec=jnp.exp(g_last - gcs_c), e_last=jnp.exp(g_last))

    items = [(d, ck, hd) for ck in range(nch) for d in range(2) for hd in range(GDN_H)]

    lm, attn = {}, {}
    for it in items:
        d, ck, hd = it
        s_ = sc[d, ck]
        seg = s_["gcs_c"][:, hd:hd + 1] - s_["gcs_r"][hd:hd + 1, :]
        dmat = jnp.where(incl[d], jnp.exp(jnp.where(incl[d], seg, 0.0)), 0.0)
        kh = k_of(*it)
        both = _dot_nt(jnp.concatenate([kh * s_["beta"][:, hd:hd + 1], q_of(*it)], axis=0).astype(BF16), kh.astype(BF16))
        lm[it] = jnp.where(strict[d], both[0:c] * dmat, 0.0)
        attn[it] = (both[c:2 * c] * dmat).astype(BF16)

    blk = GDN_INV_BLOCK
    same = [(ri // (blk << k)) == (ci_ // (blk << k)) for k in range(int(math.log2(c // blk)) + 1)]
    diag = {it: jnp.where(same[0], lm[it], 0.0) for it in items}
    inv = {it: eye - diag[it] for it in items}
    pw = {it: diag[it].astype(BF16) for it in items}
    pw = {it: _dot(pw[it], pw[it]).astype(BF16) for it in items}
    n_sq = int(math.log2(blk)) - 1
    for step in range(n_sq):
        if step < n_sq - 1:
            both = {it: _dot(jnp.concatenate([inv[it].astype(BF16), pw[it]], axis=0), pw[it]) for it in items}
            inv = {it: inv[it] + both[it][0:c] for it in items}
            pw = {it: both[it][c:2 * c].astype(BF16) for it in items}
        else:
            inv = {it: inv[it] + _dot(inv[it].astype(BF16), pw[it]) for it in items}
    for k in range(1, len(same)):
        off = jnp.logical_and(same[k], jnp.logical_not(same[k - 1]))
        invb = {it: inv[it].astype(BF16) for it in items}
        tmp = {it: _dot(invb[it], jnp.where(off, lm[it], 0.0).astype(BF16)).astype(BF16) for it in items}
        inv = {it: inv[it] - _dot(tmp[it], invb[it]) for it in items}

    u, w = {}, {}
    for it in items:
        d, ck, hd = it
        s_ = sc[d, ck]
        bc = s_["beta"][:, hd:hd + 1]
        rhs = jnp.concatenate([v_of(*it) * bc, k_of(*it) * (bc * s_["e_c"][:, hd:hd + 1])], axis=1)
        sol = _dot(inv[it].astype(BF16), rhs.astype(BF16))
        u[it] = sol[:, 0:GDN_DV]
        w[it] = sol[:, GDN_DV:GDN_DV + GDN_DK].astype(BF16)

    state = {(d, hd): s_ref[d, hd] for d in range(2) for hd in range(GDN_H)}
    for pos in range(nch):
        cur = [(d, pos if d == 0 else nch - 1 - pos, hd) for d in range(2) for hd in range(GDN_H)]
        sb = {it: state[it[0], it[2]].astype(BF16) for it in cur}
        wq = {it: _dot(jnp.concatenate(
            [w[it], (q_of(*it) * sc[it[0], it[1]]["e_c"][:, it[2]:it[2] + 1]).astype(BF16)], axis=0), sb[it]) for it in cur}
        for it in cur:
            d, ck, hd = it
            s_ = sc[d, ck]
            vpb = (u[it] - wq[it][0:c]).astype(BF16)
            refs[d][3][rows(ck), hd * GDN_DV:(hd + 1) * GDN_DV] = wq[it][c:2 * c] + _dot(attn[it], vpb)
            state[d, hd] = (s_["e_last"][:, hd:hd + 1] * state[d, hd]
                            + _dot_tn((k_of(*it) * s_["kdec"][:, hd:hd + 1]).astype(BF16), vpb))
    for (d, hd), s in state.items():
        s_ref[d, hd] = s


def _gdn_call(qkv, small, small_t, a_log, dt_bias, n_ctx):
    lt = qkv.shape[0]
    r = MIX_ROWS
    nb, ncb = lt // r, n_ctx // r
    tri, trit = _tri_tables(GDN_CHUNK)
    params = (dt_bias[:, None, :], dt_bias[:, :, None], a_log[:, None, :], a_log[:, :, None])
    xs = _dir_specs(r, GDN_QKV, ncb, nb)
    ss = _dir_specs(r, SMALL_COLS, ncb, nb)
    ts = _dir_specs_t(small_t.shape[0], r, ncb, nb)
    os_ = _dir_specs(r, GDN_H * GDN_DV, ncb, nb)
    return pl.pallas_call(
        _gdn_kernel,
        grid=(nb,),
        in_specs=[*xs, *ss, *ts, _full(tri.shape), _full(trit.shape)] + [_full(p.shape) for p in params],
        out_specs=list(os_),
        out_shape=[jax.ShapeDtypeStruct((lt, GDN_H * GDN_DV), F32)] * 2,
        scratch_shapes=[pltpu.VMEM((2, GDN_H, GDN_DK, GDN_DV), F32)],
        compiler_params=_cparams(("arbitrary",)),
        name="gdn_scan",
    )(qkv, qkv, small, small, small_t, small_t, tri, trit, *params)


def _head_rms(y, n_heads, width):
    parts = []
    for hd in range(n_heads):
        t = y[:, hd * width:(hd + 1) * width]
        parts.append(t * lax.rsqrt(jnp.mean(t * t, axis=-1, keepdims=True) + EPS))
    return jnp.concatenate(parts, axis=1)


def _merge_kernel(xc_ref, xl_ref, mod_ref, g_ref, wg_ref, wb_ref, wo_ref,
                  lf_ref, lb_ref, lg_ref, rf_ref, rb_ref, rg_ref, gf_ref, gb_ref, gz_ref, sf_ref, sb_ref, sz_ref,
                  rn_ref, gn_ref, sn_ref, o_ref, *, n_ctx_tiles, tile0):
    is_ctx = pl.program_id(0) + tile0 < n_ctx_tiles
    d = xc_ref.shape[1]
    x = jnp.where(is_ctx, xc_ref[...], xl_ref[...])
    h = _rms_mod(x, g_ref[...], _pick_mod(mod_ref, is_ctx, 0, d), _pick_mod(mod_ref, is_ctx, 1, d)).astype(BF16)
    def lru_y():
        return (lf_ref[...] + lb_ref[...]) * _gelu_tanh(lg_ref[...])

    def ret_y():
        g_cols = slice(2 * RET_H * RET_DK + RET_H * RET_DV, RET_COLS)
        return _head_rms(rf_ref[...] + rb_ref[...], RET_H, RET_DV) * rn_ref[...] * _silu(rg_ref[:, g_cols])

    def gdn_y():
        return _head_rms(gf_ref[...] + gb_ref[...], GDN_H, GDN_DV) * gn_ref[...] * _silu(gz_ref[...])

    def ssd_y():
        ssd = (sf_ref[...] + sb_ref[...]) * _silu(sz_ref[...])
        return ssd * lax.rsqrt(jnp.mean(ssd * ssd, axis=-1, keepdims=True) + EPS) * sn_ref[...]

    merged = None
    for nb_, branch_y in enumerate((lru_y, ret_y, gdn_y, ssd_y)):
        pre = _dot(h, wg_ref[:, nb_ * d:(nb_ + 1) * d])
        t = _sigmoid(pre) * _dot(branch_y().astype(BF16), wb_ref[nb_])
        merged = t if merged is None else merged + t
    out = _dot(merged.astype(BF16), wo_ref[...])
    o_ref[...] = x + _pick_mod(mod_ref, is_ctx, 2, d) * out


def _merge_call(x_ctx, x_lat, lat_row0, mod, g, w_gate, w_branch, w_out, branches, norms, n_ctx, row_start):
    d = x_ctx.shape[1]
    tm = ROW_TILE
    lt = n_ctx + x_lat.shape[0] - lat_row0
    t0 = row_start // tm
    nt = lt // tm - t0
    row = lambda wd: pl.BlockSpec((tm, wd), lambda i: (i + t0, 0))
    in_specs = [*_row_sources(x_ctx, x_lat, lat_row0, n_ctx, tm, t0), _full(mod.shape), _full(g.shape),
                _full(w_gate.shape), _full(w_branch.shape), _full(w_out.shape)]
    in_specs += [row(b.shape[1]) for b in branches]
    in_specs += [_full(n.shape) for n in norms]
    return pl.pallas_call(
        functools.partial(_merge_kernel, n_ctx_tiles=n_ctx // tm, tile0=t0),
        grid=(nt,),
        in_specs=in_specs,
        out_specs=pl.BlockSpec((tm, d), lambda i: (i, 0)),
        out_shape=jax.ShapeDtypeStruct((nt * tm, d), F32),
        compiler_params=_cparams(("parallel",)),
        name="merge_out",
    )(x_ctx, x_lat, mod, g, w_gate, w_branch, w_out, *branches, *norms)


def _ffn_kernel(x_ref, mod_ref, g_ref, wg_ref, wu_ref, wd_ref, fn_ref, o_ref, *, n_ctx_tiles, final):
    is_ctx = pl.program_id(0) < n_ctx_tiles
    d = x_ref.shape[1]
    x = x_ref[...]
    h = _rms_mod(x, g_ref[...], _pick_mod(mod_ref, is_ctx, 3, d), _pick_mod(mod_ref, is_ctx, 4, d)).astype(BF16)
    act = (_silu(_dot(h, wg_ref[...])) * _dot(h, wu_ref[...])).astype(BF16)
    y = x + _pick_mod(mod_ref, is_ctx, 5, d) * _dot(act, wd_ref[...])
    if final:
        y = y * lax.rsqrt(jnp.mean(y * y, axis=-1, keepdims=True) + EPS) * fn_ref[...]
    o_ref[...] = y


def _ffn_call(x_rows, mod, g, wg, wu, wd, final_g, n_ctx_rows, final):
    n, d = x_rows.shape
    tm = ROW_TILE
    row = pl.BlockSpec((tm, d), lambda i: (i, 0))
    return pl.pallas_call(
        functools.partial(_ffn_kernel, n_ctx_tiles=n_ctx_rows // tm, final=final),
        grid=(n // tm,),
        in_specs=[row, _full(mod.shape), _full(g.shape), _full(wg.shape), _full(wu.shape), _full(wd.shape), _full(final_g.shape)],
        out_specs=row,
        out_shape=jax.ShapeDtypeStruct((n, d), F32),
        compiler_params=_cparams(("parallel",)),
        name="dense_swiglu",
    )(x_rows, mod, g, wg, wu, wd, final_g)


def _router_kernel(x_ref, mod_ref, g_ref, rt_ref, tri_ref, h_o, gate_o, slot_o, cnt_o):
    d = x_ref.shape[1]
    h = _rms_mod(x_ref[...], g_ref[...], mod_ref[0:1, 3 * d:4 * d], mod_ref[0:1, 4 * d:5 * d])
    h_o[...] = h.astype(BF16)
    logits = _dot_nt(rt_ref[...], h, HI)
    e, b = logits.shape
    eid = lax.broadcasted_iota(jnp.int32, (e, b), 0)
    m1 = jnp.max(logits, axis=0, keepdims=True)
    i1 = jnp.min(jnp.where(logits == m1, eid, e), axis=0, keepdims=True)
    rest = jnp.where(eid == i1, -jnp.inf, logits)
    m2 = jnp.max(rest, axis=0, keepdims=True)
    i2 = jnp.min(jnp.where(rest == m2, eid, e), axis=0, keepdims=True)
    t = jnp.exp(m2 - m1)
    p1 = 1.0 / (1.0 + t)
    p2 = t / (1.0 + t)
    sel1 = eid == i1
    sel2 = eid == i2
    gate_o[...] = jnp.where(sel1, p1, jnp.where(sel2, p2, 0.0))
    sel = jnp.logical_or(sel1, sel2)
    rank = _dot(sel.astype(BF16), tri_ref[...])
    slot_o[...] = jnp.where(sel, rank, -1.0).astype(jnp.int32)
    cnt = jnp.sum(sel.astype(F32), axis=1, keepdims=True)
    cnt_o[...] = jnp.broadcast_to(cnt, (e, LANES))[None].astype(jnp.int32)


def _router_call(xl, mod, g, router_t):
    n, d = xl.shape
    b = MOE_BLOCK
    nblk = n // b
    e = router_t.shape[0]
    pos = jnp.arange(b)
    tri = (pos[:, None] < pos[None, :]).astype(BF16)
    return pl.pallas_call(
        _router_kernel,
        grid=(nblk,),
        in_specs=[pl.BlockSpec((b, d), lambda i: (i, 0)), _full(mod.shape), _full(g.shape), _full(router_t.shape), _full(tri.shape)],
        out_specs=[pl.BlockSpec((b, d), lambda i: (i, 0)), pl.BlockSpec((e, b), lambda i: (0, i)),
                   pl.BlockSpec((e, b), lambda i: (0, i)), pl.BlockSpec((1, e, LANES), lambda i: (i, 0, 0))],
        out_shape=[jax.ShapeDtypeStruct((n, d), BF16), jax.ShapeDtypeStruct((e, n), F32),
                   jax.ShapeDtypeStruct((e, n), jnp.int32), jax.ShapeDtypeStruct((nblk, e, LANES), jnp.int32)],
        compiler_params=_cparams(("parallel",)),
        name="moe_router",
    )(xl, mod, g, router_t, tri)


def _moe_kernel(nfull_ref, tail_ref, x_ref, h_ref, gate_ref, slot_ref, mod_ref, fn_ref, wg_ref, wu_ref, wd_ref, o_ref,
                hs_ref, ys_ref, *, final):
    bi, ei, fi = pl.program_id(0), pl.program_id(1), pl.program_id(2)
    n_e, n_f = pl.num_programs(1), pl.num_programs(2)
    b, d = x_ref.shape
    nfull = nfull_ref[bi * n_e + ei]
    tail = tail_ref[bi * n_e + ei]
    tail_start = pl.multiple_of(nfull * MOE_SLOTS, MOE_SLOTS)
    tail_sizes = tuple(range(MOE_TAIL, MOE_SLOTS, MOE_TAIL))

    def for_tiles(fn):
        def body(j, carry):
            fn(pl.multiple_of(j * MOE_SLOTS, MOE_SLOTS), MOE_SLOTS)
            return carry
        lax.fori_loop(0, nfull, body, 0)
        for k, size in enumerate(tail_sizes):
            pl.when(tail == k + 1)(functools.partial(fn, tail_start, size))

    def onehot(start, size):
        return slot_ref[pl.ds(ei, 1), :] == lax.broadcasted_iota(jnp.int32, (size, b), 0) + start

    @pl.when(jnp.logical_and(ei == 0, fi == 0))
    def _():
        o_ref[...] = jnp.zeros_like(o_ref)

    def gather(start, size):
        hs_ref[pl.ds(start, size), :] = _dot(onehot(start, size).astype(BF16), h_ref[...]).astype(BF16)

    pl.when(fi == 0)(functools.partial(for_tiles, gather))

    def expert(start, size):
        hs = hs_ref[pl.ds(start, size), :]
        act = (_silu(_dot(hs, wg_ref[0])) * _dot(hs, wu_ref[0])).astype(BF16)
        y = _dot(act, wd_ref[0])

        @pl.when(fi == 0)
        def _():
            ys_ref[pl.ds(start, size), :] = y

        @pl.when(fi != 0)
        def _():
            ys_ref[pl.ds(start, size), :] = ys_ref[pl.ds(start, size), :] + y

    for_tiles(expert)

    def scatter(start, size):
        oh = onehot(start, size)
        gs = jnp.sum(jnp.where(oh, gate_ref[pl.ds(ei, 1), :], 0.0), axis=1, keepdims=True)
        o_ref[...] += _dot_tn(oh.astype(BF16), (ys_ref[pl.ds(start, size), :] * gs).astype(BF16))

    pl.when(fi == n_f - 1)(functools.partial(for_tiles, scatter))

    @pl.when(jnp.logical_and(ei == n_e - 1, fi == n_f - 1))
    def _():
        y = x_ref[...] + mod_ref[0:1, 5 * d:6 * d] * o_ref[...]
        if final:
            y = y * lax.rsqrt(jnp.mean(y * y, axis=-1, keepdims=True) + EPS) * fn_ref[...]
        o_ref[...] = y


def _moe_call(xl, h2, gate_t, slot_t, counts, mod, final_g, wg, wu, wd, final):
    n, d = xl.shape
    b = MOE_BLOCK
    e, _, f = wg.shape
    fs = f // MOE_FSPLIT
    counts = counts.reshape(-1)
    tail = (counts % MOE_SLOTS + (MOE_TAIL - 1)) // MOE_TAIL
    nfull = counts // MOE_SLOTS + tail // (MOE_SLOTS // MOE_TAIL)
    tail = tail % (MOE_SLOTS // MOE_TAIL)
    once = pl.Buffered(1)
    grid_spec = pltpu.PrefetchScalarGridSpec(
        num_scalar_prefetch=2,
        grid=(n // b, e, MOE_FSPLIT),
        in_specs=[pl.BlockSpec((b, d), lambda i, j, k, *_: (i, 0), pipeline_mode=once),
                  pl.BlockSpec((b, d), lambda i, j, k, *_: (i, 0), pipeline_mode=once),
                  pl.BlockSpec((e, b), lambda i, j, k, *_: (0, i)),
                  pl.BlockSpec((e, b), lambda i, j, k, *_: (0, i)),
                  pl.BlockSpec(mod.shape, lambda i, j, k, *_: (0, 0)),
                  pl.BlockSpec(final_g.shape, lambda i, j, k, *_: (0, 0)),
                  pl.BlockSpec((1, d, fs), lambda i, j, k, *_: (j, 0, k)),
                  pl.BlockSpec((1, d, fs), lambda i, j, k, *_: (j, 0, k)),
                  pl.BlockSpec((1, fs, d), lambda i, j, k, *_: (j, k, 0))],
        out_specs=pl.BlockSpec((b, d), lambda i, j, k, *_: (i, 0)),
        scratch_shapes=[pltpu.VMEM((b, d), BF16), pltpu.VMEM((b, d), F32)],
    )
    return pl.pallas_call(
        functools.partial(_moe_kernel, final=final),
        grid_spec=grid_spec,
        out_shape=jax.ShapeDtypeStruct((n, d), F32),
        compiler_params=_cparams(("parallel", "arbitrary", "arbitrary")),
        name="moe_experts",
    )(nfull, tail, xl, h2, gate_t, slot_t, mod, final_g, wg, wu, wd)


def _mix_weight(w_mix):
    d = w_mix.shape[0]
    off = np.concatenate([[0], np.cumsum(MIX_SPLITS)])
    seg = lambda k: w_mix[:, off[k]:off[k + 1]]

    def halves(t):
        t = t.reshape(d, RET_H, RET_DK // 2, 2)
        return jnp.concatenate([t[..., 0], t[..., 1]], axis=-1).reshape(d, RET_H * RET_DK)

    n_small = MIX_SPLITS[8] + MIX_SPLITS[9] + MIX_SPLITS[12]
    return jnp.concatenate([
        seg(0), seg(6), seg(11),
        seg(1),
        halves(seg(2)), halves(seg(3)), seg(4), seg(5),
        seg(7), seg(10),
        seg(8), seg(9), seg(12), jnp.zeros((d, SMALL_COLS - n_small), w_mix.dtype),
    ], axis=1).astype(BF16)


def _block_diag(w):
    n, i, o = w.shape
    eye = jnp.eye(n, dtype=w.dtype)
    return (eye[:, None, :, None] * w[:, :, None, :]).reshape(n * i, n * o)


def _rotary_tables(n_lat, n_ctx):
    t = jnp.arange(n_ctx + n_lat) - n_ctx
    lat = t >= 0
    row = jnp.where(lat, t // GRID_W, 0).astype(F32)
    col = jnp.where(lat, t % GRID_W, 0).astype(F32)
    n_freq = RET_DK // 4
    freqs = ROPE_BASE ** (-jnp.arange(n_freq, dtype=F32) / n_freq)
    lane = np.arange(RET_H * RET_DK)
    pair = lane % (RET_DK // 2)
    freq_lane = freqs[pair % n_freq]
    by_row = jnp.asarray(pair < n_freq)
    first_half = jnp.asarray(lane % RET_DK < RET_DK // 2)
    ang = jnp.where(by_row[None, :], row[:, None], col[:, None]) * freq_lane[None, :]
    return jnp.cos(ang), jnp.where(first_half[None, :], -jnp.sin(ang), jnp.sin(ang))


def kernel(x, c, ctx, c_ctx, w_mod, b_mod, norm_mix, norm_ffn, w_in, lru_conv_w, lru_conv_b, lru_wa, lru_ba, lru_wx, lru_bx, lru_lambda, ret_norm, gdn_conv_w, gdn_a_log, gdn_dt_bias, gdn_norm, ssd_conv_w, ssd_conv_b, ssd_a_log, ssd_dt_bias, ssd_d, ssd_norm, w_branch, w_out, ffn_wg, ffn_wu, ffn_wd, moe_router, moe_wg, moe_wu, moe_wd, final_norm):
    assert x.shape[0] == 1 and c.shape[0] == 1 and ctx.shape[0] == 1
    depth = w_mod.shape[0]
    n_lat, d = x.shape[1], x.shape[2]
    n_ctx = ctx.shape[1]
    assert n_ctx % ROW_TILE == 0 and n_lat % ROW_TILE == 0 and n_ctx % MIX_ROWS == 0 and n_lat % MIX_ROWS == 0
    gate_cols = N_BRANCH * d
    cos_t, sin_t = _rotary_tables(n_lat, n_ctx)
    b_mod3 = b_mod[:, None, :]
    ct = jnp.stack([c[0], c_ctx], axis=1)
    final_g = final_norm[None, :]

    x_ctx, x_lat, lat_row0 = ctx[0], x[0], 0
    for layer in range(depth):
        ctx_out = layer < depth - 1
        last = layer == depth - 1
        mod = _mod_call(ct, w_mod, b_mod3, layer)
        w_mix = _mix_weight(w_in[layer][:, gate_cols:])
        conv_w = jnp.concatenate([lru_conv_w[layer], gdn_conv_w[layer], ssd_conv_w[layer]], axis=1)
        conv_b = jnp.concatenate([lru_conv_b[layer], jnp.zeros((GDN_QKV,), F32), ssd_conv_b[layer]])[None, :]
        (lru_u, gdn_qkv, ssd_xbc, p_lg, p_ret, p_gz, p_sz, p_small) = _proj_call(
            x_ctx, x_lat, lat_row0, mod, norm_mix[layer][None, :], w_mix, cos_t, sin_t, conv_w, conv_b, n_ctx)
        small_t = p_small[:, 0:32].T

        lru_w = jnp.stack([jnp.concatenate([_block_diag(lru_wa[layer, dd]), _block_diag(lru_wx[layer, dd])], axis=1)
                           for dd in range(2)]).astype(BF16)
        lru_b = jnp.concatenate([lru_ba[layer], lru_bx[layer]], axis=1)[:, None, :]
        lru_f, lru_b_ = _lru_call(lru_u, lru_w, lru_b, lru_lambda[layer][:, None, :], n_ctx)
        ret_f, ret_b = _ret_call(p_ret, n_ctx)
        gdn_f, gdn_b = _gdn_call(gdn_qkv, p_small, small_t, gdn_a_log[layer], gdn_dt_bias[layer], n_ctx)
        ssd_f, ssd_b = _ssd_call(ssd_xbc, p_small, small_t, ssd_dt_bias[layer], ssd_a_log[layer], ssd_d[layer], n_ctx)

        branches = (lru_f, lru_b_, p_lg, ret_f, ret_b, p_ret, gdn_f, gdn_b, p_gz, ssd_f, ssd_b, p_sz)
        norms = (ret_norm[layer][None, :], jnp.tile(gdn_norm[layer], GDN_H)[None, :], ssd_norm[layer][None, :])
        row_start = 0 if ctx_out else n_ctx
        x_rows = _merge_call(x_ctx, x_lat, lat_row0, mod, norm_mix[layer][None, :],
                             w_in[layer][:, :gate_cols].astype(BF16), w_branch[layer].astype(BF16),
                             w_out[layer].astype(BF16), branches, norms, n_ctx, row_start)
        n_ctx_rows = n_ctx - row_start
        j = layer // 2
        if layer % 2 == 0:
            x_rows = _ffn_call(x_rows, mod, norm_ffn[layer][None, :], ffn_wg[j].astype(BF16), ffn_wu[j].astype(BF16),
                               ffn_wd[j].astype(BF16), final_g, n_ctx_rows, last)
        else:
            assert not ctx_out, "expert layers that must also emit context tokens are not supported"
            assert x_rows.shape[0] % MOE_BLOCK == 0 and MOE_BLOCK % MOE_SLOTS == 0 and MOE_SLOTS % MOE_TAIL == 0
            h2, gate_t, slot_t, cnt = _router_call(x_rows, mod, norm_ffn[layer][None, :], moe_router[j].T)
            x_rows = _moe_call(x_rows, h2, gate_t, slot_t, cnt[:, :, 0], mod, final_g, moe_wg[j].astype(BF16),
                               moe_wu[j].astype(BF16), moe_wd[j].astype(BF16), last)
        x_ctx, x_lat, lat_row0 = x_rows, x_rows, n_ctx - row_start
    out = x_rows[x_rows.shape[0] - n_lat:]
    return out[None]
```

```python
import functools
import math

import numpy as np
import jax
import jax.numpy as jnp
from jax import lax
from jax.experimental import pallas as pl
from jax.experimental.pallas import tpu as pltpu

F32 = jnp.float32
BF16 = jnp.bfloat16
HI = lax.Precision.HIGHEST

EPS = 1e-6
GRID_W = 64
N_BRANCH = 4
BRANCH_W = 512
CONV_W = 4
LRU_W = 512
LRU_BLOCKS = 8
LRU_C = 8.0
RET_H, RET_DK, RET_DV, RET_CHUNK = 4, 64, 128, 128
ROPE_BASE = 10000.0
GDN_H, GDN_DK, GDN_DV, GDN_CHUNK = 4, 128, 128, 64
SSD_H, SSD_P, SSD_G, SSD_N, SSD_CHUNK = 8, 64, 2, 64, 128
N_EXPERTS = 8
GDN_QKV = 2 * GDN_H * GDN_DK + GDN_H * GDN_DV
SSD_XBC = SSD_H * SSD_P + 2 * SSD_G * SSD_N
MIX_SPLITS = (LRU_W, LRU_W, RET_H * RET_DK, RET_H * RET_DK, RET_H * RET_DV, RET_H * RET_DV,
              GDN_QKV, GDN_H * GDN_DV, 2 * GDN_H, 2 * GDN_H, SSD_H * SSD_P, SSD_XBC, 2 * SSD_H)

LANES = 128
SUBLANES = 8
VMEM_LIMIT = 56 * 1024 * 1024

ROW_TILE = 256
MIX_ROWS = 256
MOE_BLOCK = 1024
MOE_SLOTS = 512
GDN_INV_BLOCK = 16
MOE_TAIL = 64
MOE_FSPLIT = 2

CONV_COLS = LRU_W + GDN_QKV + SSD_XBC
RET_COLS = 2 * RET_H * RET_DK + 2 * RET_H * RET_DV
SMALL_COLS = LANES


def _cparams(sem):
    return pltpu.CompilerParams(dimension_semantics=sem, vmem_limit_bytes=VMEM_LIMIT)


def _dot(a, b, precision=None):
    return jnp.dot(a, b, preferred_element_type=F32, precision=precision)


def _dot_nt(a, b, precision=None):
    return lax.dot_general(a, b, (((1,), (1,)), ((), ())), preferred_element_type=F32, precision=precision)


def _dot_tn(a, b, precision=None):
    return lax.dot_general(a, b, (((0,), (0,)), ((), ())), preferred_element_type=F32, precision=precision)


def _sigmoid(x):
    return 0.5 * jnp.tanh(0.5 * x) + 0.5


def _silu(x):
    return x * _sigmoid(x)


def _softplus(x):
    return jnp.maximum(x, 0.0) + jnp.log1p(jnp.exp(-jnp.abs(x)))


def _gelu_tanh(x):
    return 0.5 * x * (1.0 + jnp.tanh(math.sqrt(2.0 / math.pi) * (x + 0.044715 * (x * x * x))))


def _rms_mod(x, g, shift, scale):
    ms = jnp.mean(x * x, axis=-1, keepdims=True)
    return (x * lax.rsqrt(ms + EPS) * g) * (1.0 + scale) + shift


def _pick_mod(mod_ref, is_ctx, k, d):
    return jnp.where(is_ctx, mod_ref[1:2, k * d:(k + 1) * d], mod_ref[0:1, k * d:(k + 1) * d])


def _full(shape):
    n = len(shape)
    return pl.BlockSpec(shape, lambda *_: (0,) * n)


def _mod_kernel(ct_ref, w_ref, b_ref, o_ref):
    s = _silu(ct_ref[...])
    w = w_ref[0]
    b = b_ref[0]
    o_ref[0:1, :] = jnp.sum(s[:, 0:1] * w, axis=0, keepdims=True) + b
    o_ref[1:2, :] = jnp.sum(s[:, 1:2] * w, axis=0, keepdims=True) + b


def _mod_call(ct, w, b, layer):
    _, d, n = w.shape
    tn = 512
    return pl.pallas_call(
        _mod_kernel,
        grid=(n // tn,),
        in_specs=[_full((d, 2)), pl.BlockSpec((1, d, tn), lambda j: (layer, 0, j)),
                  pl.BlockSpec((1, 1, tn), lambda j: (layer, 0, j))],
        out_specs=pl.BlockSpec((2, tn), lambda j: (0, j)),
        out_shape=jax.ShapeDtypeStruct((2, n), F32),
        compiler_params=_cparams(("arbitrary",)),
        name="adaln_mod",
    )(ct, w, b)


def _proj_kernel(xc_ref, xl_ref, xcp_ref, xlp_ref, xcn_ref, xln_ref, mod_ref, g_ref, w_ref, rrow_ref, rcol_ref, cw_ref, cb_ref,
                 lru_o, gdn_o, ssd_o, lg_o, ret_o, gz_o, sz_o, sm_o, *, n_ctx_tiles):
    i = pl.program_id(0)
    is_ctx = i < n_ctx_tiles
    d = xc_ref.shape[1]
    tm = xc_ref.shape[0]
    shift = _pick_mod(mod_ref, is_ctx, 0, d)
    scale = _pick_mod(mod_ref, is_ctx, 1, d)
    hf = _rms_mod(jnp.where(is_ctx, xc_ref[...], xl_ref[...]), g_ref[...], shift, scale)
    h = hf.astype(BF16)

    def mm(a, b):
        return _dot(h, w_ref[:, a:b])

    has_prev = jnp.logical_and(i != 0, i != n_ctx_tiles).astype(F32)
    has_next = jnp.logical_and(i != n_ctx_tiles - 1, i != pl.num_programs(0) - 1).astype(F32)
    h_ext = jnp.concatenate([_rms_mod(jnp.where(is_ctx, xcp_ref[...], xlp_ref[...]), g_ref[...], shift, scale), hf,
                             _rms_mod(jnp.where(is_ctx, xcn_ref[...], xln_ref[...]), g_ref[...], shift, scale)],
                            axis=0).astype(BF16)
    sub = lax.broadcasted_iota(jnp.int32, (1, SUBLANES, 1), 1)
    g = tm // SUBLANES

    def conv(c0, c1):
        wd = c1 - c0
        pe = _dot(h_ext, w_ref[:, c0:c1]).reshape(g + 2, SUBLANES, wd)
        u = pe[1:g + 1]
        ext = jnp.concatenate([pe[0:1] * has_prev, u, pe[g + 1:g + 2] * has_next], axis=0)
        r1 = pltpu.roll(ext, 1, axis=1)
        r7 = pltpu.roll(ext, SUBLANES - 1, axis=1)
        r6 = pltpu.roll(ext, SUBLANES - 2, axis=1)
        um1 = jnp.where(sub >= 1, r1[1:g + 1], r1[0:g])
        up1 = jnp.where(sub < SUBLANES - 1, r7[1:g + 1], r7[2:g + 2])
        up2 = jnp.where(sub < SUBLANES - 2, r6[1:g + 1], r6[2:g + 2])
        w = cw_ref[:, c0:c1]
        y = w[0:1] * um1 + w[1:2] * u + w[2:3] * up1 + w[3:4] * up2 + cb_ref[:, c0:c1]
        return y.reshape(tm, wd)

    step = 2 * LANES
    conv_jobs, plain_jobs = [], []

    def lru_job(c0):
        lru_o[:, c0:c0 + step] = conv(c0, c0 + step)

    def gdn_qk_job(c0):
        t2 = _silu(conv(LRU_W + c0, LRU_W + c0 + step))
        for k in range(step // GDN_DK):
            t = t2[:, k * GDN_DK:(k + 1) * GDN_DK]
            t = t * lax.rsqrt(jnp.sum(t * t, axis=-1, keepdims=True) + EPS)
            if c0 < GDN_H * GDN_DK:
                t = t * (GDN_DK ** -0.5)
            gdn_o[:, c0 + k * GDN_DK:c0 + (k + 1) * GDN_DK] = t

    def gdn_v_job(c0):
        gdn_o[:, c0:c0 + step] = _silu(conv(LRU_W + c0, LRU_W + c0 + step))

    def ssd_job(c0):
        ssd_o[:, c0:c0 + step] = _silu(conv(LRU_W + GDN_QKV + c0, LRU_W + GDN_QKV + c0 + step))

    conv_jobs += [functools.partial(lru_job, c0) for c0 in range(0, LRU_W, step)]
    conv_jobs += [functools.partial(gdn_qk_job, c0) for c0 in range(0, 2 * GDN_H * GDN_DK, step)]
    conv_jobs += [functools.partial(gdn_v_job, c0) for c0 in range(2 * GDN_H * GDN_DK, GDN_QKV, step)]
    conv_jobs += [functools.partial(ssd_job, c0) for c0 in range(0, SSD_XBC, step)]

    qk_w = 2 * RET_H * RET_DK
    half = RET_DK // 2
    lane = lax.broadcasted_iota(jnp.int32, (tm, RET_H * RET_DK), 1)
    first = (lane % RET_DK) < half

    by_row = (lane % half) < half // 2
    rows8 = (tm // SUBLANES, SUBLANES, RET_H * RET_DK)

    def table(k, rest):
        r = jnp.broadcast_to(rrow_ref[k][:, None, :], rows8).reshape(tm, RET_H * RET_DK)
        return jnp.where(by_row, r, jnp.where(is_ctx, rest, rcol_ref[k]))

    def rot(t):
        partner = jnp.where(first, pltpu.roll(t, RET_H * RET_DK - half, axis=1), pltpu.roll(t, half, axis=1))
        return t * table(0, 1.0) + partner * table(1, 0.0)

    c_lg = CONV_COLS
    c_ret = c_lg + LRU_W
    c_gz = c_ret + RET_COLS
    c_sz = c_gz + GDN_H * GDN_DV
    c_sm = c_sz + SSD_H * SSD_P

    def plain(o_ref, o0, c0, width):
        def job():
            o_ref[:, o0:o0 + width] = mm(c0, c0 + width)
        return job

    def ret_q_job():
        ret_o[:, 0:qk_w // 2] = rot(mm(c_ret, c_ret + qk_w // 2)) * (RET_DK ** -0.5)

    def ret_k_job():
        ret_o[:, qk_w // 2:qk_w] = rot(mm(c_ret + qk_w // 2, c_ret + qk_w))

    plain_jobs += [plain(lg_o, 0, c_lg, LRU_W), ret_q_job, ret_k_job,
                   plain(ret_o, qk_w, c_ret + qk_w, RET_H * RET_DV),
                   plain(ret_o, qk_w + RET_H * RET_DV, c_ret + qk_w + RET_H * RET_DV, RET_H * RET_DV),
                   plain(gz_o, 0, c_gz, GDN_H * GDN_DV), plain(sz_o, 0, c_sz, SSD_H * SSD_P),
                   plain(sm_o, 0, c_sm, SMALL_COLS)]
    for k in range(max(len(conv_jobs), len(plain_jobs))):
        if k < len(conv_jobs):
            conv_jobs[k]()
        if k < len(plain_jobs):
            plain_jobs[k]()


def _row_sources(x_ctx, x_lat, lat_row0, n_ctx, tm, tile0=0):
    d = x_ctx.shape[1]
    nct, ob = n_ctx // tm, lat_row0 // tm
    return (pl.BlockSpec((tm, d), lambda i: (jnp.minimum(i + tile0, nct - 1), 0)),
            pl.BlockSpec((tm, d), lambda i: (jnp.maximum(i + tile0 - nct, 0) + ob, 0)))


def _proj_call(x_ctx, x_lat, lat_row0, mod, g, w, rot_rows, rot_cols, conv_w, conv_b, n_ctx):
    d = x_ctx.shape[1]
    tm = ROW_TILE
    lt = n_ctx + x_lat.shape[0] - lat_row0
    nt, nct, ob = lt // tm, n_ctx // tm, lat_row0 // tm
    hb = tm // SUBLANES
    widths = (LRU_W, GDN_QKV, SSD_XBC, LRU_W, RET_COLS, GDN_H * GDN_DV, SSD_H * SSD_P, SMALL_COLS)
    row = lambda wd: pl.BlockSpec((tm, wd), lambda i: (i, 0))
    halo = lambda fn: pl.BlockSpec((SUBLANES, d), lambda i: (fn(i), 0))
    return pl.pallas_call(
        functools.partial(_proj_kernel, n_ctx_tiles=nct),
        grid=(nt,),
        in_specs=[*_row_sources(x_ctx, x_lat, lat_row0, n_ctx, tm),
                  halo(lambda i: jnp.clip(i * hb - 1, 0, nct * hb - 1)),
                  halo(lambda i: jnp.maximum((i - nct) * hb - 1, 0) + ob * hb),
                  halo(lambda i: jnp.minimum((i + 1) * hb, nct * hb - 1)),
                  halo(lambda i: jnp.clip((i + 1 - nct) * hb, 0, (nt - nct) * hb - 1) + ob * hb),
                  _full(mod.shape), _full(g.shape), _full(w.shape),
                  pl.BlockSpec((2, tm // SUBLANES, rot_rows.shape[2]), lambda i: (0, i, 0)), _full(rot_cols.shape),
                  _full(conv_w.shape), _full(conv_b.shape)],
        out_specs=[row(wd) for wd in widths],
        out_shape=[jax.ShapeDtypeStruct((lt, wd), F32) for wd in widths],
        compiler_params=_cparams(("parallel",)),
        name="mix_proj",
    )(x_ctx, x_lat, x_ctx, x_lat, x_ctx, x_lat, mod, g, w, rot_rows, rot_cols, conv_w, conv_b)


def _bwd_block(i, n_ctx_blocks, n_blocks):
    return jnp.where(i < n_ctx_blocks, n_ctx_blocks - 1 - i, n_blocks + n_ctx_blocks - 1 - i)


def _dir_specs(r, width, ncb, nb):
    return (pl.BlockSpec((r, width), lambda i: (i, 0)),
            pl.BlockSpec((r, width), lambda i: (_bwd_block(i, ncb, nb), 0)))


def _dir_specs_t(rows, r, ncb, nb):
    return (pl.BlockSpec((rows, r), lambda i: (0, i)),
            pl.BlockSpec((rows, r), lambda i: (0, _bwd_block(i, ncb, nb))))


def _chunk_order(d, n):
    return range(n) if d == 0 else range(n - 1, -1, -1)


def _lru_kernel(uf_ref, ub_ref, w_ref, b_ref, lam_ref, yf_o, yb_o, carry_ref):
    @pl.when(pl.program_id(0) == 0)
    def _():
        carry_ref[...] = jnp.zeros_like(carry_ref)

    r = uf_ref.shape[0]
    sub = lax.broadcasted_iota(jnp.int32, (1, SUBLANES, 1), 1)

    def run(d, u_ref, o_ref):
        u = u_ref[...]
        gates = _sigmoid(_dot(u.astype(BF16), w_ref[d]) + b_ref[d])
        rg = gates[:, 0:LRU_W]
        ig = gates[:, LRU_W:2 * LRU_W]
        log_a = (-LRU_C) * rg * _softplus(-lam_ref[d])
        a = jnp.exp(log_a)
        b = jnp.sqrt(1.0 - a * a) * (ig * u)
        a = a.reshape(r // SUBLANES, SUBLANES, LRU_W)
        b = b.reshape(r // SUBLANES, SUBLANES, LRU_W)
        sh = 1
        while sh < SUBLANES:
            valid = sub >= sh if d == 0 else sub < SUBLANES - sh
            shift = sh if d == 0 else SUBLANES - sh
            a_s = pltpu.roll(a, shift, axis=1)
            b_s = pltpu.roll(b, shift, axis=1)
            b = jnp.where(valid, a * b_s + b, b)
            a = jnp.where(valid, a * a_s, a)
            sh *= 2
        carry = carry_ref[d, 0:1, :]
        for g in _chunk_order(d, r // SUBLANES):
            gs = slice(g * SUBLANES, (g + 1) * SUBLANES)
            h = a[g] * carry + b[g]
            o_ref[gs, :] = h
            carry = h[SUBLANES - 1:SUBLANES, :] if d == 0 else h[0:1, :]
        carry_ref[d, 0:1, :] = carry

    run(0, uf_ref, yf_o)
    run(1, ub_ref, yb_o)


def _lru_call(u, w, b, lam, n_ctx):
    lt = u.shape[0]
    r = MIX_ROWS
    nb, ncb = lt // r, n_ctx // r
    fs, bs = _dir_specs(r, LRU_W, ncb, nb)
    return pl.pallas_call(
        _lru_kernel,
        grid=(nb,),
        in_specs=[fs, bs, _full(w.shape), _full(b.shape), _full(lam.shape)],
        out_specs=[fs, bs],
        out_shape=[jax.ShapeDtypeStruct((lt, LRU_W), F32)] * 2,
        scratch_shapes=[pltpu.VMEM((2, 8, LRU_W), F32)],
        compiler_params=_cparams(("arbitrary",)),
        name="rglru_scan",
    )(u, u, w, b, lam)


def _ret_kernel(xf_ref, xb_ref, dmat_ref, qd_ref, kd_ref, sdec_ref, bd_ref, yf_o, yb_o, s_ref):
    @pl.when(pl.program_id(0) == 0)
    def _():
        s_ref[...] = jnp.zeros_like(s_ref)

    c = RET_CHUNK
    qw = RET_H * RET_DK
    lane_head = lax.broadcasted_iota(jnp.int32, (c, qw), 1) // RET_DK

    nch = xf_ref.shape[0] // c
    refs = ((xf_ref, yf_o), (xb_ref, yb_o))
    items = [(d, ck) for ck in range(nch) for d in range(2)]

    def rows(ck):
        return slice(ck * c, (ck + 1) * c)

    y_intra, upd, qdec = {}, {}, {}
    for it in items:
        d, ck = it
        x_ref = refs[d][0]
        rs = rows(ck)
        q = x_ref[rs, 0:qw]
        k = x_ref[rs, qw:2 * qw]
        kb = k.astype(BF16)
        vb = x_ref[rs, 2 * qw:2 * qw + RET_H * RET_DV].astype(BF16)
        upd[it] = bd_ref[...] * _dot_tn((k * kd_ref[d]).astype(BF16), vb)
        qdec[it] = (q * qd_ref[d]).astype(BF16)
        parts = []
        for hd in range(RET_H):
            qh = jnp.where(lane_head == hd, q, 0.0).astype(BF16)
            sc = _dot_nt(qh, kb) * dmat_ref[d, hd]
            parts.append(_dot(sc.astype(BF16), vb[:, hd * RET_DV:(hd + 1) * RET_DV]))
        y_intra[it] = parts

    state = [s_ref[0], s_ref[1]]
    for pos in range(nch):
        for d in range(2):
            ck = pos if d == 0 else nch - 1 - pos
            y_inter = _dot(qdec[d, ck], state[d].astype(BF16))
            for hd in range(RET_H):
                vs = slice(hd * RET_DV, (hd + 1) * RET_DV)
                refs[d][1][rows(ck), vs] = y_intra[d, ck][hd] + y_inter[:, vs]
            state[d] = sdec_ref[...] * state[d] + upd[d, ck]
    s_ref[0] = state[0]
    s_ref[1] = state[1]


def _ret_tables():
    c = RET_CHUNK
    lg = jnp.log(1.0 - 2.0 ** (-5.0 - jnp.arange(RET_H, dtype=F32)))
    pos = jnp.arange(c, dtype=F32)
    dist = pos[:, None] - pos[None, :]
    d_f = jnp.where(dist >= 0, jnp.exp(jnp.maximum(dist, 0.0)[None] * lg[:, None, None]), 0.0)
    d_b = jnp.where(dist < 0, jnp.exp(jnp.maximum(-dist, 0.0)[None] * lg[:, None, None]), 0.0)
    dmat = jnp.stack([d_f, d_b])
    rep = lambda t: jnp.repeat(t, RET_DK, axis=1)
    qd = jnp.stack([rep(jnp.exp((pos + 1.0)[:, None] * lg)), rep(jnp.exp((c - pos)[:, None] * lg))])
    kd = jnp.stack([rep(jnp.exp((c - 1.0 - pos)[:, None] * lg)), rep(jnp.exp(pos[:, None] * lg))])
    hk = jnp.repeat(jnp.arange(RET_H), RET_DK)
    hv = jnp.repeat(jnp.arange(RET_H), RET_DV)
    bd = (hk[:, None] == hv[None, :]).astype(F32)
    sdec = jnp.broadcast_to(jnp.repeat(jnp.exp(c * lg), RET_DK)[:, None], bd.shape)
    return dmat, qd, kd, sdec, bd


def _ret_call(x, n_ctx):
    lt = x.shape[0]
    r = MIX_ROWS
    nb, ncb = lt // r, n_ctx // r
    tabs = _ret_tables()
    fs, bs = _dir_specs(r, 2 * RET_H * RET_DK + RET_H * RET_DV, ncb, nb)
    os_f, os_b = _dir_specs(r, RET_H * RET_DV, ncb, nb)
    return pl.pallas_call(
        _ret_kernel,
        grid=(nb,),
        in_specs=[fs, bs] + [_full(t.shape) for t in tabs],
        out_specs=[os_f, os_b],
        out_shape=[jax.ShapeDtypeStruct((lt, RET_H * RET_DV), F32)] * 2,
        scratch_shapes=[pltpu.VMEM((2, RET_H * RET_DK, RET_H * RET_DV), F32)],
        compiler_params=_cparams(("arbitrary",)),
        name="retention_scan",
    )(x, x, *tabs)


def _ssd_kernel(xf_ref, xb_ref, smf_ref, smb_ref, stf_ref, stb_ref, tri_ref, trit_ref,
                dtb_c_ref, dtb_r_ref, alog_c_ref, alog_r_ref, dskip_ref, yf_o, yb_o, s_ref):
    @pl.when(pl.program_id(0) == 0)
    def _():
        s_ref[...] = jnp.zeros_like(s_ref)

    c = SSD_CHUNK
    xw = SSD_H * SSD_P
    gw = SSD_G * SSD_N
    lo = lax.broadcasted_iota(jnp.int32, (c, LANES), 1) < SSD_N
    row_lo = lax.broadcasted_iota(jnp.int32, (LANES, 1), 0) < SSD_N
    lane_lo = lax.broadcasted_iota(jnp.int32, (1, LANES), 1) < SSD_N
    bd = row_lo == lane_lo
    dt0 = 2 * 2 * GDN_H

    nch = xf_ref.shape[0] // c
    n_pair = SSD_H // 2
    refs = ((xf_ref, smf_ref, stf_ref, yf_o), (xb_ref, smb_ref, stb_ref, yb_o))
    mask = [tri_ref[d] > 0.0 for d in range(2)]

    def rows(ck):
        return slice(ck * c, (ck + 1) * c)

    sc = {}
    for d in range(2):
        _, sm_ref, st_ref, _ = refs[d]
        a_c = -jnp.exp(alog_c_ref[d])
        a_r = -jnp.exp(alog_r_ref[d])
        for ck in range(nch):
            rs = rows(ck)
            dtc = _softplus(sm_ref[rs, dt0 + d * SSD_H:dt0 + (d + 1) * SSD_H] + dtb_c_ref[d])
            dtr = _softplus(st_ref[dt0 + d * SSD_H:dt0 + (d + 1) * SSD_H, rs] + dtb_r_ref[d])
            cs_c = _dot(tri_ref[d], dtc * a_c, HI)
            cs_r = _dot(dtr * a_r, trit_ref[d], HI)
            tot = cs_c[c - 1:c, :] if d == 0 else cs_c[0:1, :]
            sc[d, ck] = dict(dtr=dtr, cs_c=cs_c, cs_r=cs_r, e_c=jnp.exp(cs_c), dec_c=jnp.exp(tot - cs_c) * dtc,
                             e_tot=jnp.exp(tot))

    cb, c_dup, b_dup = {}, {}, {}
    for d in range(2):
        x_ref = refs[d][0]
        for ck in range(nch):
            rs = rows(ck)
            bm = x_ref[rs, xw:xw + gw]
            cm = x_ref[rs, xw + gw:xw + 2 * gw]
            bmb = bm.astype(BF16)
            b_roll = pltpu.roll(bm, SSD_N, axis=1)
            c_roll = pltpu.roll(cm, SSD_N, axis=1)
            for g in range(SSD_G):
                keep = lo if g == 0 else jnp.logical_not(lo)
                cb[d, ck, g] = _dot_nt(jnp.where(keep, cm, 0.0).astype(BF16), bmb)
                c_dup[d, ck, g] = jnp.where(keep, cm, c_roll)
                b_dup[d, ck, g] = jnp.where(keep, bm, b_roll)

    items = [(d, ck, m) for ck in range(nch) for d in range(2) for m in range(n_pair)]
    y_intra, upd, cq = {}, {}, {}
    for it in items:
        d, ck, m = it
        s_ = sc[d, ck]
        g = m // (n_pair // SSD_G)
        h0, h1 = 2 * m, 2 * m + 1
        scores = []
        for hd in (h0, h1):
            seg = s_["cs_c"][:, hd:hd + 1] - s_["cs_r"][hd:hd + 1, :]
            lmat = jnp.where(mask[d], jnp.exp(jnp.where(mask[d], seg, 0.0)), 0.0)
            scores.append(cb[d, ck, g] * lmat * s_["dtr"][hd:hd + 1, :])
        scb = jnp.concatenate(scores, axis=1).astype(BF16)
        ls = slice(m * LANES, (m + 1) * LANES)
        xp = refs[d][0][rows(ck), ls]
        xs = jnp.concatenate([jnp.where(lo, xp, 0.0), jnp.where(lo, 0.0, xp)], axis=0).astype(BF16)
        y = _dot(scb, xs)
        if d == 0:
            y = y + dskip_ref[:, ls] * xp
        y_intra[it] = y
        dec_pair = jnp.where(lo, s_["dec_c"][:, h0:h0 + 1], s_["dec_c"][:, h1:h1 + 1])
        upd[it] = jnp.where(bd, _dot_tn((b_dup[d, ck, g] * dec_pair).astype(BF16), xp.astype(BF16)), 0.0)
        e_pair = jnp.where(lo, s_["e_c"][:, h0:h0 + 1], s_["e_c"][:, h1:h1 + 1])
        cq[it] = (c_dup[d, ck, g] * e_pair).astype(BF16)

    state = {(d, m): s_ref[d, m] for d in range(2) for m in range(n_pair)}
    for pos in range(nch):
        for d in range(2):
            ck = pos if d == 0 else nch - 1 - pos
            e_tot = sc[d, ck]["e_tot"]
            for m in range(n_pair):
                it = (d, ck, m)
                st = state[d, m]
                refs[d][3][rows(ck), m * LANES:(m + 1) * LANES] = y_intra[it] + _dot(cq[it], st.astype(BF16))
                sdec = jnp.where(row_lo, e_tot[:, 2 * m:2 * m + 1], e_tot[:, 2 * m + 1:2 * m + 2])
                state[d, m] = sdec * st + upd[it]
    for (d, m), st in state.items():
        s_ref[d, m] = st


def _tri_tables(c):
    pos = jnp.arange(c)
    lower = (pos[:, None] >= pos[None, :]).astype(F32)
    tri = jnp.stack([lower, lower.T])
    trit = jnp.stack([lower.T, lower])
    return tri, trit


def _ssd_call(xbc, small, small_t, dt_bias, a_log, d_skip, n_ctx):
    lt = xbc.shape[0]
    r = MIX_ROWS
    nb, ncb = lt // r, n_ctx // r
    tri, trit = _tri_tables(SSD_CHUNK)
    params = (dt_bias[:, None, :], dt_bias[:, :, None], a_log[:, None, :], a_log[:, :, None],
              jnp.repeat(d_skip, SSD_P)[None, :])
    xs = _dir_specs(r, SSD_XBC, ncb, nb)
    ss = _dir_specs(r, SMALL_COLS, ncb, nb)
    ts = _dir_specs_t(small_t.shape[0], r, ncb, nb)
    os_ = _dir_specs(r, SSD_H * SSD_P, ncb, nb)
    return pl.pallas_call(
        _ssd_kernel,
        grid=(nb,),
        in_specs=[*xs, *ss, *ts, _full(tri.shape), _full(trit.shape)] + [_full(p.shape) for p in params],
        out_specs=list(os_),
        out_shape=[jax.ShapeDtypeStruct((lt, SSD_H * SSD_P), F32)] * 2,
        scratch_shapes=[pltpu.VMEM((2, SSD_H // 2, 2 * SSD_N, 2 * SSD_P), F32)],
        compiler_params=_cparams(("arbitrary",)),
        name="ssd_scan",
    )(xbc, xbc, small, small, small_t, small_t, tri, trit, *params)


def _gdn_kernel(xf_ref, xb_ref, smf_ref, smb_ref, stf_ref, stb_ref, tri_ref, trit_ref,
                dtb_c_ref, dtb_r_ref, alog_c_ref, alog_r_ref, yf_o, yb_o, s_ref):
    @pl.when(pl.program_id(0) == 0)
    def _():
        s_ref[...] = jnp.zeros_like(s_ref)

    c = GDN_CHUNK
    kw = GDN_H * GDN_DK
    ri = lax.broadcasted_iota(jnp.int32, (c, c), 0)
    ci_ = lax.broadcasted_iota(jnp.int32, (c, c), 1)
    eye = (ri == ci_).astype(F32)

    nch = xf_ref.shape[0] // c
    refs = ((xf_ref, smf_ref, stf_ref, yf_o), (xb_ref, smb_ref, stb_ref, yb_o))
    incl = [tri_ref[d] > 0.0 for d in range(2)]
    strict = [jnp.logical_and(incl[d], ri != ci_) for d in range(2)]

    def rows(ck):
        return slice(ck * c, (ck + 1) * c)

    def q_of(d, ck, hd):
        return refs[d][0][rows(ck), hd * GDN_DK:(hd + 1) * GDN_DK]

    def k_of(d, ck, hd):
        return refs[d][0][rows(ck), kw + hd * GDN_DK:kw + (hd + 1) * GDN_DK]

    def v_of(d, ck, hd):
        return refs[d][0][rows(ck), 2 * kw + hd * GDN_DV:2 * kw + (hd + 1) * GDN_DV]

    sc = {}
    for d in range(2):
        _, sm_ref, st_ref, _ = refs[d]
        for ck in range(nch):
            rs = rows(ck)
            a_c = sm_ref[rs, d * GDN_H:(d + 1) * GDN_H]
            b_c = sm_ref[rs, 2 * GDN_H + d * GDN_H:2 * GDN_H + (d + 1) * GDN_H]
            a_r = st_ref[d * GDN_H:(d + 1) * GDN_H, rs]
            g_c = -jnp.exp(alog_c_ref[d]) * _softplus(a_c + dtb_c_ref[d])
            g_r = -jnp.exp(alog_r_ref[d]) * _softplus(a_r + dtb_r_ref[d])
            gcs_c = _dot(tri_ref[d], g_c, HI)
            gcs_r = _dot(g_r, trit_ref[d], HI)
            g_last = gcs_c[c - 1:c, :] if d == 0 else gcs_c[0:1, :]
            sc[d, ck] = dict(beta=_sigmoid(b_c), gcs_c=gcs_c, gcs_r=gcs_r, e_c=jnp.exp(gcs_c),
                             kdec=jnp.exp(g_last - gcs_c), e_last=jnp.exp(g_last))

    items = [(d, ck, hd) for ck in range(nch) for d in range(2) for hd in range(GDN_H)]

    lm, attn = {}, {}
    for it in items:
        d, ck, hd = it
        s_ = sc[d, ck]
        seg = s_["gcs_c"][:, hd:hd + 1] - s_["gcs_r"][hd:hd + 1, :]
        dmat = jnp.where(incl[d], jnp.exp(jnp.where(incl[d], seg, 0.0)), 0.0)
        kh = k_of(*it)
        both = _dot_nt(jnp.concatenate([kh * s_["beta"][:, hd:hd + 1], q_of(*it)], axis=0).astype(BF16), kh.astype(BF16))
        lm[it] = jnp.where(strict[d], both[0:c] * dmat, 0.0)
        attn[it] = (both[c:2 * c] * dmat).astype(BF16)

    blk = GDN_INV_BLOCK
    same = [(ri // (blk << k)) == (ci_ // (blk << k)) for k in range(int(math.log2(c // blk)) + 1)]
    diag = {it: jnp.where(same[0], lm[it], 0.0) for it in items}
    inv = {it: eye - diag[it] for it in items}
    pw = {it: diag[it].astype(BF16) for it in items}
    pw = {it: _dot(pw[it], pw[it]).astype(BF16) for it in items}
    n_sq = int(math.log2(blk)) - 1
    for step in range(n_sq):
        if step < n_sq - 1:
            both = {it: _dot(jnp.concatenate([inv[it].astype(BF16), pw[it]], axis=0), pw[it]) for it in items}
            inv = {it: inv[it] + both[it][0:c] for it in items}
            pw = {it: both[it][c:2 * c].astype(BF16) for it in items}
        else:
            inv = {it: inv[it] + _dot(inv[it].astype(BF16), pw[it]) for it in items}
    for k in range(1, len(same)):
        off = jnp.logical_and(same[k], jnp.logical_not(same[k - 1]))
        invb = {it: inv[it].astype(BF16) for it in items}
        tmp = {it: _dot(invb[it], jnp.where(off, lm[it], 0.0).astype(BF16)).astype(BF16) for it in items}
        inv = {it: inv[it] - _dot(tmp[it], invb[it]) for it in items}

    u, w = {}, {}
    for it in items:
        d, ck, hd = it
        s_ = sc[d, ck]
        bc = s_["beta"][:, hd:hd + 1]
        rhs = jnp.concatenate([v_of(*it) * bc, k_of(*it) * (bc * s_["e_c"][:, hd:hd + 1])], axis=1)
        sol = _dot(inv[it].astype(BF16), rhs.astype(BF16))
        u[it] = sol[:, 0:GDN_DV]
        w[it] = sol[:, GDN_DV:GDN_DV + GDN_DK].astype(BF16)

    state = {(d, hd): s_ref[d, hd] for d in range(2) for hd in range(GDN_H)}
    for pos in range(nch):
        cur = [(d, pos if d == 0 else nch - 1 - pos, hd) for d in range(2) for hd in range(GDN_H)]
        sb = {it: state[it[0], it[2]].astype(BF16) for it in cur}
        wq = {it: _dot(jnp.concatenate(
            [w[it], (q_of(*it) * sc[it[0], it[1]]["e_c"][:, it[2]:it[2] + 1]).astype(BF16)], axis=0), sb[it]) for it in cur}
        for it in cur:
            d, ck, hd = it
            s_ = sc[d, ck]
            vpb = (u[it] - wq[it][0:c]).astype(BF16)
            refs[d][3][rows(ck), hd * GDN_DV:(hd + 1) * GDN_DV] = wq[it][c:2 * c] + _dot(attn[it], vpb)
            state[d, hd] = (s_["e_last"][:, hd:hd + 1] * state[d, hd]
                            + _dot_tn((k_of(*it) * s_["kdec"][:, hd:hd + 1]).astype(BF16), vpb))
    for (d, hd), s in state.items():
        s_ref[d, hd] = s


def _gdn_call(qkv, small, small_t, a_log, dt_bias, n_ctx):
    lt = qkv.shape[0]
    r = MIX_ROWS
    nb, ncb = lt // r, n_ctx // r
    tri, trit = _tri_tables(GDN_CHUNK)
    params = (dt_bias[:, None, :], dt_bias[:, :, None], a_log[:, None, :], a_log[:, :, None])
    xs = _dir_specs(r, GDN_QKV, ncb, nb)
    ss = _dir_specs(r, SMALL_COLS, ncb, nb)
    ts = _dir_specs_t(small_t.shape[0], r, ncb, nb)
    os_ = _dir_specs(r, GDN_H * GDN_DV, ncb, nb)
    return pl.pallas_call(
        _gdn_kernel,
        grid=(nb,),
        in_specs=[*xs, *ss, *ts, _full(tri.shape), _full(trit.shape)] + [_full(p.shape) for p in params],
        out_specs=list(os_),
        out_shape=[jax.ShapeDtypeStruct((lt, GDN_H * GDN_DV), F32)] * 2,
        scratch_shapes=[pltpu.VMEM((2, GDN_H, GDN_DK, GDN_DV), F32)],
        compiler_params=_cparams(("arbitrary",)),
        name="gdn_scan",
    )(qkv, qkv, small, small, small_t, small_t, tri, trit, *params)


def _head_rms(y, n_heads, width):
    parts = []
    for hd in range(n_heads):
        t = y[:, hd * width:(hd + 1) * width]
        parts.append(t * lax.rsqrt(jnp.mean(t * t, axis=-1, keepdims=True) + EPS))
    return jnp.concatenate(parts, axis=1)


def _merge_kernel(xc_ref, xl_ref, mod_ref, g_ref, wg_ref, wb_ref, wo_ref,
                  lf_ref, lb_ref, lg_ref, rf_ref, rb_ref, rg_ref, gf_ref, gb_ref, gz_ref, sf_ref, sb_ref, sz_ref,
                  rn_ref, gn_ref, sn_ref, o_ref, *, n_ctx_tiles, tile0):
    is_ctx = pl.program_id(0) + tile0 < n_ctx_tiles
    d = xc_ref.shape[1]
    x = jnp.where(is_ctx, xc_ref[...], xl_ref[...])
    h = _rms_mod(x, g_ref[...], _pick_mod(mod_ref, is_ctx, 0, d), _pick_mod(mod_ref, is_ctx, 1, d)).astype(BF16)
    def lru_y():
        return (lf_ref[...] + lb_ref[...]) * _gelu_tanh(lg_ref[...])

    def ret_y():
        g_cols = slice(2 * RET_H * RET_DK + RET_H * RET_DV, RET_COLS)
        return _head_rms(rf_ref[...] + rb_ref[...], RET_H, RET_DV) * rn_ref[...] * _silu(rg_ref[:, g_cols])

    def gdn_y():
        return _head_rms(gf_ref[...] + gb_ref[...], GDN_H, GDN_DV) * gn_ref[...] * _silu(gz_ref[...])

    def ssd_y():
        ssd = (sf_ref[...] + sb_ref[...]) * _silu(sz_ref[...])
        return ssd * lax.rsqrt(jnp.mean(ssd * ssd, axis=-1, keepdims=True) + EPS) * sn_ref[...]

    merged = None
    for nb_, branch_y in enumerate((lru_y, ret_y, gdn_y, ssd_y)):
        pre = _dot(h, wg_ref[:, nb_ * d:(nb_ + 1) * d])
        t = _sigmoid(pre) * _dot(branch_y().astype(BF16), wb_ref[nb_])
        merged = t if merged is None else merged + t
    out = _dot(merged.astype(BF16), wo_ref[...])
    o_ref[...] = x + _pick_mod(mod_ref, is_ctx, 2, d) * out


def _merge_call(x_ctx, x_lat, lat_row0, mod, g, w_gate, w_branch, w_out, branches, norms, n_ctx, row_start):
    d = x_ctx.shape[1]
    tm = ROW_TILE
    lt = n_ctx + x_lat.shape[0] - lat_row0
    t0 = row_start // tm
    nt = lt // tm - t0
    row = lambda wd: pl.BlockSpec((tm, wd), lambda i: (i + t0, 0))
    in_specs = [*_row_sources(x_ctx, x_lat, lat_row0, n_ctx, tm, t0), _full(mod.shape), _full(g.shape),
                _full(w_gate.shape), _full(w_branch.shape), _full(w_out.shape)]
    in_specs += [row(b.shape[1]) for b in branches]
    in_specs += [_full(n.shape) for n in norms]
    return pl.pallas_call(
        functools.partial(_merge_kernel, n_ctx_tiles=n_ctx // tm, tile0=t0),
        grid=(nt,),
        in_specs=in_specs,
        out_specs=pl.BlockSpec((tm, d), lambda i: (i, 0)),
        out_shape=jax.ShapeDtypeStruct((nt * tm, d), F32),
        compiler_params=_cparams(("parallel",)),
        name="merge_out",
    )(x_ctx, x_lat, mod, g, w_gate, w_branch, w_out, *branches, *norms)


def _ffn_kernel(x_ref, mod_ref, g_ref, wg_ref, wu_ref, wd_ref, fn_ref, o_ref, *, n_ctx_tiles, final):
    is_ctx = pl.program_id(0) < n_ctx_tiles
    d = x_ref.shape[1]
    x = x_ref[...]
    h = _rms_mod(x, g_ref[...], _pick_mod(mod_ref, is_ctx, 3, d), _pick_mod(mod_ref, is_ctx, 4, d)).astype(BF16)
    act = (_silu(_dot(h, wg_ref[...])) * _dot(h, wu_ref[...])).astype(BF16)
    y = x + _pick_mod(mod_ref, is_ctx, 5, d) * _dot(act, wd_ref[...])
    if final:
        y = y * lax.rsqrt(jnp.mean(y * y, axis=-1, keepdims=True) + EPS) * fn_ref[...]
    o_ref[...] = y


def _ffn_call(x_rows, mod, g, wg, wu, wd, final_g, n_ctx_rows, final):
    n, d = x_rows.shape
    tm = ROW_TILE
    row = pl.BlockSpec((tm, d), lambda i: (i, 0))
    return pl.pallas_call(
        functools.partial(_ffn_kernel, n_ctx_tiles=n_ctx_rows // tm, final=final),
        grid=(n // tm,),
        in_specs=[row, _full(mod.shape), _full(g.shape), _full(wg.shape), _full(wu.shape), _full(wd.shape), _full(final_g.shape)],
        out_specs=row,
        out_shape=jax.ShapeDtypeStruct((n, d), F32),
        compiler_params=_cparams(("parallel",)),
        name="dense_swiglu",
    )(x_rows, mod, g, wg, wu, wd, final_g)


def _router_kernel(x_ref, mod_ref, g_ref, rt_ref, tri_ref, h_o, gate_o, slot_o, cnt_o):
    d = x_ref.shape[1]
    h = _rms_mod(x_ref[...], g_ref[...], mod_ref[0:1, 3 * d:4 * d], mod_ref[0:1, 4 * d:5 * d])
    h_o[...] = h.astype(BF16)
    logits = _dot_nt(rt_ref[...], h, HI)
    e, b = logits.shape
    eid = lax.broadcasted_iota(jnp.int32, (e, b), 0)
    m1 = jnp.max(logits, axis=0, keepdims=True)
    i1 = jnp.min(jnp.where(logits == m1, eid, e), axis=0, keepdims=True)
    rest = jnp.where(eid == i1, -jnp.inf, logits)
    m2 = jnp.max(rest, axis=0, keepdims=True)
    i2 = jnp.min(jnp.where(rest == m2, eid, e), axis=0, keepdims=True)
    t = jnp.exp(m2 - m1)
    p1 = 1.0 / (1.0 + t)
    p2 = t / (1.0 + t)
    sel1 = eid == i1
    sel2 = eid == i2
    gate_o[...] = jnp.where(sel1, p1, jnp.where(sel2, p2, 0.0))
    sel = jnp.logical_or(sel1, sel2)
    rank = _dot(sel.astype(BF16), tri_ref[...])
    slot_o[...] = jnp.where(sel, rank, -1.0).astype(jnp.int32)
    cnt = jnp.sum(sel.astype(F32), axis=1, keepdims=True)
    cnt_o[...] = jnp.broadcast_to(cnt, (e, LANES))[None].astype(jnp.int32)


def _router_call(xl, mod, g, router_t):
    n, d = xl.shape
    b = MOE_BLOCK
    nblk = n // b
    e = router_t.shape[0]
    pos = jnp.arange(b)
    tri = (pos[:, None] < pos[None, :]).astype(BF16)
    return pl.pallas_call(
        _router_kernel,
        grid=(nblk,),
        in_specs=[pl.BlockSpec((b, d), lambda i: (i, 0)), _full(mod.shape), _full(g.shape), _full(router_t.shape), _full(tri.shape)],
        out_specs=[pl.BlockSpec((b, d), lambda i: (i, 0)), pl.BlockSpec((e, b), lambda i: (0, i)),
                   pl.BlockSpec((e, b), lambda i: (0, i)), pl.BlockSpec((1, e, LANES), lambda i: (i, 0, 0))],
        out_shape=[jax.ShapeDtypeStruct((n, d), BF16), jax.ShapeDtypeStruct((e, n), F32),
                   jax.ShapeDtypeStruct((e, n), jnp.int32), jax.ShapeDtypeStruct((nblk, e, LANES), jnp.int32)],
        compiler_params=_cparams(("parallel",)),
        name="moe_router",
    )(xl, mod, g, router_t, tri)


def _moe_kernel(nfull_ref, tail_ref, x_ref, h_ref, gate_ref, slot_ref, mod_ref, fn_ref, wg_ref, wu_ref, wd_ref, o_ref,
                hs_ref, ys_ref, *, final):
    bi, ei, fi = pl.program_id(0), pl.program_id(1), pl.program_id(2)
    n_e, n_f = pl.num_programs(1), pl.num_programs(2)
    b, d = x_ref.shape
    nfull = nfull_ref[bi * n_e + ei]
    tail = tail_ref[bi * n_e + ei]
    tail_start = pl.multiple_of(nfull * MOE_SLOTS, MOE_SLOTS)
    tail_sizes = tuple(range(MOE_TAIL, MOE_SLOTS, MOE_TAIL))

    def for_tiles(fn):
        def body(j, carry):
            fn(pl.multiple_of(j * MOE_SLOTS, MOE_SLOTS), MOE_SLOTS)
            return carry
        lax.fori_loop(0, nfull, body, 0)
        for k, size in enumerate(tail_sizes):
            pl.when(tail == k + 1)(functools.partial(fn, tail_start, size))

    def onehot(start, size):
        return slot_ref[pl.ds(ei, 1), :] == lax.broadcasted_iota(jnp.int32, (size, b), 0) + start

    @pl.when(jnp.logical_and(ei == 0, fi == 0))
    def _():
        o_ref[...] = jnp.zeros_like(o_ref)

    def gather(start, size):
        hs_ref[pl.ds(start, size), :] = _dot(onehot(start, size).astype(BF16), h_ref[...]).astype(BF16)

    pl.when(fi == 0)(functools.partial(for_tiles, gather))

    def expert(start, size):
        hs = hs_ref[pl.ds(start, size), :]
        act = (_silu(_dot(hs, wg_ref[0])) * _dot(hs, wu_ref[0])).astype(BF16)
        y = _dot(act, wd_ref[0])

        @pl.when(fi == 0)
        def _():
            ys_ref[pl.ds(start, size), :] = y

        @pl.when(fi != 0)
        def _():
            ys_ref[pl.ds(start, size), :] = ys_ref[pl.ds(start, size), :] + y

    for_tiles(expert)

    def scatter(start, size):
        oh = onehot(start, size)
        gs = jnp.sum(jnp.where(oh, gate_ref[pl.ds(ei, 1), :], 0.0), axis=1, keepdims=True)
        o_ref[...] += _dot_tn(oh.astype(BF16), (ys_ref[pl.ds(start, size), :] * gs).astype(BF16))

    pl.when(fi == n_f - 1)(functools.partial(for_tiles, scatter))

    @pl.when(jnp.logical_and(ei == n_e - 1, fi == n_f - 1))
    def _():
        y = x_ref[...] + mod_ref[0:1, 5 * d:6 * d] * o_ref[...]
        if final:
            y = y * lax.rsqrt(jnp.mean(y * y, axis=-1, keepdims=True) + EPS) * fn_ref[...]
        o_ref[...] = y


def _moe_call(xl, h2, gate_t, slot_t, counts, mod, final_g, wg, wu, wd, final):
    n, d = xl.shape
    b = MOE_BLOCK
    e, _, f = wg.shape
    fs = f // MOE_FSPLIT
    counts = counts.reshape(-1)
    tail = (counts % MOE_SLOTS + (MOE_TAIL - 1)) // MOE_TAIL
    nfull = counts // MOE_SLOTS + tail // (MOE_SLOTS // MOE_TAIL)
    tail = tail % (MOE_SLOTS // MOE_TAIL)
    once = pl.Buffered(1)
    grid_spec = pltpu.PrefetchScalarGridSpec(
        num_scalar_prefetch=2,
        grid=(n // b, e, MOE_FSPLIT),
        in_specs=[pl.BlockSpec((b, d), lambda i, j, k, *_: (i, 0), pipeline_mode=once),
                  pl.BlockSpec((b, d), lambda i, j, k, *_: (i, 0), pipeline_mode=once),
                  pl.BlockSpec((e, b), lambda i, j, k, *_: (0, i)),
                  pl.BlockSpec((e, b), lambda i, j, k, *_: (0, i)),
                  pl.BlockSpec(mod.shape, lambda i, j, k, *_: (0, 0)),
                  pl.BlockSpec(final_g.shape, lambda i, j, k, *_: (0, 0)),
                  pl.BlockSpec((1, d, fs), lambda i, j, k, *_: (j, 0, k)),
                  pl.BlockSpec((1, d, fs), lambda i, j, k, *_: (j, 0, k)),
                  pl.BlockSpec((1, fs, d), lambda i, j, k, *_: (j, k, 0))],
        out_specs=pl.BlockSpec((b, d), lambda i, j, k, *_: (i, 0)),
        scratch_shapes=[pltpu.VMEM((b, d), BF16), pltpu.VMEM((b, d), F32)],
    )
    return pl.pallas_call(
        functools.partial(_moe_kernel, final=final),
        grid_spec=grid_spec,
        out_shape=jax.ShapeDtypeStruct((n, d), F32),
        compiler_params=_cparams(("parallel", "arbitrary", "arbitrary")),
        name="moe_experts",
    )(nfull, tail, xl, h2, gate_t, slot_t, mod, final_g, wg, wu, wd)


def _mix_weight(w_mix):
    d = w_mix.shape[0]
    off = np.concatenate([[0], np.cumsum(MIX_SPLITS)])
    seg = lambda k: w_mix[:, off[k]:off[k + 1]]

    def halves(t):
        t = t.reshape(d, RET_H, RET_DK // 2, 2)
        return jnp.concatenate([t[..., 0], t[..., 1]], axis=-1).reshape(d, RET_H * RET_DK)

    n_small = MIX_SPLITS[8] + MIX_SPLITS[9] + MIX_SPLITS[12]
    return jnp.concatenate([
        seg(0), seg(6), seg(11),
        seg(1),
        halves(seg(2)), halves(seg(3)), seg(4), seg(5),
        seg(7), seg(10),
        seg(8), seg(9), seg(12), jnp.zeros((d, SMALL_COLS - n_small), w_mix.dtype),
    ], axis=1).astype(BF16)


def _block_diag(w):
    n, i, o = w.shape
    eye = jnp.eye(n, dtype=w.dtype)
    return (eye[:, None, :, None] * w[:, :, None, :]).reshape(n * i, n * o)


def _rotary_tables(n_lat, n_ctx):
    assert GRID_W % SUBLANES == 0 and n_ctx % SUBLANES == 0 and ROW_TILE % GRID_W == 0
    n_freq = RET_DK // 4
    freqs = ROPE_BASE ** (-jnp.arange(n_freq, dtype=F32) / n_freq)
    lane = np.arange(RET_H * RET_DK)
    freq_lane = freqs[(lane % (RET_DK // 2)) % n_freq]
    sign = jnp.asarray(np.where(lane % RET_DK < RET_DK // 2, -1.0, 1.0).astype(np.float32))
    t8 = jnp.arange((n_ctx + n_lat) // SUBLANES) * SUBLANES - n_ctx
    row = jnp.where(t8 >= 0, t8 // GRID_W, 0).astype(F32)
    col = (jnp.arange(ROW_TILE) % GRID_W).astype(F32)

    def cs(pos):
        ang = pos[:, None] * freq_lane[None, :]
        return jnp.stack([jnp.cos(ang), sign[None, :] * jnp.sin(ang)])

    return cs(row), cs(col)


def kernel(x, c, ctx, c_ctx, w_mod, b_mod, norm_mix, norm_ffn, w_in, lru_conv_w, lru_conv_b, lru_wa, lru_ba, lru_wx, lru_bx, lru_lambda, ret_norm, gdn_conv_w, gdn_a_log, gdn_dt_bias, gdn_norm, ssd_conv_w, ssd_conv_b, ssd_a_log, ssd_dt_bias, ssd_d, ssd_norm, w_branch, w_out, ffn_wg, ffn_wu, ffn_wd, moe_router, moe_wg, moe_wu, moe_wd, final_norm):
    assert x.shape[0] == 1 and c.shape[0] == 1 and ctx.shape[0] == 1
    depth = w_mod.shape[0]
    n_lat, d = x.shape[1], x.shape[2]
    n_ctx = ctx.shape[1]
    assert n_ctx % ROW_TILE == 0 and n_lat % ROW_TILE == 0 and n_ctx % MIX_ROWS == 0 and n_lat % MIX_ROWS == 0
    gate_cols = N_BRANCH * d
    rot_rows, rot_cols = _rotary_tables(n_lat, n_ctx)
    b_mod3 = b_mod[:, None, :]
    ct = jnp.stack([c[0], c_ctx], axis=1)
    final_g = final_norm[None, :]

    x_ctx, x_lat, lat_row0 = ctx[0], x[0], 0
    for layer in range(depth):
        ctx_out = layer < depth - 1
        last = layer == depth - 1
        mod = _mod_call(ct, w_mod, b_mod3, layer)
        w_mix = _mix_weight(w_in[layer][:, gate_cols:])
        conv_w = jnp.concatenate([lru_conv_w[layer], gdn_conv_w[layer], ssd_conv_w[layer]], axis=1)
        conv_b = jnp.concatenate([lru_conv_b[layer], jnp.zeros((GDN_QKV,), F32), ssd_conv_b[layer]])[None, :]
        (lru_u, gdn_qkv, ssd_xbc, p_lg, p_ret, p_gz, p_sz, p_small) = _proj_call(
            x_ctx, x_lat, lat_row0, mod, norm_mix[layer][None, :], w_mix, rot_rows, rot_cols, conv_w, conv_b, n_ctx)
        small_t = p_small[:, 0:32].T

        lru_w = jnp.stack([jnp.concatenate([_block_diag(lru_wa[layer, dd]), _block_diag(lru_wx[layer, dd])], axis=1)
                           for dd in range(2)]).astype(BF16)
        lru_b = jnp.concatenate([lru_ba[layer], lru_bx[layer]], axis=1)[:, None, :]
        lru_f, lru_b_ = _lru_call(lru_u, lru_w, lru_b, lru_lambda[layer][:, None, :], n_ctx)
        ret_f, ret_b = _ret_call(p_ret, n_ctx)
        gdn_f, gdn_b = _gdn_call(gdn_qkv, p_small, small_t, gdn_a_log[layer], gdn_dt_bias[layer], n_ctx)
        ssd_f, ssd_b = _ssd_call(ssd_xbc, p_small, small_t, ssd_dt_bias[layer], ssd_a_log[layer], ssd_d[layer], n_ctx)

        branches = (lru_f, lru_b_, p_lg, ret_f, ret_b, p_ret, gdn_f, gdn_b, p_gz, ssd_f, ssd_b, p_sz)
        norms = (ret_norm[layer][None, :], jnp.tile(gdn_norm[layer], GDN_H)[None, :], ssd_norm[layer][None, :])
        row_start = 0 if ctx_out else n_ctx
        x_rows = _merge_call(x_ctx, x_lat, lat_row0, mod, norm_mix[layer][None, :],
                             w_in[layer][:, :gate_cols].astype(BF16), w_branch[layer].astype(BF16),
                             w_out[layer].astype(BF16), branches, norms, n_ctx, row_start)
        n_ctx_rows = n_ctx - row_start
        j = layer // 2
        if layer % 2 == 0:
            x_rows = _ffn_call(x_rows, mod, norm_ffn[layer][None, :], ffn_wg[j].astype(BF16), ffn_wu[j].astype(BF16),
                               ffn_wd[j].astype(BF16), final_g, n_ctx_rows, last)
        else:
            assert not ctx_out, "expert layers that must also emit context tokens are not supported"
            assert x_rows.shape[0] % MOE_BLOCK == 0 and MOE_BLOCK % MOE_SLOTS == 0 and MOE_SLOTS % MOE_TAIL == 0
            h2, gate_t, slot_t, cnt = _router_call(x_rows, mod, norm_ffn[layer][None, :], moe_router[j].T)
            x_rows = _moe_call(x_rows, h2, gate_t, slot_t, cnt[:, :, 0], mod, final_g, moe_wg[j].astype(BF16),
                               moe_wu[j].astype(BF16), moe_wd[j].astype(BF16), last)
        x_ctx, x_lat, lat_row0 = x_rows, x_rows, n_ctx - row_start
    out = x_rows[x_rows.shape[0] - n_lat:]
    return out[None]
```

```python
import functools
import math

import numpy as np
import jax
import jax.numpy as jnp
from jax import lax
from jax.experimental import pallas as pl
from jax.experimental.pallas import tpu as pltpu

F32 = jnp.float32
BF16 = jnp.bfloat16
HI = lax.Precision.HIGHEST

EPS = 1e-6
GRID_W = 64
N_BRANCH = 4
BRANCH_W = 512
CONV_W = 4
LRU_W = 512
LRU_BLOCKS = 8
LRU_C = 8.0
RET_H, RET_DK, RET_DV, RET_CHUNK = 4, 64, 128, 128
ROPE_BASE = 10000.0
GDN_H, GDN_DK, GDN_DV, GDN_CHUNK = 4, 128, 128, 64
SSD_H, SSD_P, SSD_G, SSD_N, SSD_CHUNK = 8, 64, 2, 64, 128
N_EXPERTS = 8
GDN_QKV = 2 * GDN_H * GDN_DK + GDN_H * GDN_DV
SSD_XBC = SSD_H * SSD_P + 2 * SSD_G * SSD_N
MIX_SPLITS = (LRU_W, LRU_W, RET_H * RET_DK, RET_H * RET_DK, RET_H * RET_DV, RET_H * RET_DV,
              GDN_QKV, GDN_H * GDN_DV, 2 * GDN_H, 2 * GDN_H, SSD_H * SSD_P, SSD_XBC, 2 * SSD_H)

LANES = 128
SUBLANES = 8
VMEM_LIMIT = 56 * 1024 * 1024

ROW_TILE = 256
MIX_ROWS = 256
MOE_BLOCK = 1024
MOE_SLOTS = 512
GDN_INV_BLOCK = 16
MOE_TAIL = 64
MOE_FSPLIT = 2

CONV_COLS = LRU_W + GDN_QKV + SSD_XBC
RET_COLS = 2 * RET_H * RET_DK + 2 * RET_H * RET_DV
SMALL_COLS = LANES


def _cparams(sem):
    return pltpu.CompilerParams(dimension_semantics=sem, vmem_limit_bytes=VMEM_LIMIT)


def _dot(a, b, precision=None):
    return jnp.dot(a, b, preferred_element_type=F32, precision=precision)


def _dot_nt(a, b, precision=None):
    return lax.dot_general(a, b, (((1,), (1,)), ((), ())), preferred_element_type=F32, precision=precision)


def _dot_tn(a, b, precision=None):
    return lax.dot_general(a, b, (((0,), (0,)), ((), ())), preferred_element_type=F32, precision=precision)


def _sigmoid(x):
    return 0.5 * jnp.tanh(0.5 * x) + 0.5


def _silu(x):
    return x * _sigmoid(x)


def _softplus(x):
    return jnp.maximum(x, 0.0) + jnp.log1p(jnp.exp(-jnp.abs(x)))


def _gelu_tanh(x):
    return 0.5 * x * (1.0 + jnp.tanh(math.sqrt(2.0 / math.pi) * (x + 0.044715 * (x * x * x))))


def _rms_mod(x, g, shift, scale):
    ms = jnp.mean(x * x, axis=-1, keepdims=True)
    return (x * lax.rsqrt(ms + EPS) * g) * (1.0 + scale) + shift


def _pick_mod(mod_ref, is_ctx, k, d):
    return jnp.where(is_ctx, mod_ref[1:2, k * d:(k + 1) * d], mod_ref[0:1, k * d:(k + 1) * d])


def _full(shape):
    n = len(shape)
    return pl.BlockSpec(shape, lambda *_: (0,) * n)


def _mod_kernel(ct_ref, w_ref, b_ref, o_ref):
    s = _silu(ct_ref[...])
    w = w_ref[0]
    b = b_ref[0]
    o_ref[0:1, :] = jnp.sum(s[:, 0:1] * w, axis=0, keepdims=True) + b
    o_ref[1:2, :] = jnp.sum(s[:, 1:2] * w, axis=0, keepdims=True) + b


def _mod_call(ct, w, b, layer):
    _, d, n = w.shape
    tn = 512
    return pl.pallas_call(
        _mod_kernel,
        grid=(n // tn,),
        in_specs=[_full((d, 2)), pl.BlockSpec((1, d, tn), lambda j: (layer, 0, j)),
                  pl.BlockSpec((1, 1, tn), lambda j: (layer, 0, j))],
        out_specs=pl.BlockSpec((2, tn), lambda j: (0, j)),
        out_shape=jax.ShapeDtypeStruct((2, n), F32),
        compiler_params=_cparams(("arbitrary",)),
        name="adaln_mod",
    )(ct, w, b)


def _proj_kernel(xc_ref, xl_ref, xcp_ref, xlp_ref, xcn_ref, xln_ref, mod_ref, g_ref, w_ref, rrow_ref, rcol_ref, cw_ref, cb_ref,
                 lru_o, gdn_o, ssd_o, lg_o, ret_o, gz_o, sz_o, sm_o, *, n_ctx_tiles):
    i = pl.program_id(0)
    is_ctx = i < n_ctx_tiles
    d = xc_ref.shape[1]
    tm = xc_ref.shape[0]
    shift = _pick_mod(mod_ref, is_ctx, 0, d)
    scale = _pick_mod(mod_ref, is_ctx, 1, d)
    hf = _rms_mod(jnp.where(is_ctx, xc_ref[...], xl_ref[...]), g_ref[...], shift, scale)
    h = hf.astype(BF16)

    def mm(a, b):
        return _dot(h, w_ref[:, a:b])

    has_prev = jnp.logical_and(i != 0, i != n_ctx_tiles).astype(F32)
    has_next = jnp.logical_and(i != n_ctx_tiles - 1, i != pl.num_programs(0) - 1).astype(F32)
    h_ext = jnp.concatenate([_rms_mod(jnp.where(is_ctx, xcp_ref[...], xlp_ref[...]), g_ref[...], shift, scale), hf,
                             _rms_mod(jnp.where(is_ctx, xcn_ref[...], xln_ref[...]), g_ref[...], shift, scale)],
                            axis=0).astype(BF16)
    sub = lax.broadcasted_iota(jnp.int32, (1, SUBLANES, 1), 1)
    g = tm // SUBLANES

    def conv(c0, c1):
        wd = c1 - c0
        pe = _dot(h_ext, w_ref[:, c0:c1]).reshape(g + 2, SUBLANES, wd)
        u = pe[1:g + 1]
        ext = jnp.concatenate([pe[0:1] * has_prev, u, pe[g + 1:g + 2] * has_next], axis=0)
        r1 = pltpu.roll(ext, 1, axis=1)
        r7 = pltpu.roll(ext, SUBLANES - 1, axis=1)
        r6 = pltpu.roll(ext, SUBLANES - 2, axis=1)
        um1 = jnp.where(sub >= 1, r1[1:g + 1], r1[0:g])
        up1 = jnp.where(sub < SUBLANES - 1, r7[1:g + 1], r7[2:g + 2])
        up2 = jnp.where(sub < SUBLANES - 2, r6[1:g + 1], r6[2:g + 2])
        w = cw_ref[:, c0:c1]
        y = w[0:1] * um1 + w[1:2] * u + w[2:3] * up1 + w[3:4] * up2 + cb_ref[:, c0:c1]
        return y.reshape(tm, wd)

    step = 2 * LANES
    conv_jobs, plain_jobs = [], []

    def lru_job(c0):
        lru_o[:, c0:c0 + step] = conv(c0, c0 + step)

    def gdn_qk_job(c0):
        t2 = _silu(conv(LRU_W + c0, LRU_W + c0 + step))
        for k in range(step // GDN_DK):
            t = t2[:, k * GDN_DK:(k + 1) * GDN_DK]
            t = t * lax.rsqrt(jnp.sum(t * t, axis=-1, keepdims=True) + EPS)
            if c0 < GDN_H * GDN_DK:
                t = t * (GDN_DK ** -0.5)
            gdn_o[:, c0 + k * GDN_DK:c0 + (k + 1) * GDN_DK] = t

    def gdn_v_job(c0):
        gdn_o[:, c0:c0 + step] = _silu(conv(LRU_W + c0, LRU_W + c0 + step))

    def ssd_job(c0):
        ssd_o[:, c0:c0 + step] = _silu(conv(LRU_W + GDN_QKV + c0, LRU_W + GDN_QKV + c0 + step))

    conv_jobs += [functools.partial(lru_job, c0) for c0 in range(0, LRU_W, step)]
    conv_jobs += [functools.partial(gdn_qk_job, c0) for c0 in range(0, 2 * GDN_H * GDN_DK, step)]
    conv_jobs += [functools.partial(gdn_v_job, c0) for c0 in range(2 * GDN_H * GDN_DK, GDN_QKV, step)]
    conv_jobs += [functools.partial(ssd_job, c0) for c0 in range(0, SSD_XBC, step)]

    qk_w = 2 * RET_H * RET_DK
    half = RET_DK // 2
    lane = lax.broadcasted_iota(jnp.int32, (tm, RET_H * RET_DK), 1)
    first = (lane % RET_DK) < half

    by_row = (lane % half) < half // 2
    rows8 = (tm // SUBLANES, SUBLANES, RET_H * RET_DK)

    def table(k, rest):
        r = jnp.broadcast_to(rrow_ref[k][:, None, :], rows8).reshape(tm, RET_H * RET_DK)
        return jnp.where(by_row, r, jnp.where(is_ctx, rest, rcol_ref[k]))

    tables = []

    def rot(t):
        if not tables:
            tables.extend([table(0, 1.0), table(1, 0.0)])
        partner = jnp.where(first, pltpu.roll(t, RET_H * RET_DK - half, axis=1), pltpu.roll(t, half, axis=1))
        return t * tables[0] + partner * tables[1]

    c_lg = CONV_COLS
    c_ret = c_lg + LRU_W
    c_gz = c_ret + RET_COLS
    c_sz = c_gz + GDN_H * GDN_DV
    c_sm = c_sz + SSD_H * SSD_P

    def plain(o_ref, o0, c0, width):
        def job():
            o_ref[:, o0:o0 + width] = mm(c0, c0 + width)
        return job

    def ret_q_job():
        ret_o[:, 0:qk_w // 2] = rot(mm(c_ret, c_ret + qk_w // 2)) * (RET_DK ** -0.5)

    def ret_k_job():
        ret_o[:, qk_w // 2:qk_w] = rot(mm(c_ret + qk_w // 2, c_ret + qk_w))

    plain_jobs += [plain(lg_o, 0, c_lg, LRU_W), ret_q_job, ret_k_job,
                   plain(ret_o, qk_w, c_ret + qk_w, RET_H * RET_DV),
                   plain(ret_o, qk_w + RET_H * RET_DV, c_ret + qk_w + RET_H * RET_DV, RET_H * RET_DV),
                   plain(gz_o, 0, c_gz, GDN_H * GDN_DV), plain(sz_o, 0, c_sz, SSD_H * SSD_P),
                   plain(sm_o, 0, c_sm, SMALL_COLS)]
    for k in range(max(len(conv_jobs), len(plain_jobs))):
        if k < len(conv_jobs):
            conv_jobs[k]()
        if k < len(plain_jobs):
            plain_jobs[k]()


def _row_sources(x_ctx, x_lat, lat_row0, n_ctx, tm, tile0=0):
    d = x_ctx.shape[1]
    nct, ob = n_ctx // tm, lat_row0 // tm
    return (pl.BlockSpec((tm, d), lambda i: (jnp.minimum(i + tile0, nct - 1), 0)),
            pl.BlockSpec((tm, d), lambda i: (jnp.maximum(i + tile0 - nct, 0) + ob, 0)))


def _proj_call(x_ctx, x_lat, lat_row0, mod, g, w, rot_rows, rot_cols, conv_w, conv_b, n_ctx):
    d = x_ctx.shape[1]
    tm = ROW_TILE
    lt = n_ctx + x_lat.shape[0] - lat_row0
    nt, nct, ob = lt // tm, n_ctx // tm, lat_row0 // tm
    hb = tm // SUBLANES
    widths = (LRU_W, GDN_QKV, SSD_XBC, LRU_W, RET_COLS, GDN_H * GDN_DV, SSD_H * SSD_P, SMALL_COLS)
    row = lambda wd: pl.BlockSpec((tm, wd), lambda i: (i, 0))
    halo = lambda fn: pl.BlockSpec((SUBLANES, d), lambda i: (fn(i), 0))
    return pl.pallas_call(
        functools.partial(_proj_kernel, n_ctx_tiles=nct),
        grid=(nt,),
        in_specs=[*_row_sources(x_ctx, x_lat, lat_row0, n_ctx, tm),
                  halo(lambda i: jnp.clip(i * hb - 1, 0, nct * hb - 1)),
                  halo(lambda i: jnp.maximum((i - nct) * hb - 1, 0) + ob * hb),
                  halo(lambda i: jnp.minimum((i + 1) * hb, nct * hb - 1)),
                  halo(lambda i: jnp.clip((i + 1 - nct) * hb, 0, (nt - nct) * hb - 1) + ob * hb),
                  _full(mod.shape), _full(g.shape), _full(w.shape),
                  pl.BlockSpec((2, tm // SUBLANES, rot_rows.shape[2]), lambda i: (0, i, 0)), _full(rot_cols.shape),
                  _full(conv_w.shape), _full(conv_b.shape)],
        out_specs=[row(wd) for wd in widths],
        out_shape=[jax.ShapeDtypeStruct((lt, wd), F32) for wd in widths],
        compiler_params=_cparams(("parallel",)),
        name="mix_proj",
    )(x_ctx, x_lat, x_ctx, x_lat, x_ctx, x_lat, mod, g, w, rot_rows, rot_cols, conv_w, conv_b)


def _bwd_block(i, n_ctx_blocks, n_blocks):
    return jnp.where(i < n_ctx_blocks, n_ctx_blocks - 1 - i, n_blocks + n_ctx_blocks - 1 - i)


def _dir_specs(r, width, ncb, nb):
    return (pl.BlockSpec((r, width), lambda i: (i, 0)),
            pl.BlockSpec((r, width), lambda i: (_bwd_block(i, ncb, nb), 0)))


def _dir_specs_t(rows, r, ncb, nb):
    return (pl.BlockSpec((rows, r), lambda i: (0, i)),
            pl.BlockSpec((rows, r), lambda i: (0, _bwd_block(i, ncb, nb))))


def _chunk_order(d, n):
    return range(n) if d == 0 else range(n - 1, -1, -1)


def _lru_kernel(uf_ref, ub_ref, w_ref, b_ref, lam_ref, yf_o, yb_o, carry_ref):
    @pl.when(pl.program_id(0) == 0)
    def _():
        carry_ref[...] = jnp.zeros_like(carry_ref)

    r = uf_ref.shape[0]
    sub = lax.broadcasted_iota(jnp.int32, (1, SUBLANES, 1), 1)

    def run(d, u_ref, o_ref):
        u = u_ref[...]
        gates = _sigmoid(_dot(u.astype(BF16), w_ref[d]) + b_ref[d])
        rg = gates[:, 0:LRU_W]
        ig = gates[:, LRU_W:2 * LRU_W]
        log_a = (-LRU_C) * rg * _softplus(-lam_ref[d])
        a = jnp.exp(log_a)
        b = jnp.sqrt(1.0 - a * a) * (ig * u)
        a = a.reshape(r // SUBLANES, SUBLANES, LRU_W)
        b = b.reshape(r // SUBLANES, SUBLANES, LRU_W)
        sh = 1
        while sh < SUBLANES:
            valid = sub >= sh if d == 0 else sub < SUBLANES - sh
            shift = sh if d == 0 else SUBLANES - sh
            a_s = pltpu.roll(a, shift, axis=1)
            b_s = pltpu.roll(b, shift, axis=1)
            b = jnp.where(valid, a * b_s + b, b)
            a = jnp.where(valid, a * a_s, a)
            sh *= 2
        carry = carry_ref[d, 0:1, :]
        for g in _chunk_order(d, r // SUBLANES):
            gs = slice(g * SUBLANES, (g + 1) * SUBLANES)
            h = a[g] * carry + b[g]
            o_ref[gs, :] = h
            carry = h[SUBLANES - 1:SUBLANES, :] if d == 0 else h[0:1, :]
        carry_ref[d, 0:1, :] = carry

    run(0, uf_ref, yf_o)
    run(1, ub_ref, yb_o)


def _lru_call(u, w, b, lam, n_ctx):
    lt = u.shape[0]
    r = MIX_ROWS
    nb, ncb = lt // r, n_ctx // r
    fs, bs = _dir_specs(r, LRU_W, ncb, nb)
    return pl.pallas_call(
        _lru_kernel,
        grid=(nb,),
        in_specs=[fs, bs, _full(w.shape), _full(b.shape), _full(lam.shape)],
        out_specs=[fs, bs],
        out_shape=[jax.ShapeDtypeStruct((lt, LRU_W), F32)] * 2,
        scratch_shapes=[pltpu.VMEM((2, 8, LRU_W), F32)],
        compiler_params=_cparams(("arbitrary",)),
        name="rglru_scan",
    )(u, u, w, b, lam)


def _ret_kernel(xf_ref, xb_ref, dmat_ref, qd_ref, kd_ref, sdec_ref, bd_ref, yf_o, yb_o, s_ref):
    @pl.when(pl.program_id(0) == 0)
    def _():
        s_ref[...] = jnp.zeros_like(s_ref)

    c = RET_CHUNK
    qw = RET_H * RET_DK
    lane_head = lax.broadcasted_iota(jnp.int32, (c, qw), 1) // RET_DK

    nch = xf_ref.shape[0] // c
    refs = ((xf_ref, yf_o), (xb_ref, yb_o))
    items = [(d, ck) for ck in range(nch) for d in range(2)]

    def rows(ck):
        return slice(ck * c, (ck + 1) * c)

    y_intra, upd, qdec = {}, {}, {}
    for it in items:
        d, ck = it
        x_ref = refs[d][0]
        rs = rows(ck)
        q = x_ref[rs, 0:qw]
        k = x_ref[rs, qw:2 * qw]
        kb = k.astype(BF16)
        vb = x_ref[rs, 2 * qw:2 * qw + RET_H * RET_DV].astype(BF16)
        upd[it] = bd_ref[...] * _dot_tn((k * kd_ref[d]).astype(BF16), vb)
        qdec[it] = (q * qd_ref[d]).astype(BF16)
        parts = []
        for hd in range(RET_H):
            qh = jnp.where(lane_head == hd, q, 0.0).astype(BF16)
            sc = _dot_nt(qh, kb) * dmat_ref[d, hd]
            parts.append(_dot(sc.astype(BF16), vb[:, hd * RET_DV:(hd + 1) * RET_DV]))
        y_intra[it] = parts

    state = [s_ref[0], s_ref[1]]
    for pos in range(nch):
        for d in range(2):
            ck = pos if d == 0 else nch - 1 - pos
            y_inter = _dot(qdec[d, ck], state[d].astype(BF16))
            for hd in range(RET_H):
                vs = slice(hd * RET_DV, (hd + 1) * RET_DV)
                refs[d][1][rows(ck), vs] = y_intra[d, ck][hd] + y_inter[:, vs]
            state[d] = sdec_ref[...] * state[d] + upd[d, ck]
    s_ref[0] = state[0]
    s_ref[1] = state[1]


def _ret_tables():
    c = RET_CHUNK
    lg = np.log(1.0 - 2.0 ** (-5.0 - np.arange(RET_H)))
    pos = np.arange(c, dtype=np.float64)
    dist = pos[:, None] - pos[None, :]
    d_f = np.where(dist >= 0, np.exp(np.maximum(dist, 0.0)[None] * lg[:, None, None]), 0.0)
    d_b = np.where(dist < 0, np.exp(np.maximum(-dist, 0.0)[None] * lg[:, None, None]), 0.0)
    dmat = np.stack([d_f, d_b])
    rep = lambda t: np.repeat(t, RET_DK, axis=1)
    qd = np.stack([rep(np.exp((pos + 1.0)[:, None] * lg)), rep(np.exp((c - pos)[:, None] * lg))])
    kd = np.stack([rep(np.exp((c - 1.0 - pos)[:, None] * lg)), rep(np.exp(pos[:, None] * lg))])
    hk = np.repeat(np.arange(RET_H), RET_DK)
    hv = np.repeat(np.arange(RET_H), RET_DV)
    bd = hk[:, None] == hv[None, :]
    sdec = np.broadcast_to(np.repeat(np.exp(c * lg), RET_DK)[:, None], bd.shape)
    return tuple(jnp.asarray(t, dtype=F32) for t in (dmat, qd, kd, sdec, bd))


def _ret_call(x, n_ctx):
    lt = x.shape[0]
    r = MIX_ROWS
    nb, ncb = lt // r, n_ctx // r
    tabs = _ret_tables()
    fs, bs = _dir_specs(r, 2 * RET_H * RET_DK + RET_H * RET_DV, ncb, nb)
    os_f, os_b = _dir_specs(r, RET_H * RET_DV, ncb, nb)
    return pl.pallas_call(
        _ret_kernel,
        grid=(nb,),
        in_specs=[fs, bs] + [_full(t.shape) for t in tabs],
        out_specs=[os_f, os_b],
        out_shape=[jax.ShapeDtypeStruct((lt, RET_H * RET_DV), F32)] * 2,
        scratch_shapes=[pltpu.VMEM((2, RET_H * RET_DK, RET_H * RET_DV), F32)],
        compiler_params=_cparams(("arbitrary",)),
        name="retention_scan",
    )(x, x, *tabs)


def _ssd_kernel(xf_ref, xb_ref, smf_ref, smb_ref, stf_ref, stb_ref, tri_ref, trit_ref,
                dtb_c_ref, dtb_r_ref, alog_c_ref, alog_r_ref, dskip_ref, yf_o, yb_o, s_ref):
    @pl.when(pl.program_id(0) == 0)
    def _():
        s_ref[...] = jnp.zeros_like(s_ref)

    c = SSD_CHUNK
    xw = SSD_H * SSD_P
    gw = SSD_G * SSD_N
    lo = lax.broadcasted_iota(jnp.int32, (c, LANES), 1) < SSD_N
    row_lo = lax.broadcasted_iota(jnp.int32, (LANES, 1), 0) < SSD_N
    lane_lo = lax.broadcasted_iota(jnp.int32, (1, LANES), 1) < SSD_N
    bd = row_lo == lane_lo
    dt0 = 2 * 2 * GDN_H

    nch = xf_ref.shape[0] // c
    n_pair = SSD_H // 2
    refs = ((xf_ref, smf_ref, stf_ref, yf_o), (xb_ref, smb_ref, stb_ref, yb_o))
    mask = [tri_ref[d] > 0.0 for d in range(2)]

    def rows(ck):
        return slice(ck * c, (ck + 1) * c)

    sc = {}
    for d in range(2):
        _, sm_ref, st_ref, _ = refs[d]
        a_c = -jnp.exp(alog_c_ref[d])
        a_r = -jnp.exp(alog_r_ref[d])
        for ck in range(nch):
            rs = rows(ck)
            dtc = _softplus(sm_ref[rs, dt0 + d * SSD_H:dt0 + (d + 1) * SSD_H] + dtb_c_ref[d])
            dtr = _softplus(st_ref[dt0 + d * SSD_H:dt0 + (d + 1) * SSD_H, rs] + dtb_r_ref[d])
            cs_c = _dot(tri_ref[d], dtc * a_c, HI)
            cs_r = _dot(dtr * a_r, trit_ref[d], HI)
            tot = cs_c[c - 1:c, :] if d == 0 else cs_c[0:1, :]
            sc[d, ck] = dict(dtr=dtr, cs_c=cs_c, cs_r=cs_r, e_c=jnp.exp(cs_c), dec_c=jnp.exp(tot - cs_c) * dtc,
                             e_tot=jnp.exp(tot))

    cb, c_dup, b_dup = {}, {}, {}
    for d in range(2):
        x_ref = refs[d][0]
        for ck in range(nch):
            rs = rows(ck)
            bm = x_ref[rs, xw:xw + gw]
            cm = x_ref[rs, xw + gw:xw + 2 * gw]
            bmb = bm.astype(BF16)
            b_roll = pltpu.roll(bm, SSD_N, axis=1)
            c_roll = pltpu.roll(cm, SSD_N, axis=1)
            for g in range(SSD_G):
                keep = lo if g == 0 else jnp.logical_not(lo)
                cb[d, ck, g] = _dot_nt(jnp.where(keep, cm, 0.0).astype(BF16), bmb)
                c_dup[d, ck, g] = jnp.where(keep, cm, c_roll)
                b_dup[d, ck, g] = jnp.where(keep, bm, b_roll)

    items = [(d, ck, m) for ck in range(nch) for d in range(2) for m in range(n_pair)]
    y_intra, upd, cq = {}, {}, {}
    for it in items:
        d, ck, m = it
        s_ = sc[d, ck]
        g = m // (n_pair // SSD_G)
        h0, h1 = 2 * m, 2 * m + 1
        scores = []
        for hd in (h0, h1):
            seg = s_["cs_c"][:, hd:hd + 1] - s_["cs_r"][hd:hd + 1, :]
            lmat = jnp.where(mask[d], jnp.exp(jnp.where(mask[d], seg, 0.0)), 0.0)
            scores.append(cb[d, ck, g] * lmat * s_["dtr"][hd:hd + 1, :])
        scb = jnp.concatenate(scores, axis=1).astype(BF16)
        ls = slice(m * LANES, (m + 1) * LANES)
        xp = refs[d][0][rows(ck), ls]
        xs = jnp.concatenate([jnp.where(lo, xp, 0.0), jnp.where(lo, 0.0, xp)], axis=0).astype(BF16)
        y = _dot(scb, xs)
        if d == 0:
            y = y + dskip_ref[:, ls] * xp
        y_intra[it] = y
        dec_pair = jnp.where(lo, s_["dec_c"][:, h0:h0 + 1], s_["dec_c"][:, h1:h1 + 1])
        upd[it] = jnp.where(bd, _dot_tn((b_dup[d, ck, g] * dec_pair).astype(BF16), xp.astype(BF16)), 0.0)
        e_pair = jnp.where(lo, s_["e_c"][:, h0:h0 + 1], s_["e_c"][:, h1:h1 + 1])
        cq[it] = (c_dup[d, ck, g] * e_pair).astype(BF16)

    state = {(d, m): s_ref[d, m] for d in range(2) for m in range(n_pair)}
    for pos in range(nch):
        for d in range(2):
            ck = pos if d == 0 else nch - 1 - pos
            e_tot = sc[d, ck]["e_tot"]
            for m in range(n_pair):
                it = (d, ck, m)
                st = state[d, m]
                refs[d][3][rows(ck), m * LANES:(m + 1) * LANES] = y_intra[it] + _dot(cq[it], st.astype(BF16))
                sdec = jnp.where(row_lo, e_tot[:, 2 * m:2 * m + 1], e_tot[:, 2 * m + 1:2 * m + 2])
                state[d, m] = sdec * st + upd[it]
    for (d, m), st in state.items():
        s_ref[d, m] = st


def _tri_tables(c):
    pos = np.arange(c)
    lower = (pos[:, None] >= pos[None, :]).astype(np.float32)
    tri = np.stack([lower, lower.T])
    trit = np.stack([lower.T, lower])
    return jnp.asarray(tri), jnp.asarray(trit)


def _ssd_call(xbc, small, small_t, dt_bias, a_log, d_skip, n_ctx):
    lt = xbc.shape[0]
    r = MIX_ROWS
    nb, ncb = lt // r, n_ctx // r
    tri, trit = _tri_tables(SSD_CHUNK)
    params = (dt_bias[:, None, :], dt_bias[:, :, None], a_log[:, None, :], a_log[:, :, None],
              jnp.repeat(d_skip, SSD_P)[None, :])
    xs = _dir_specs(r, SSD_XBC, ncb, nb)
    ss = _dir_specs(r, SMALL_COLS, ncb, nb)
    ts = _dir_specs_t(small_t.shape[0], r, ncb, nb)
    os_ = _dir_specs(r, SSD_H * SSD_P, ncb, nb)
    return pl.pallas_call(
        _ssd_kernel,
        grid=(nb,),
        in_specs=[*xs, *ss, *ts, _full(tri.shape), _full(trit.shape)] + [_full(p.shape) for p in params],
        out_specs=list(os_),
        out_shape=[jax.ShapeDtypeStruct((lt, SSD_H * SSD_P), F32)] * 2,
        scratch_shapes=[pltpu.VMEM((2, SSD_H // 2, 2 * SSD_N, 2 * SSD_P), F32)],
        compiler_params=_cparams(("arbitrary",)),
        name="ssd_scan",
    )(xbc, xbc, small, small, small_t, small_t, tri, trit, *params)


def _gdn_kernel(xf_ref, xb_ref, smf_ref, smb_ref, stf_ref, stb_ref, tri_ref, trit_ref,
                dtb_c_ref, dtb_r_ref, alog_c_ref, alog_r_ref, yf_o, yb_o, s_ref):
    @pl.when(pl.program_id(0) == 0)
    def _():
        s_ref[...] = jnp.zeros_like(s_ref)

    c = GDN_CHUNK
    kw = GDN_H * GDN_DK
    ri = lax.broadcasted_iota(jnp.int32, (c, c), 0)
    ci_ = lax.broadcasted_iota(jnp.int32, (c, c), 1)
    eye = (ri == ci_).astype(F32)

    nch = xf_ref.shape[0] // c
    refs = ((xf_ref, smf_ref, stf_ref, yf_o), (xb_ref, smb_ref, stb_ref, yb_o))
    incl = [tri_ref[d] > 0.0 for d in range(2)]
    strict = [jnp.logical_and(incl[d], ri != ci_) for d in range(2)]

    def rows(ck):
        return slice(ck * c, (ck + 1) * c)

    def q_of(d, ck, hd):
        return refs[d][0][rows(ck), hd * GDN_DK:(hd + 1) * GDN_DK]

    def k_of(d, ck, hd):
        return refs[d][0][rows(ck), kw + hd * GDN_DK:kw + (hd + 1) * GDN_DK]

    def v_of(d, ck, hd):
        return refs[d][0][rows(ck), 2 * kw + hd * GDN_DV:2 * kw + (hd + 1) * GDN_DV]

    sc = {}
    for d in range(2):
        _, sm_ref, st_ref, _ = refs[d]
        for ck in range(nch):
            rs = rows(ck)
            a_c = sm_ref[rs, d * GDN_H:(d + 1) * GDN_H]
            b_c = sm_ref[rs, 2 * GDN_H + d * GDN_H:2 * GDN_H + (d + 1) * GDN_H]
            a_r = st_ref[d * GDN_H:(d + 1) * GDN_H, rs]
            g_c = -jnp.exp(alog_c_ref[d]) * _softplus(a_c + dtb_c_ref[d])
            g_r = -jnp.exp(alog_r_ref[d]) * _softplus(a_r + dtb_r_ref[d])
            gcs_c = _dot(tri_ref[d], g_c, HI)
            gcs_r = _dot(g_r, trit_ref[d], HI)
            g_last = gcs_c[c - 1:c, :] if d == 0 else gcs_c[0:1, :]
            sc[d, ck] = dict(beta=_sigmoid(b_c), gcs_c=gcs_c, gcs_r=gcs_r, e_c=jnp.exp(gcs_c),
                             kdec=jnp.exp(g_last - gcs_c), e_last=jnp.exp(g_last))

    items = [(d, ck, hd) for ck in range(nch) for d in range(2) for hd in range(GDN_H)]

    lm, attn = {}, {}
    for it in items:
        d, ck, hd = it
        s_ = sc[d, ck]
        seg = s_["gcs_c"][:, hd:hd + 1] - s_["gcs_r"][hd:hd + 1, :]
        dmat = jnp.where(incl[d], jnp.exp(jnp.where(incl[d], seg, 0.0)), 0.0)
        kh = k_of(*it)
        both = _dot_nt(jnp.concatenate([kh * s_["beta"][:, hd:hd + 1], q_of(*it)], axis=0).astype(BF16), kh.astype(BF16))
        lm[it] = jnp.where(strict[d], both[0:c] * dmat, 0.0)
        attn[it] = (both[c:2 * c] * dmat).astype(BF16)

    blk = GDN_INV_BLOCK
    same = [(ri // (blk << k)) == (ci_ // (blk << k)) for k in range(int(math.log2(c // blk)) + 1)]
    diag = {it: jnp.where(same[0], lm[it], 0.0) for it in items}
    inv = {it: eye - diag[it] for it in items}
    pw = {it: diag[it].astype(BF16) for it in items}
    pw = {it: _dot(pw[it], pw[it]).astype(BF16) for it in items}
    n_sq = int(math.log2(blk)) - 1
    for step in range(n_sq):
        if step < n_sq - 1:
            both = {it: _dot(jnp.concatenate([inv[it].astype(BF16), pw[it]], axis=0), pw[it]) for it in items}
            inv = {it: inv[it] + both[it][0:c] for it in items}
            pw = {it: both[it][c:2 * c].astype(BF16) for it in items}
        else:
            inv = {it: inv[it] + _dot(inv[it].astype(BF16), pw[it]) for it in items}
    for k in range(1, len(same)):
        off = jnp.logical_and(same[k], jnp.logical_not(same[k - 1]))
        invb = {it: inv[it].astype(BF16) for it in items}
        tmp = {it: _dot(invb[it], jnp.where(off, lm[it], 0.0).astype(BF16)).astype(BF16) for it in items}
        inv = {it: inv[it] - _dot(tmp[it], invb[it]) for it in items}

    u, w = {}, {}
    for it in items:
        d, ck, hd = it
        s_ = sc[d, ck]
        bc = s_["beta"][:, hd:hd + 1]
        rhs = jnp.concatenate([v_of(*it) * bc, k_of(*it) * (bc * s_["e_c"][:, hd:hd + 1])], axis=1)
        sol = _dot(inv[it].astype(BF16), rhs.astype(BF16))
        u[it] = sol[:, 0:GDN_DV]
        w[it] = sol[:, GDN_DV:GDN_DV + GDN_DK].astype(BF16)

    state = {(d, hd): s_ref[d, hd] for d in range(2) for hd in range(GDN_H)}
    for pos in range(nch):
        cur = [(d, pos if d == 0 else nch - 1 - pos, hd) for d in range(2) for hd in range(GDN_H)]
        sb = {it: state[it[0], it[2]].astype(BF16) for it in cur}
        wq = {it: _dot(jnp.concatenate(
            [w[it], (q_of(*it) * sc[it[0], it[1]]["e_c"][:, it[2]:it[2] + 1]).astype(BF16)], axis=0), sb[it]) for it in cur}
        for it in cur:
            d, ck, hd = it
            s_ = sc[d, ck]
            vpb = (u[it] - wq[it][0:c]).astype(BF16)
            refs[d][3][rows(ck), hd * GDN_DV:(hd + 1) * GDN_DV] = wq[it][c:2 * c] + _dot(attn[it], vpb)
            state[d, hd] = (s_["e_last"][:, hd:hd + 1] * state[d, hd]
                            + _dot_tn((k_of(*it) * s_["kdec"][:, hd:hd + 1]).astype(BF16), vpb))
    for (d, hd), s in state.items():
        s_ref[d, hd] = s


def _gdn_call(qkv, small, small_t, a_log, dt_bias, n_ctx):
    lt = qkv.shape[0]
    r = MIX_ROWS
    nb, ncb = lt // r, n_ctx // r
    tri, trit = _tri_tables(GDN_CHUNK)
    params = (dt_bias[:, None, :], dt_bias[:, :, None], a_log[:, None, :], a_log[:, :, None])
    xs = _dir_specs(r, GDN_QKV, ncb, nb)
    ss = _dir_specs(r, SMALL_COLS, ncb, nb)
    ts = _dir_specs_t(small_t.shape[0], r, ncb, nb)
    os_ = _dir_specs(r, GDN_H * GDN_DV, ncb, nb)
    return pl.pallas_call(
        _gdn_kernel,
        grid=(nb,),
        in_specs=[*xs, *ss, *ts, _full(tri.shape), _full(trit.shape)] + [_full(p.shape) for p in params],
        out_specs=list(os_),
        out_shape=[jax.ShapeDtypeStruct((lt, GDN_H * GDN_DV), F32)] * 2,
        scratch_shapes=[pltpu.VMEM((2, GDN_H, GDN_DK, GDN_DV), F32)],
        compiler_params=_cparams(("arbitrary",)),
        name="gdn_scan",
    )(qkv, qkv, small, small, small_t, small_t, tri, trit, *params)


def _head_rms(y, n_heads, width):
    parts = []
    for hd in range(n_heads):
        t = y[:, hd * width:(hd + 1) * width]
        parts.append(t * lax.rsqrt(jnp.mean(t * t, axis=-1, keepdims=True) + EPS))
    return jnp.concatenate(parts, axis=1)


def _merge_kernel(xc_ref, xl_ref, mod_ref, g_ref, wg_ref, wb_ref, wo_ref,
                  lf_ref, lb_ref, lg_ref, rf_ref, rb_ref, rg_ref, gf_ref, gb_ref, gz_ref, sf_ref, sb_ref, sz_ref,
                  rn_ref, gn_ref, sn_ref, o_ref, *, n_ctx_tiles, tile0):
    is_ctx = pl.program_id(0) + tile0 < n_ctx_tiles
    d = xc_ref.shape[1]
    x = jnp.where(is_ctx, xc_ref[...], xl_ref[...])
    h = _rms_mod(x, g_ref[...], _pick_mod(mod_ref, is_ctx, 0, d), _pick_mod(mod_ref, is_ctx, 1, d)).astype(BF16)
    def lru_y():
        return (lf_ref[...] + lb_ref[...]) * _gelu_tanh(lg_ref[...])

    def ret_y():
        g_cols = slice(2 * RET_H * RET_DK + RET_H * RET_DV, RET_COLS)
        return _head_rms(rf_ref[...] + rb_ref[...], RET_H, RET_DV) * rn_ref[...] * _silu(rg_ref[:, g_cols])

    def gdn_y():
        return _head_rms(gf_ref[...] + gb_ref[...], GDN_H, GDN_DV) * gn_ref[...] * _silu(gz_ref[...])

    def ssd_y():
        ssd = (sf_ref[...] + sb_ref[...]) * _silu(sz_ref[...])
        return ssd * lax.rsqrt(jnp.mean(ssd * ssd, axis=-1, keepdims=True) + EPS) * sn_ref[...]

    merged = None
    for nb_, branch_y in enumerate((lru_y, ret_y, gdn_y, ssd_y)):
        pre = _dot(h, wg_ref[:, nb_ * d:(nb_ + 1) * d])
        t = _sigmoid(pre) * _dot(branch_y().astype(BF16), wb_ref[nb_])
        merged = t if merged is None else merged + t
    out = _dot(merged.astype(BF16), wo_ref[...])
    o_ref[...] = x + _pick_mod(mod_ref, is_ctx, 2, d) * out


def _merge_call(x_ctx, x_lat, lat_row0, mod, g, w_gate, w_branch, w_out, branches, norms, n_ctx, row_start):
    d = x_ctx.shape[1]
    tm = ROW_TILE
    lt = n_ctx + x_lat.shape[0] - lat_row0
    t0 = row_start // tm
    nt = lt // tm - t0
    row = lambda wd: pl.BlockSpec((tm, wd), lambda i: (i + t0, 0))
    in_specs = [*_row_sources(x_ctx, x_lat, lat_row0, n_ctx, tm, t0), _full(mod.shape), _full(g.shape),
                _full(w_gate.shape), _full(w_branch.shape), _full(w_out.shape)]
    in_specs += [row(b.shape[1]) for b in branches]
    in_specs += [_full(n.shape) for n in norms]
    return pl.pallas_call(
        functools.partial(_merge_kernel, n_ctx_tiles=n_ctx // tm, tile0=t0),
        grid=(nt,),
        in_specs=in_specs,
        out_specs=pl.BlockSpec((tm, d), lambda i: (i, 0)),
        out_shape=jax.ShapeDtypeStruct((nt * tm, d), F32),
        compiler_params=_cparams(("parallel",)),
        name="merge_out",
    )(x_ctx, x_lat, mod, g, w_gate, w_branch, w_out, *branches, *norms)


def _ffn_kernel(x_ref, mod_ref, g_ref, wg_ref, wu_ref, wd_ref, fn_ref, o_ref, *, n_ctx_tiles, final):
    is_ctx = pl.program_id(0) < n_ctx_tiles
    d = x_ref.shape[1]
    x = x_ref[...]
    h = _rms_mod(x, g_ref[...], _pick_mod(mod_ref, is_ctx, 3, d), _pick_mod(mod_ref, is_ctx, 4, d)).astype(BF16)
    act = (_silu(_dot(h, wg_ref[...])) * _dot(h, wu_ref[...])).astype(BF16)
    y = x + _pick_mod(mod_ref, is_ctx, 5, d) * _dot(act, wd_ref[...])
    if final:
        y = y * lax.rsqrt(jnp.mean(y * y, axis=-1, keepdims=True) + EPS) * fn_ref[...]
    o_ref[...] = y


def _ffn_call(x_rows, mod, g, wg, wu, wd, final_g, n_ctx_rows, final):
    n, d = x_rows.shape
    tm = ROW_TILE
    row = pl.BlockSpec((tm, d), lambda i: (i, 0))
    return pl.pallas_call(
        functools.partial(_ffn_kernel, n_ctx_tiles=n_ctx_rows // tm, final=final),
        grid=(n // tm,),
        in_specs=[row, _full(mod.shape), _full(g.shape), _full(wg.shape), _full(wu.shape), _full(wd.shape), _full(final_g.shape)],
        out_specs=row,
        out_shape=jax.ShapeDtypeStruct((n, d), F32),
        compiler_params=_cparams(("parallel",)),
        name="dense_swiglu",
    )(x_rows, mod, g, wg, wu, wd, final_g)


def _router_kernel(x_ref, mod_ref, g_ref, rt_ref, tri_ref, h_o, gate_o, slot_o, cnt_o):
    d = x_ref.shape[1]
    h = _rms_mod(x_ref[...], g_ref[...], mod_ref[0:1, 3 * d:4 * d], mod_ref[0:1, 4 * d:5 * d])
    h_o[...] = h.astype(BF16)
    logits = _dot_nt(rt_ref[...], h, HI)
    e, b = logits.shape
    eid = lax.broadcasted_iota(jnp.int32, (e, b), 0)
    m1 = jnp.max(logits, axis=0, keepdims=True)
    i1 = jnp.min(jnp.where(logits == m1, eid, e), axis=0, keepdims=True)
    rest = jnp.where(eid == i1, -jnp.inf, logits)
    m2 = jnp.max(rest, axis=0, keepdims=True)
    i2 = jnp.min(jnp.where(rest == m2, eid, e), axis=0, keepdims=True)
    t = jnp.exp(m2 - m1)
    p1 = 1.0 / (1.0 + t)
    p2 = t / (1.0 + t)
    sel1 = eid == i1
    sel2 = eid == i2
    gate_o[...] = jnp.where(sel1, p1, jnp.where(sel2, p2, 0.0))
    sel = jnp.logical_or(sel1, sel2)
    rank = _dot(sel.astype(BF16), tri_ref[...])
    slot_o[...] = jnp.where(sel, rank, -1.0).astype(jnp.int32)
    cnt = jnp.sum(sel.astype(F32), axis=1, keepdims=True)
    cnt_o[...] = jnp.broadcast_to(cnt, (e, LANES))[None].astype(jnp.int32)


def _router_call(xl, mod, g, router_t):
    n, d = xl.shape
    b = MOE_BLOCK
    nblk = n // b
    e = router_t.shape[0]
    pos = np.arange(b)
    tri = jnp.asarray((pos[:, None] < pos[None, :]).astype(np.float32), dtype=BF16)
    return pl.pallas_call(
        _router_kernel,
        grid=(nblk,),
        in_specs=[pl.BlockSpec((b, d), lambda i: (i, 0)), _full(mod.shape), _full(g.shape), _full(router_t.shape), _full(tri.shape)],
        out_specs=[pl.BlockSpec((b, d), lambda i: (i, 0)), pl.BlockSpec((e, b), lambda i: (0, i)),
                   pl.BlockSpec((e, b), lambda i: (0, i)), pl.BlockSpec((1, e, LANES), lambda i: (i, 0, 0))],
        out_shape=[jax.ShapeDtypeStruct((n, d), BF16), jax.ShapeDtypeStruct((e, n), F32),
                   jax.ShapeDtypeStruct((e, n), jnp.int32), jax.ShapeDtypeStruct((nblk, e, LANES), jnp.int32)],
        compiler_params=_cparams(("parallel",)),
        name="moe_router",
    )(xl, mod, g, router_t, tri)


def _moe_kernel(nfull_ref, tail_ref, x_ref, h_ref, gate_ref, slot_ref, mod_ref, fn_ref, wg_ref, wu_ref, wd_ref, o_ref,
                hs_ref, ys_ref, *, final):
    bi, ei, fi = pl.program_id(0), pl.program_id(1), pl.program_id(2)
    n_e, n_f = pl.num_programs(1), pl.num_programs(2)
    b, d = x_ref.shape
    nfull = nfull_ref[bi * n_e + ei]
    tail = tail_ref[bi * n_e + ei]
    tail_start = pl.multiple_of(nfull * MOE_SLOTS, MOE_SLOTS)
    tail_sizes = tuple(range(MOE_TAIL, MOE_SLOTS, MOE_TAIL))

    def for_tiles(fn):
        def body(j, carry):
            fn(pl.multiple_of(j * MOE_SLOTS, MOE_SLOTS), MOE_SLOTS)
            return carry
        lax.fori_loop(0, nfull, body, 0)
        for k, size in enumerate(tail_sizes):
            pl.when(tail == k + 1)(functools.partial(fn, tail_start, size))

    def onehot(start, size):
        return slot_ref[pl.ds(ei, 1), :] == lax.broadcasted_iota(jnp.int32, (size, b), 0) + start

    @pl.when(jnp.logical_and(ei == 0, fi == 0))
    def _():
        o_ref[...] = jnp.zeros_like(o_ref)

    def gather(start, size):
        hs_ref[pl.ds(start, size), :] = _dot(onehot(start, size).astype(BF16), h_ref[...]).astype(BF16)

    pl.when(fi == 0)(functools.partial(for_tiles, gather))

    def expert(start, size):
        hs = hs_ref[pl.ds(start, size), :]
        act = (_silu(_dot(hs, wg_ref[0])) * _dot(hs, wu_ref[0])).astype(BF16)
        y = _dot(act, wd_ref[0])

        @pl.when(fi == 0)
        def _():
            ys_ref[pl.ds(start, size), :] = y

        @pl.when(fi != 0)
        def _():
            ys_ref[pl.ds(start, size), :] = ys_ref[pl.ds(start, size), :] + y

    for_tiles(expert)

    def scatter(start, size):
        oh = onehot(start, size)
        gs = jnp.sum(jnp.where(oh, gate_ref[pl.ds(ei, 1), :], 0.0), axis=1, keepdims=True)
        o_ref[...] += _dot_tn(oh.astype(BF16), (ys_ref[pl.ds(start, size), :] * gs).astype(BF16))

    pl.when(fi == n_f - 1)(functools.partial(for_tiles, scatter))

    @pl.when(jnp.logical_and(ei == n_e - 1, fi == n_f - 1))
    def _():
        y = x_ref[...] + mod_ref[0:1, 5 * d:6 * d] * o_ref[...]
        if final:
            y = y * lax.rsqrt(jnp.mean(y * y, axis=-1, keepdims=True) + EPS) * fn_ref[...]
        o_ref[...] = y


def _moe_call(xl, h2, gate_t, slot_t, counts, mod, final_g, wg, wu, wd, final):
    n, d = xl.shape
    b = MOE_BLOCK
    e, _, f = wg.shape
    fs = f // MOE_FSPLIT
    counts = counts.reshape(-1)
    tail = (counts % MOE_SLOTS + (MOE_TAIL - 1)) // MOE_TAIL
    nfull = counts // MOE_SLOTS + tail // (MOE_SLOTS // MOE_TAIL)
    tail = tail % (MOE_SLOTS // MOE_TAIL)
    once = pl.Buffered(1)
    grid_spec = pltpu.PrefetchScalarGridSpec(
        num_scalar_prefetch=2,
        grid=(n // b, e, MOE_FSPLIT),
        in_specs=[pl.BlockSpec((b, d), lambda i, j, k, *_: (i, 0), pipeline_mode=once),
                  pl.BlockSpec((b, d), lambda i, j, k, *_: (i, 0), pipeline_mode=once),
                  pl.BlockSpec((e, b), lambda i, j, k, *_: (0, i)),
                  pl.BlockSpec((e, b), lambda i, j, k, *_: (0, i)),
                  pl.BlockSpec(mod.shape, lambda i, j, k, *_: (0, 0)),
                  pl.BlockSpec(final_g.shape, lambda i, j, k, *_: (0, 0)),
                  pl.BlockSpec((1, d, fs), lambda i, j, k, *_: (j, 0, k)),
                  pl.BlockSpec((1, d, fs), lambda i, j, k, *_: (j, 0, k)),
                  pl.BlockSpec((1, fs, d), lambda i, j, k, *_: (j, k, 0))],
        out_specs=pl.BlockSpec((b, d), lambda i, j, k, *_: (i, 0)),
        scratch_shapes=[pltpu.VMEM((b, d), BF16), pltpu.VMEM((b, d), F32)],
    )
    return pl.pallas_call(
        functools.partial(_moe_kernel, final=final),
        grid_spec=grid_spec,
        out_shape=jax.ShapeDtypeStruct((n, d), F32),
        compiler_params=_cparams(("parallel", "arbitrary", "arbitrary")),
        name="moe_experts",
    )(nfull, tail, xl, h2, gate_t, slot_t, mod, final_g, wg, wu, wd)


def _mix_weight(w_mix):
    d = w_mix.shape[0]
    off = np.concatenate([[0], np.cumsum(MIX_SPLITS)])
    seg = lambda k: w_mix[:, off[k]:off[k + 1]]

    def halves(t):
        t = t.reshape(d, RET_H, RET_DK // 2, 2)
        return jnp.concatenate([t[..., 0], t[..., 1]], axis=-1).reshape(d, RET_H * RET_DK)

    n_small = MIX_SPLITS[8] + MIX_SPLITS[9] + MIX_SPLITS[12]
    return jnp.concatenate([
        seg(0), seg(6), seg(11),
        seg(1),
        halves(seg(2)), halves(seg(3)), seg(4), seg(5),
        seg(7), seg(10),
        seg(8), seg(9), seg(12), jnp.zeros((d, SMALL_COLS - n_small), w_mix.dtype),
    ], axis=1).astype(BF16)


def _block_diag(w):
    n, i, o = w.shape
    eye = jnp.eye(n, dtype=w.dtype)
    return (eye[:, None, :, None] * w[:, :, None, :]).reshape(n * i, n * o)


def _rotary_tables(n_lat, n_ctx):
    assert GRID_W % SUBLANES == 0 and n_ctx % SUBLANES == 0 and ROW_TILE % GRID_W == 0
    n_freq = RET_DK // 4
    freqs = ROPE_BASE ** (-jnp.arange(n_freq, dtype=F32) / n_freq)
    lane = np.arange(RET_H * RET_DK)
    freq_lane = freqs[(lane % (RET_DK // 2)) % n_freq]
    sign = jnp.asarray(np.where(lane % RET_DK < RET_DK // 2, -1.0, 1.0).astype(np.float32))
    t8 = jnp.arange((n_ctx + n_lat) // SUBLANES) * SUBLANES - n_ctx
    row = jnp.where(t8 >= 0, t8 // GRID_W, 0).astype(F32)
    col = (jnp.arange(ROW_TILE) % GRID_W).astype(F32)

    def cs(pos):
        ang = pos[:, None] * freq_lane[None, :]
        return jnp.stack([jnp.cos(ang), sign[None, :] * jnp.sin(ang)])

    return cs(row), cs(col)


def kernel(x, c, ctx, c_ctx, w_mod, b_mod, norm_mix, norm_ffn, w_in, lru_conv_w, lru_conv_b, lru_wa, lru_ba, lru_wx, lru_bx, lru_lambda, ret_norm, gdn_conv_w, gdn_a_log, gdn_dt_bias, gdn_norm, ssd_conv_w, ssd_conv_b, ssd_a_log, ssd_dt_bias, ssd_d, ssd_norm, w_branch, w_out, ffn_wg, ffn_wu, ffn_wd, moe_router, moe_wg, moe_wu, moe_wd, final_norm):
    assert x.shape[0] == 1 and c.shape[0] == 1 and ctx.shape[0] == 1
    depth = w_mod.shape[0]
    n_lat, d = x.shape[1], x.shape[2]
    n_ctx = ctx.shape[1]
    assert n_ctx % ROW_TILE == 0 and n_lat % ROW_TILE == 0 and n_ctx % MIX_ROWS == 0 and n_lat % MIX_ROWS == 0
    gate_cols = N_BRANCH * d
    rot_rows, rot_cols = _rotary_tables(n_lat, n_ctx)
    b_mod3 = b_mod[:, None, :]
    ct = jnp.stack([c[0], c_ctx], axis=1)
    final_g = final_norm[None, :]

    x_ctx, x_lat, lat_row0 = ctx[0], x[0], 0
    for layer in range(depth):
        ctx_out = layer < depth - 1
        last = layer == depth - 1
        mod = _mod_call(ct, w_mod, b_mod3, layer)
        w_mix = _mix_weight(w_in[layer][:, gate_cols:])
        conv_w = jnp.concatenate([lru_conv_w[layer], gdn_conv_w[layer], ssd_conv_w[layer]], axis=1)
        conv_b = jnp.concatenate([lru_conv_b[layer], jnp.zeros((GDN_QKV,), F32), ssd_conv_b[layer]])[None, :]
        (lru_u, gdn_qkv, ssd_xbc, p_lg, p_ret, p_gz, p_sz, p_small) = _proj_call(
            x_ctx, x_lat, lat_row0, mod, norm_mix[layer][None, :], w_mix, rot_rows, rot_cols, conv_w, conv_b, n_ctx)
        small_t = p_small[:, 0:32].T

        lru_w = jnp.stack([jnp.concatenate([_block_diag(lru_wa[layer, dd]), _block_diag(lru_wx[layer, dd])], axis=1)
                           for dd in range(2)]).astype(BF16)
        lru_b = jnp.concatenate([lru_ba[layer], lru_bx[layer]], axis=1)[:, None, :]
        lru_f, lru_b_ = _lru_call(lru_u, lru_w, lru_b, lru_lambda[layer][:, None, :], n_ctx)
        ret_f, ret_b = _ret_call(p_ret, n_ctx)
        gdn_f, gdn_b = _gdn_call(gdn_qkv, p_small, small_t, gdn_a_log[layer], gdn_dt_bias[layer], n_ctx)
        ssd_f, ssd_b = _ssd_call(ssd_xbc, p_small, small_t, ssd_dt_bias[layer], ssd_a_log[layer], ssd_d[layer], n_ctx)

        branches = (lru_f, lru_b_, p_lg, ret_f, ret_b, p_ret, gdn_f, gdn_b, p_gz, ssd_f, ssd_b, p_sz)
        norms = (ret_norm[layer][None, :], jnp.tile(gdn_norm[layer], GDN_H)[None, :], ssd_norm[layer][None, :])
        row_start = 0 if ctx_out else n_ctx
        x_rows = _merge_call(x_ctx, x_lat, lat_row0, mod, norm_mix[layer][None, :],
                             w_in[layer][:, :gate_cols].astype(BF16), w_branch[layer].astype(BF16),
                             w_out[layer].astype(BF16), branches, norms, n_ctx, row_start)
        n_ctx_rows = n_ctx - row_start
        j = layer // 2
        if layer % 2 == 0:
            x_rows = _ffn_call(x_rows, mod, norm_ffn[layer][None, :], ffn_wg[j].astype(BF16), ffn_wu[j].astype(BF16),
                               ffn_wd[j].astype(BF16), final_g, n_ctx_rows, last)
        else:
            assert not ctx_out, "expert layers that must also emit context tokens are not supported"
            assert x_rows.shape[0] % MOE_BLOCK == 0 and MOE_BLOCK % MOE_SLOTS == 0 and MOE_SLOTS % MOE_TAIL == 0
            h2, gate_t, slot_t, cnt = _router_call(x_rows, mod, norm_ffn[layer][None, :], moe_router[j].T)
            x_rows = _moe_call(x_rows, h2, gate_t, slot_t, cnt[:, :, 0], mod, final_g, moe_wg[j].astype(BF16),
                               moe_wu[j].astype(BF16), moe_wd[j].astype(BF16), last)
        x_ctx, x_lat, lat_row0 = x_rows, x_rows, n_ctx - row_start
    out = x_rows[x_rows.shape[0] - n_lat:]
    return out[None]
```

```python
import functools
import math

import numpy as np
import jax
import jax.numpy as jnp
from jax import lax
from jax.experimental import pallas as pl
from jax.experimental.pallas import tpu as pltpu

F32 = jnp.float32
BF16 = jnp.bfloat16
HI = lax.Precision.HIGHEST

EPS = 1e-6
GRID_W = 64
N_BRANCH = 4
BRANCH_W = 512
CONV_W = 4
LRU_W = 512
LRU_BLOCKS = 8
LRU_C = 8.0
RET_H, RET_DK, RET_DV, RET_CHUNK = 4, 64, 128, 128
ROPE_BASE = 10000.0
GDN_H, GDN_DK, GDN_DV, GDN_CHUNK = 4, 128, 128, 64
SSD_H, SSD_P, SSD_G, SSD_N, SSD_CHUNK = 8, 64, 2, 64, 128
N_EXPERTS = 8
GDN_QKV = 2 * GDN_H * GDN_DK + GDN_H * GDN_DV
SSD_XBC = SSD_H * SSD_P + 2 * SSD_G * SSD_N
MIX_SPLITS = (LRU_W, LRU_W, RET_H * RET_DK, RET_H * RET_DK, RET_H * RET_DV, RET_H * RET_DV,
              GDN_QKV, GDN_H * GDN_DV, 2 * GDN_H, 2 * GDN_H, SSD_H * SSD_P, SSD_XBC, 2 * SSD_H)

LANES = 128
SUBLANES = 8
VMEM_LIMIT = 56 * 1024 * 1024

ROW_TILE = 256
MIX_ROWS = 256
MOE_BLOCK = 1024
MOE_SLOTS = 512
GDN_INV_BLOCK = 16
MOE_TAIL = 64
MOE_FSPLIT = 2

CONV_COLS = LRU_W + GDN_QKV + SSD_XBC
RET_COLS = 2 * RET_H * RET_DK + 2 * RET_H * RET_DV
SMALL_COLS = LANES


def _cparams(sem):
    return pltpu.CompilerParams(dimension_semantics=sem, vmem_limit_bytes=VMEM_LIMIT)


def _dot(a, b, precision=None):
    return jnp.dot(a, b, preferred_element_type=F32, precision=precision)


def _dot_nt(a, b, precision=None):
    return lax.dot_general(a, b, (((1,), (1,)), ((), ())), preferred_element_type=F32, precision=precision)


def _dot_tn(a, b, precision=None):
    return lax.dot_general(a, b, (((0,), (0,)), ((), ())), preferred_element_type=F32, precision=precision)


def _sigmoid(x):
    return 0.5 * jnp.tanh(0.5 * x) + 0.5


def _silu(x):
    return x * _sigmoid(x)


def _softplus(x):
    return jnp.maximum(x, 0.0) + jnp.log1p(jnp.exp(-jnp.abs(x)))


def _gelu_tanh(x):
    return 0.5 * x * (1.0 + jnp.tanh(math.sqrt(2.0 / math.pi) * (x + 0.044715 * (x * x * x))))


def _rms_mod(x, g, shift, scale):
    ms = jnp.mean(x * x, axis=-1, keepdims=True)
    return (x * lax.rsqrt(ms + EPS) * g) * (1.0 + scale) + shift


def _pick_mod(mod_ref, is_ctx, k, d):
    return jnp.where(is_ctx, mod_ref[1:2, k * d:(k + 1) * d], mod_ref[0:1, k * d:(k + 1) * d])


def _full(shape):
    n = len(shape)
    return pl.BlockSpec(shape, lambda *_: (0,) * n)


def _mod_kernel(ct_ref, w_ref, b_ref, o_ref):
    s = _silu(ct_ref[...])
    w = w_ref[0]
    b = b_ref[0]
    o_ref[0:1, :] = jnp.sum(s[:, 0:1] * w, axis=0, keepdims=True) + b
    o_ref[1:2, :] = jnp.sum(s[:, 1:2] * w, axis=0, keepdims=True) + b


def _mod_call(ct, w, b, layer):
    _, d, n = w.shape
    tn = 512
    return pl.pallas_call(
        _mod_kernel,
        grid=(n // tn,),
        in_specs=[_full((d, 2)), pl.BlockSpec((1, d, tn), lambda j: (layer, 0, j)),
                  pl.BlockSpec((1, 1, tn), lambda j: (layer, 0, j))],
        out_specs=pl.BlockSpec((2, tn), lambda j: (0, j)),
        out_shape=jax.ShapeDtypeStruct((2, n), F32),
        compiler_params=_cparams(("arbitrary",)),
        name="adaln_mod",
    )(ct, w, b)


def _proj_kernel(xc_ref, xl_ref, xcp_ref, xlp_ref, xcn_ref, xln_ref, mod_ref, g_ref, w_ref, rrow_ref, rcol_ref, cw_ref, cb_ref,
                 lru_o, gdn_o, ssd_o, lg_o, ret_o, gz_o, sz_o, sm_o, *, n_ctx_tiles):
    i = pl.program_id(0)
    is_ctx = i < n_ctx_tiles
    d = xc_ref.shape[1]
    tm = xc_ref.shape[0]
    shift = _pick_mod(mod_ref, is_ctx, 0, d)
    scale = _pick_mod(mod_ref, is_ctx, 1, d)
    hf = _rms_mod(jnp.where(is_ctx, xc_ref[...], xl_ref[...]), g_ref[...], shift, scale)
    h = hf.astype(BF16)

    def mm(a, b):
        return _dot(h, w_ref[:, a:b])

    has_prev = jnp.logical_and(i != 0, i != n_ctx_tiles).astype(F32)
    has_next = jnp.logical_and(i != n_ctx_tiles - 1, i != pl.num_programs(0) - 1).astype(F32)
    h_ext = jnp.concatenate([_rms_mod(jnp.where(is_ctx, xcp_ref[...], xlp_ref[...]), g_ref[...], shift, scale), hf,
                             _rms_mod(jnp.where(is_ctx, xcn_ref[...], xln_ref[...]), g_ref[...], shift, scale)],
                            axis=0).astype(BF16)
    sub = lax.broadcasted_iota(jnp.int32, (1, SUBLANES, 1), 1)
    g = tm // SUBLANES

    def conv(c0, c1):
        wd = c1 - c0
        pe = _dot(h_ext, w_ref[:, c0:c1]).reshape(g + 2, SUBLANES, wd)
        u = pe[1:g + 1]
        ext = jnp.concatenate([pe[0:1] * has_prev, u, pe[g + 1:g + 2] * has_next], axis=0)
        r1 = pltpu.roll(ext, 1, axis=1)
        r7 = pltpu.roll(ext, SUBLANES - 1, axis=1)
        r6 = pltpu.roll(ext, SUBLANES - 2, axis=1)
        um1 = jnp.where(sub >= 1, r1[1:g + 1], r1[0:g])
        up1 = jnp.where(sub < SUBLANES - 1, r7[1:g + 1], r7[2:g + 2])
        up2 = jnp.where(sub < SUBLANES - 2, r6[1:g + 1], r6[2:g + 2])
        w = cw_ref[:, c0:c1]
        y = w[0:1] * um1 + w[1:2] * u + w[2:3] * up1 + w[3:4] * up2 + cb_ref[:, c0:c1]
        return y.reshape(tm, wd)

    step = 2 * LANES
    conv_jobs, plain_jobs = [], []

    def lru_job(c0):
        lru_o[:, c0:c0 + step] = conv(c0, c0 + step)

    def gdn_qk_job(c0):
        t2 = _silu(conv(LRU_W + c0, LRU_W + c0 + step))
        for k in range(step // GDN_DK):
            t = t2[:, k * GDN_DK:(k + 1) * GDN_DK]
            t = t * lax.rsqrt(jnp.sum(t * t, axis=-1, keepdims=True) + EPS)
            if c0 < GDN_H * GDN_DK:
                t = t * (GDN_DK ** -0.5)
            gdn_o[:, c0 + k * GDN_DK:c0 + (k + 1) * GDN_DK] = t

    def gdn_v_job(c0):
        gdn_o[:, c0:c0 + step] = _silu(conv(LRU_W + c0, LRU_W + c0 + step))

    def ssd_job(c0):
        ssd_o[:, c0:c0 + step] = _silu(conv(LRU_W + GDN_QKV + c0, LRU_W + GDN_QKV + c0 + step))

    conv_jobs += [functools.partial(lru_job, c0) for c0 in range(0, LRU_W, step)]
    conv_jobs += [functools.partial(gdn_qk_job, c0) for c0 in range(0, 2 * GDN_H * GDN_DK, step)]
    conv_jobs += [functools.partial(gdn_v_job, c0) for c0 in range(2 * GDN_H * GDN_DK, GDN_QKV, step)]
    conv_jobs += [functools.partial(ssd_job, c0) for c0 in range(0, SSD_XBC, step)]

    qk_w = 2 * RET_H * RET_DK
    half = RET_DK // 2
    lane = lax.broadcasted_iota(jnp.int32, (tm, RET_H * RET_DK), 1)
    first = (lane % RET_DK) < half

    by_row = (lane % half) < half // 2
    rows8 = (tm // SUBLANES, SUBLANES, RET_H * RET_DK)

    def table(k, rest):
        r = jnp.broadcast_to(rrow_ref[k][:, None, :], rows8).reshape(tm, RET_H * RET_DK)
        return jnp.where(by_row, r, jnp.where(is_ctx, rest, rcol_ref[k]))

    tables = []

    def rot(t):
        if not tables:
            tables.extend([table(0, 1.0), table(1, 0.0)])
        partner = jnp.where(first, pltpu.roll(t, RET_H * RET_DK - half, axis=1), pltpu.roll(t, half, axis=1))
        return t * tables[0] + partner * tables[1]

    c_lg = CONV_COLS
    c_ret = c_lg + LRU_W
    c_gz = c_ret + RET_COLS
    c_sz = c_gz + GDN_H * GDN_DV
    c_sm = c_sz + SSD_H * SSD_P

    def plain(o_ref, o0, c0, width):
        def job():
            o_ref[:, o0:o0 + width] = mm(c0, c0 + width)
        return job

    def ret_q_job():
        ret_o[:, 0:qk_w // 2] = rot(mm(c_ret, c_ret + qk_w // 2)) * (RET_DK ** -0.5)

    def ret_k_job():
        ret_o[:, qk_w // 2:qk_w] = rot(mm(c_ret + qk_w // 2, c_ret + qk_w))

    plain_jobs += [plain(lg_o, 0, c_lg, LRU_W), ret_q_job, ret_k_job,
                   plain(ret_o, qk_w, c_ret + qk_w, RET_H * RET_DV),
                   plain(ret_o, qk_w + RET_H * RET_DV, c_ret + qk_w + RET_H * RET_DV, RET_H * RET_DV),
                   plain(gz_o, 0, c_gz, GDN_H * GDN_DV), plain(sz_o, 0, c_sz, SSD_H * SSD_P),
                   plain(sm_o, 0, c_sm, SMALL_COLS)]
    for k in range(max(len(conv_jobs), len(plain_jobs))):
        if k < len(conv_jobs):
            conv_jobs[k]()
        if k < len(plain_jobs):
            plain_jobs[k]()


def _row_sources(x_ctx, x_lat, lat_row0, n_ctx, tm, tile0=0):
    d = x_ctx.shape[1]
    nct, ob = n_ctx // tm, lat_row0 // tm
    return (pl.BlockSpec((tm, d), lambda i: (jnp.minimum(i + tile0, nct - 1), 0)),
            pl.BlockSpec((tm, d), lambda i: (jnp.maximum(i + tile0 - nct, 0) + ob, 0)))


def _proj_call(x_ctx, x_lat, lat_row0, mod, g, w, rot_rows, rot_cols, conv_w, conv_b, n_ctx):
    d = x_ctx.shape[1]
    tm = ROW_TILE
    lt = n_ctx + x_lat.shape[0] - lat_row0
    nt, nct, ob = lt // tm, n_ctx // tm, lat_row0 // tm
    hb = tm // SUBLANES
    widths = (LRU_W, GDN_QKV, SSD_XBC, LRU_W, RET_COLS, GDN_H * GDN_DV, SSD_H * SSD_P, SMALL_COLS)
    row = lambda wd: pl.BlockSpec((tm, wd), lambda i: (i, 0))
    halo = lambda fn: pl.BlockSpec((SUBLANES, d), lambda i: (fn(i), 0))
    return pl.pallas_call(
        functools.partial(_proj_kernel, n_ctx_tiles=nct),
        grid=(nt,),
        in_specs=[*_row_sources(x_ctx, x_lat, lat_row0, n_ctx, tm),
                  halo(lambda i: jnp.clip(i * hb - 1, 0, nct * hb - 1)),
                  halo(lambda i: jnp.maximum((i - nct) * hb - 1, 0) + ob * hb),
                  halo(lambda i: jnp.minimum((i + 1) * hb, nct * hb - 1)),
                  halo(lambda i: jnp.clip((i + 1 - nct) * hb, 0, (nt - nct) * hb - 1) + ob * hb),
                  _full(mod.shape), _full(g.shape), _full(w.shape),
                  pl.BlockSpec((2, tm // SUBLANES, rot_rows.shape[2]), lambda i: (0, i, 0)), _full(rot_cols.shape),
                  _full(conv_w.shape), _full(conv_b.shape)],
        out_specs=[row(wd) for wd in widths],
        out_shape=[jax.ShapeDtypeStruct((lt, wd), F32) for wd in widths],
        compiler_params=_cparams(("parallel",)),
        name="mix_proj",
    )(x_ctx, x_lat, x_ctx, x_lat, x_ctx, x_lat, mod, g, w, rot_rows, rot_cols, conv_w, conv_b)


def _bwd_block(i, n_ctx_blocks, n_blocks):
    return jnp.where(i < n_ctx_blocks, n_ctx_blocks - 1 - i, n_blocks + n_ctx_blocks - 1 - i)


def _dir_specs(r, width, ncb, nb):
    return (pl.BlockSpec((r, width), lambda i: (i, 0)),
            pl.BlockSpec((r, width), lambda i: (_bwd_block(i, ncb, nb), 0)))


def _dir_specs_t(rows, r, ncb, nb):
    return (pl.BlockSpec((rows, r), lambda i: (0, i)),
            pl.BlockSpec((rows, r), lambda i: (0, _bwd_block(i, ncb, nb))))


def _chunk_order(d, n):
    return range(n) if d == 0 else range(n - 1, -1, -1)


def _lru_kernel(uf_ref, ub_ref, w_ref, b_ref, lam_ref, yf_o, yb_o, carry_ref):
    @pl.when(pl.program_id(0) == 0)
    def _():
        carry_ref[...] = jnp.zeros_like(carry_ref)

    r = uf_ref.shape[0]
    sub = lax.broadcasted_iota(jnp.int32, (1, SUBLANES, 1), 1)

    def run(d, u_ref, o_ref):
        u = u_ref[...]
        gates = _sigmoid(_dot(u.astype(BF16), w_ref[d]) + b_ref[d])
        rg = gates[:, 0:LRU_W]
        ig = gates[:, LRU_W:2 * LRU_W]
        log_a = (-LRU_C) * rg * _softplus(-lam_ref[d])
        a = jnp.exp(log_a)
        b = jnp.sqrt(1.0 - a * a) * (ig * u)
        a = a.reshape(r // SUBLANES, SUBLANES, LRU_W)
        b = b.reshape(r // SUBLANES, SUBLANES, LRU_W)
        sh = 1
        while sh < SUBLANES:
            valid = sub >= sh if d == 0 else sub < SUBLANES - sh
            shift = sh if d == 0 else SUBLANES - sh
            a_s = pltpu.roll(a, shift, axis=1)
            b_s = pltpu.roll(b, shift, axis=1)
            b = jnp.where(valid, a * b_s + b, b)
            a = jnp.where(valid, a * a_s, a)
            sh *= 2
        carry = carry_ref[d, 0:1, :]
        for g in _chunk_order(d, r // SUBLANES):
            gs = slice(g * SUBLANES, (g + 1) * SUBLANES)
            h = a[g] * carry + b[g]
            o_ref[gs, :] = h
            carry = h[SUBLANES - 1:SUBLANES, :] if d == 0 else h[0:1, :]
        carry_ref[d, 0:1, :] = carry

    run(0, uf_ref, yf_o)
    run(1, ub_ref, yb_o)


def _lru_call(u, w, b, lam, n_ctx):
    lt = u.shape[0]
    r = MIX_ROWS
    nb, ncb = lt // r, n_ctx // r
    fs, bs = _dir_specs(r, LRU_W, ncb, nb)
    return pl.pallas_call(
        _lru_kernel,
        grid=(nb,),
        in_specs=[fs, bs, _full(w.shape), _full(b.shape), _full(lam.shape)],
        out_specs=[fs, bs],
        out_shape=[jax.ShapeDtypeStruct((lt, LRU_W), F32)] * 2,
        scratch_shapes=[pltpu.VMEM((2, 8, LRU_W), F32)],
        compiler_params=_cparams(("arbitrary",)),
        name="rglru_scan",
    )(u, u, w, b, lam)


def _ret_kernel(xf_ref, xb_ref, dmat_ref, qd_ref, kd_ref, sdec_ref, bd_ref, yf_o, yb_o, s_ref):
    @pl.when(pl.program_id(0) == 0)
    def _():
        s_ref[...] = jnp.zeros_like(s_ref)

    c = RET_CHUNK
    qw = RET_H * RET_DK
    lane_head = lax.broadcasted_iota(jnp.int32, (c, qw), 1) // RET_DK

    nch = xf_ref.shape[0] // c
    refs = ((xf_ref, yf_o), (xb_ref, yb_o))
    items = [(d, ck) for ck in range(nch) for d in range(2)]

    def rows(ck):
        return slice(ck * c, (ck + 1) * c)

    y_intra, upd, qdec = {}, {}, {}
    for it in items:
        d, ck = it
        x_ref = refs[d][0]
        rs = rows(ck)
        q = x_ref[rs, 0:qw]
        k = x_ref[rs, qw:2 * qw]
        kb = k.astype(BF16)
        vb = x_ref[rs, 2 * qw:2 * qw + RET_H * RET_DV].astype(BF16)
        upd[it] = bd_ref[...] * _dot_tn((k * kd_ref[d]).astype(BF16), vb)
        qdec[it] = (q * qd_ref[d]).astype(BF16)
        parts = []
        for hd in range(RET_H):
            qh = jnp.where(lane_head == hd, q, 0.0).astype(BF16)
            sc = _dot_nt(qh, kb) * dmat_ref[d, hd]
            parts.append(_dot(sc.astype(BF16), vb[:, hd * RET_DV:(hd + 1) * RET_DV]))
        y_intra[it] = parts

    state = [s_ref[0], s_ref[1]]
    for pos in range(nch):
        for d in range(2):
            ck = pos if d == 0 else nch - 1 - pos
            y_inter = _dot(qdec[d, ck], state[d].astype(BF16))
            for hd in range(RET_H):
                vs = slice(hd * RET_DV, (hd + 1) * RET_DV)
                refs[d][1][rows(ck), vs] = y_intra[d, ck][hd] + y_inter[:, vs]
            state[d] = sdec_ref[...] * state[d] + upd[d, ck]
    s_ref[0] = state[0]
    s_ref[1] = state[1]


def _ret_tables():
    c = RET_CHUNK
    lg = np.log(1.0 - 2.0 ** (-5.0 - np.arange(RET_H)))
    pos = np.arange(c, dtype=np.float64)
    dist = pos[:, None] - pos[None, :]
    d_f = np.where(dist >= 0, np.exp(np.maximum(dist, 0.0)[None] * lg[:, None, None]), 0.0)
    d_b = np.where(dist < 0, np.exp(np.maximum(-dist, 0.0)[None] * lg[:, None, None]), 0.0)
    dmat = np.stack([d_f, d_b])
    rep = lambda t: np.repeat(t, RET_DK, axis=1)
    qd = np.stack([rep(np.exp((pos + 1.0)[:, None] * lg)), rep(np.exp((c - pos)[:, None] * lg))])
    kd = np.stack([rep(np.exp((c - 1.0 - pos)[:, None] * lg)), rep(np.exp(pos[:, None] * lg))])
    hk = np.repeat(np.arange(RET_H), RET_DK)
    hv = np.repeat(np.arange(RET_H), RET_DV)
    bd = hk[:, None] == hv[None, :]
    sdec = np.broadcast_to(np.repeat(np.exp(c * lg), RET_DK)[:, None], bd.shape)
    return tuple(jnp.asarray(t, dtype=F32) for t in (dmat, qd, kd, sdec, bd))


def _ret_call(x, n_ctx):
    lt = x.shape[0]
    r = MIX_ROWS
    nb, ncb = lt // r, n_ctx // r
    tabs = _ret_tables()
    fs, bs = _dir_specs(r, 2 * RET_H * RET_DK + RET_H * RET_DV, ncb, nb)
    os_f, os_b = _dir_specs(r, RET_H * RET_DV, ncb, nb)
    return pl.pallas_call(
        _ret_kernel,
        grid=(nb,),
        in_specs=[fs, bs] + [_full(t.shape) for t in tabs],
        out_specs=[os_f, os_b],
        out_shape=[jax.ShapeDtypeStruct((lt, RET_H * RET_DV), F32)] * 2,
        scratch_shapes=[pltpu.VMEM((2, RET_H * RET_DK, RET_H * RET_DV), F32)],
        compiler_params=_cparams(("arbitrary",)),
        name="retention_scan",
    )(x, x, *tabs)


def _ssd_kernel(xf_ref, xb_ref, smf_ref, smb_ref, stf_ref, stb_ref, tri_ref, trit_ref,
                dtb_c_ref, dtb_r_ref, alog_c_ref, alog_r_ref, dskip_ref, yf_o, yb_o, s_ref):
    @pl.when(pl.program_id(0) == 0)
    def _():
        s_ref[...] = jnp.zeros_like(s_ref)

    c = SSD_CHUNK
    xw = SSD_H * SSD_P
    gw = SSD_G * SSD_N
    lo = lax.broadcasted_iota(jnp.int32, (c, LANES), 1) < SSD_N
    row_lo = lax.broadcasted_iota(jnp.int32, (LANES, 1), 0) < SSD_N
    lane_lo = lax.broadcasted_iota(jnp.int32, (1, LANES), 1) < SSD_N
    bd = row_lo == lane_lo
    dt0 = 2 * 2 * GDN_H

    nch = xf_ref.shape[0] // c
    n_pair = SSD_H // 2
    refs = ((xf_ref, smf_ref, stf_ref, yf_o), (xb_ref, smb_ref, stb_ref, yb_o))
    mask = [tri_ref[d] > 0.0 for d in range(2)]

    def rows(ck):
        return slice(ck * c, (ck + 1) * c)

    sc = {}
    for d in range(2):
        _, sm_ref, st_ref, _ = refs[d]
        a_c = -jnp.exp(alog_c_ref[d])
        a_r = -jnp.exp(alog_r_ref[d])
        for ck in range(nch):
            rs = rows(ck)
            dtc = _softplus(sm_ref[rs, dt0 + d * SSD_H:dt0 + (d + 1) * SSD_H] + dtb_c_ref[d])
            dtr = _softplus(st_ref[dt0 + d * SSD_H:dt0 + (d + 1) * SSD_H, rs] + dtb_r_ref[d])
            cs_c = _dot(tri_ref[d], dtc * a_c, HI)
            cs_r = _dot(dtr * a_r, trit_ref[d], HI)
            tot = cs_c[c - 1:c, :] if d == 0 else cs_c[0:1, :]
            sc[d, ck] = dict(dtr=dtr, cs_c=cs_c, cs_r=cs_r, e_c=jnp.exp(cs_c), dec_c=jnp.exp(tot - cs_c) * dtc,
                             e_tot=jnp.exp(tot))

    cb, c_dup, b_dup = {}, {}, {}
    for d in range(2):
        x_ref = refs[d][0]
        for ck in range(nch):
            rs = rows(ck)
            bm = x_ref[rs, xw:xw + gw]
            cm = x_ref[rs, xw + gw:xw + 2 * gw]
            bmb = bm.astype(BF16)
            b_roll = pltpu.roll(bm, SSD_N, axis=1)
            c_roll = pltpu.roll(cm, SSD_N, axis=1)
            for g in range(SSD_G):
                keep = lo if g == 0 else jnp.logical_not(lo)
                cb[d, ck, g] = _dot_nt(jnp.where(keep, cm, 0.0).astype(BF16), bmb)
                c_dup[d, ck, g] = jnp.where(keep, cm, c_roll)
                b_dup[d, ck, g] = jnp.where(keep, bm, b_roll)

    items = [(d, ck, m) for ck in range(nch) for d in range(2) for m in range(n_pair)]
    y_intra, upd, cq = {}, {}, {}
    for it in items:
        d, ck, m = it
        s_ = sc[d, ck]
        g = m // (n_pair // SSD_G)
        h0, h1 = 2 * m, 2 * m + 1
        scores = []
        for hd in (h0, h1):
            seg = s_["cs_c"][:, hd:hd + 1] - s_["cs_r"][hd:hd + 1, :]
            lmat = jnp.where(mask[d], jnp.exp(jnp.where(mask[d], seg, 0.0)), 0.0)
            scores.append(cb[d, ck, g] * lmat * s_["dtr"][hd:hd + 1, :])
        scb = jnp.concatenate(scores, axis=1).astype(BF16)
        ls = slice(m * LANES, (m + 1) * LANES)
        xp = refs[d][0][rows(ck), ls]
        xs = jnp.concatenate([jnp.where(lo, xp, 0.0), jnp.where(lo, 0.0, xp)], axis=0).astype(BF16)
        y = _dot(scb, xs)
        if d == 0:
            y = y + dskip_ref[:, ls] * xp
        y_intra[it] = y
        dec_pair = jnp.where(lo, s_["dec_c"][:, h0:h0 + 1], s_["dec_c"][:, h1:h1 + 1])
        upd[it] = jnp.where(bd, _dot_tn((b_dup[d, ck, g] * dec_pair).astype(BF16), xp.astype(BF16)), 0.0)
        e_pair = jnp.where(lo, s_["e_c"][:, h0:h0 + 1], s_["e_c"][:, h1:h1 + 1])
        cq[it] = (c_dup[d, ck, g] * e_pair).astype(BF16)

    state = {(d, m): s_ref[d, m] for d in range(2) for m in range(n_pair)}
    for pos in range(nch):
        for d in range(2):
            ck = pos if d == 0 else nch - 1 - pos
            e_tot = sc[d, ck]["e_tot"]
            for m in range(n_pair):
                it = (d, ck, m)
                st = state[d, m]
                refs[d][3][rows(ck), m * LANES:(m + 1) * LANES] = y_intra[it] + _dot(cq[it], st.astype(BF16))
                sdec = jnp.where(row_lo, e_tot[:, 2 * m:2 * m + 1], e_tot[:, 2 * m + 1:2 * m + 2])
                state[d, m] = sdec * st + upd[it]
    for (d, m), st in state.items():
        s_ref[d, m] = st


def _tri_tables(c):
    pos = np.arange(c)
    lower = (pos[:, None] >= pos[None, :]).astype(np.float32)
    tri = np.stack([lower, lower.T])
    trit = np.stack([lower.T, lower])
    return jnp.asarray(tri), jnp.asarray(trit)


def _ssd_call(xbc, small, small_t, dt_bias, a_log, d_skip, n_ctx):
    lt = xbc.shape[0]
    r = MIX_ROWS
    nb, ncb = lt // r, n_ctx // r
    tri, trit = _tri_tables(SSD_CHUNK)
    params = (dt_bias[:, None, :], dt_bias[:, :, None], a_log[:, None, :], a_log[:, :, None],
              jnp.repeat(d_skip, SSD_P)[None, :])
    xs = _dir_specs(r, SSD_XBC, ncb, nb)
    ss = _dir_specs(r, SMALL_COLS, ncb, nb)
    ts = _dir_specs_t(small_t.shape[0], r, ncb, nb)
    os_ = _dir_specs(r, SSD_H * SSD_P, ncb, nb)
    return pl.pallas_call(
        _ssd_kernel,
        grid=(nb,),
        in_specs=[*xs, *ss, *ts, _full(tri.shape), _full(trit.shape)] + [_full(p.shape) for p in params],
        out_specs=list(os_),
        out_shape=[jax.ShapeDtypeStruct((lt, SSD_H * SSD_P), F32)] * 2,
        scratch_shapes=[pltpu.VMEM((2, SSD_H // 2, 2 * SSD_N, 2 * SSD_P), F32)],
        compiler_params=_cparams(("arbitrary",)),
        name="ssd_scan",
    )(xbc, xbc, small, small, small_t, small_t, tri, trit, *params)


def _gdn_kernel(xf_ref, xb_ref, smf_ref, smb_ref, stf_ref, stb_ref, tri_ref, trit_ref,
                dtb_c_ref, dtb_r_ref, alog_c_ref, alog_r_ref, yf_o, yb_o, s_ref):
    @pl.when(pl.program_id(0) == 0)
    def _():
        s_ref[...] = jnp.zeros_like(s_ref)

    c = GDN_CHUNK
    kw = GDN_H * GDN_DK
    ri = lax.broadcasted_iota(jnp.int32, (c, c), 0)
    ci_ = lax.broadcasted_iota(jnp.int32, (c, c), 1)
    eye = (ri == ci_).astype(F32)

    nch = xf_ref.shape[0] // c
    refs = ((xf_ref, smf_ref, stf_ref, yf_o), (xb_ref, smb_ref, stb_ref, yb_o))
    incl = [tri_ref[d] > 0.0 for d in range(2)]
    strict = [jnp.logical_and(incl[d], ri != ci_) for d in range(2)]

    def rows(ck):
        return slice(ck * c, (ck + 1) * c)

    def q_of(d, ck, hd):
        return refs[d][0][rows(ck), hd * GDN_DK:(hd + 1) * GDN_DK]

    def k_of(d, ck, hd):
        return refs[d][0][rows(ck), kw + hd * GDN_DK:kw + (hd + 1) * GDN_DK]

    def v_of(d, ck, hd):
        return refs[d][0][rows(ck), 2 * kw + hd * GDN_DV:2 * kw + (hd + 1) * GDN_DV]

    sc = {}
    for d in range(2):
        _, sm_ref, st_ref, _ = refs[d]
        for ck in range(nch):
            rs = rows(ck)
            a_c = sm_ref[rs, d * GDN_H:(d + 1) * GDN_H]
            b_c = sm_ref[rs, 2 * GDN_H + d * GDN_H:2 * GDN_H + (d + 1) * GDN_H]
            a_r = st_ref[d * GDN_H:(d + 1) * GDN_H, rs]
            g_c = -jnp.exp(alog_c_ref[d]) * _softplus(a_c + dtb_c_ref[d])
            g_r = -jnp.exp(alog_r_ref[d]) * _softplus(a_r + dtb_r_ref[d])
            gcs_c = _dot(tri_ref[d], g_c, HI)
            gcs_r = _dot(g_r, trit_ref[d], HI)
            g_last = gcs_c[c - 1:c, :] if d == 0 else gcs_c[0:1, :]
            sc[d, ck] = dict(beta=_sigmoid(b_c), gcs_c=gcs_c, gcs_r=gcs_r, e_c=jnp.exp(gcs_c),
                             kdec=jnp.exp(g_last - gcs_c), e_last=jnp.exp(g_last))

    items = [(d, ck, hd) for ck in range(nch) for d in range(2) for hd in range(GDN_H)]

    lm, attn = {}, {}
    for it in items:
        d, ck, hd = it
        s_ = sc[d, ck]
        seg = s_["gcs_c"][:, hd:hd + 1] - s_["gcs_r"][hd:hd + 1, :]
        dmat = jnp.where(incl[d], jnp.exp(jnp.where(incl[d], seg, 0.0)), 0.0)
        kh = k_of(*it)
        both = _dot_nt(jnp.concatenate([kh * s_["beta"][:, hd:hd + 1], q_of(*it)], axis=0).astype(BF16), kh.astype(BF16))
        lm[it] = jnp.where(strict[d], both[0:c] * dmat, 0.0)
        attn[it] = (both[c:2 * c] * dmat).astype(BF16)

    blk = GDN_INV_BLOCK
    same = [(ri // (blk << k)) == (ci_ // (blk << k)) for k in range(int(math.log2(c // blk)) + 1)]
    diag = {it: jnp.where(same[0], lm[it], 0.0) for it in items}
    inv = {it: eye - diag[it] for it in items}
    pw = {it: diag[it].astype(BF16) for it in items}
    pw = {it: _dot(pw[it], pw[it]).astype(BF16) for it in items}
    n_sq = int(math.log2(blk)) - 1
    for step in range(n_sq):
        if step < n_sq - 1:
            both = {it: _dot(jnp.concatenate([inv[it].astype(BF16), pw[it]], axis=0), pw[it]) for it in items}
            inv = {it: inv[it] + both[it][0:c] for it in items}
            pw = {it: both[it][c:2 * c].astype(BF16) for it in items}
        else:
            inv = {it: inv[it] + _dot(inv[it].astype(BF16), pw[it]) for it in items}
    for k in range(1, len(same)):
        off = jnp.logical_and(same[k], jnp.logical_not(same[k - 1]))
        invb = {it: inv[it].astype(BF16) for it in items}
        tmp = {it: _dot(invb[it], jnp.where(off, lm[it], 0.0).astype(BF16)).astype(BF16) for it in items}
        inv = {it: inv[it] - _dot(tmp[it], invb[it]) for it in items}

    u, w = {}, {}
    for it in items:
        d, ck, hd = it
        s_ = sc[d, ck]
        bc = s_["beta"][:, hd:hd + 1]
        rhs = jnp.concatenate([v_of(*it) * bc, k_of(*it) * (bc * s_["e_c"][:, hd:hd + 1])], axis=1)
        sol = _dot(inv[it].astype(BF16), rhs.astype(BF16))
        u[it] = sol[:, 0:GDN_DV]
        w[it] = sol[:, GDN_DV:GDN_DV + GDN_DK].astype(BF16)

    state = {(d, hd): s_ref[d, hd] for d in range(2) for hd in range(GDN_H)}
    for pos in range(nch):
        cur = [(d, pos if d == 0 else nch - 1 - pos, hd) for d in range(2) for hd in range(GDN_H)]
        sb = {it: state[it[0], it[2]].astype(BF16) for it in cur}
        wq = {it: _dot(jnp.concatenate(
            [w[it], (q_of(*it) * sc[it[0], it[1]]["e_c"][:, it[2]:it[2] + 1]).astype(BF16)], axis=0), sb[it]) for it in cur}
        for it in cur:
            d, ck, hd = it
            s_ = sc[d, ck]
            vpb = (u[it] - wq[it][0:c]).astype(BF16)
            refs[d][3][rows(ck), hd * GDN_DV:(hd + 1) * GDN_DV] = wq[it][c:2 * c] + _dot(attn[it], vpb)
            state[d, hd] = (s_["e_last"][:, hd:hd + 1] * state[d, hd]
                            + _dot_tn((k_of(*it) * s_["kdec"][:, hd:hd + 1]).astype(BF16), vpb))
    for (d, hd), s in state.items():
        s_ref[d, hd] = s


def _gdn_call(qkv, small, small_t, a_log, dt_bias, n_ctx):
    lt = qkv.shape[0]
    r = MIX_ROWS
    nb, ncb = lt // r, n_ctx // r
    tri, trit = _tri_tables(GDN_CHUNK)
    params = (dt_bias[:, None, :], dt_bias[:, :, None], a_log[:, None, :], a_log[:, :, None])
    xs = _dir_specs(r, GDN_QKV, ncb, nb)
    ss = _dir_specs(r, SMALL_COLS, ncb, nb)
    ts = _dir_specs_t(small_t.shape[0], r, ncb, nb)
    os_ = _dir_specs(r, GDN_H * GDN_DV, ncb, nb)
    return pl.pallas_call(
        _gdn_kernel,
        grid=(nb,),
        in_specs=[*xs, *ss, *ts, _full(tri.shape), _full(trit.shape)] + [_full(p.shape) for p in params],
        out_specs=list(os_),
        out_shape=[jax.ShapeDtypeStruct((lt, GDN_H * GDN_DV), F32)] * 2,
        scratch_shapes=[pltpu.VMEM((2, GDN_H, GDN_DK, GDN_DV), F32)],
        compiler_params=_cparams(("arbitrary",)),
        name="gdn_scan",
    )(qkv, qkv, small, small, small_t, small_t, tri, trit, *params)


def _head_rms(y, n_heads, width):
    parts = []
    for hd in range(n_heads):
        t = y[:, hd * width:(hd + 1) * width]
        parts.append(t * lax.rsqrt(jnp.mean(t * t, axis=-1, keepdims=True) + EPS))
    return jnp.concatenate(parts, axis=1)


def _merge_kernel(xc_ref, xl_ref, mod_ref, g_ref, wg_ref, wb_ref, wo_ref,
                  lf_ref, lb_ref, lg_ref, rf_ref, rb_ref, rg_ref, gf_ref, gb_ref, gz_ref, sf_ref, sb_ref, sz_ref,
                  rn_ref, gn_ref, sn_ref, o_ref, *, n_ctx_tiles, tile0):
    is_ctx = pl.program_id(0) + tile0 < n_ctx_tiles
    d = xc_ref.shape[1]
    x = jnp.where(is_ctx, xc_ref[...], xl_ref[...])
    h = _rms_mod(x, g_ref[...], _pick_mod(mod_ref, is_ctx, 0, d), _pick_mod(mod_ref, is_ctx, 1, d)).astype(BF16)
    def lru_y():
        return (lf_ref[...] + lb_ref[...]) * _gelu_tanh(lg_ref[...])

    def ret_y():
        g_cols = slice(2 * RET_H * RET_DK + RET_H * RET_DV, RET_COLS)
        return _head_rms(rf_ref[...] + rb_ref[...], RET_H, RET_DV) * rn_ref[...] * _silu(rg_ref[:, g_cols])

    def gdn_y():
        return _head_rms(gf_ref[...] + gb_ref[...], GDN_H, GDN_DV) * gn_ref[...] * _silu(gz_ref[...])

    def ssd_y():
        ssd = (sf_ref[...] + sb_ref[...]) * _silu(sz_ref[...])
        return ssd * lax.rsqrt(jnp.mean(ssd * ssd, axis=-1, keepdims=True) + EPS) * sn_ref[...]

    merged = None
    for nb_, branch_y in enumerate((lru_y, ret_y, gdn_y, ssd_y)):
        pre = _dot(h, wg_ref[:, nb_ * d:(nb_ + 1) * d])
        t = _sigmoid(pre) * _dot(branch_y().astype(BF16), wb_ref[nb_])
        merged = t if merged is None else merged + t
    out = _dot(merged.astype(BF16), wo_ref[...])
    o_ref[...] = x + _pick_mod(mod_ref, is_ctx, 2, d) * out


def _merge_call(x_ctx, x_lat, lat_row0, mod, g, w_gate, w_branch, w_out, branches, norms, n_ctx, row_start):
    d = x_ctx.shape[1]
    tm = ROW_TILE
    lt = n_ctx + x_lat.shape[0] - lat_row0
    t0 = row_start // tm
    nt = lt // tm - t0
    row = lambda wd: pl.BlockSpec((tm, wd), lambda i: (i + t0, 0))
    in_specs = [*_row_sources(x_ctx, x_lat, lat_row0, n_ctx, tm, t0), _full(mod.shape), _full(g.shape),
                _full(w_gate.shape), _full(w_branch.shape), _full(w_out.shape)]
    in_specs += [row(b.shape[1]) for b in branches]
    in_specs += [_full(n.shape) for n in norms]
    return pl.pallas_call(
        functools.partial(_merge_kernel, n_ctx_tiles=n_ctx // tm, tile0=t0),
        grid=(nt,),
        in_specs=in_specs,
        out_specs=pl.BlockSpec((tm, d), lambda i: (i, 0)),
        out_shape=jax.ShapeDtypeStruct((nt * tm, d), F32),
        compiler_params=_cparams(("parallel",)),
        name="merge_out",
    )(x_ctx, x_lat, mod, g, w_gate, w_branch, w_out, *branches, *norms)


def _ffn_kernel(x_ref, mod_ref, g_ref, wg_ref, wu_ref, wd_ref, fn_ref, o_ref, *, n_ctx_tiles, final):
    is_ctx = pl.program_id(0) < n_ctx_tiles
    d = x_ref.shape[1]
    x = x_ref[...]
    h = _rms_mod(x, g_ref[...], _pick_mod(mod_ref, is_ctx, 3, d), _pick_mod(mod_ref, is_ctx, 4, d)).astype(BF16)
    act = (_silu(_dot(h, wg_ref[...])) * _dot(h, wu_ref[...])).astype(BF16)
    y = x + _pick_mod(mod_ref, is_ctx, 5, d) * _dot(act, wd_ref[...])
    if final:
        y = y * lax.rsqrt(jnp.mean(y * y, axis=-1, keepdims=True) + EPS) * fn_ref[...]
    o_ref[...] = y


def _ffn_call(x_rows, mod, g, wg, wu, wd, final_g, n_ctx_rows, final):
    n, d = x_rows.shape
    tm = ROW_TILE
    row = pl.BlockSpec((tm, d), lambda i: (i, 0))
    return pl.pallas_call(
        functools.partial(_ffn_kernel, n_ctx_tiles=n_ctx_rows // tm, final=final),
        grid=(n // tm,),
        in_specs=[row, _full(mod.shape), _full(g.shape), _full(wg.shape), _full(wu.shape), _full(wd.shape), _full(final_g.shape)],
        out_specs=row,
        out_shape=jax.ShapeDtypeStruct((n, d), F32),
        compiler_params=_cparams(("parallel",)),
        name="dense_swiglu",
    )(x_rows, mod, g, wg, wu, wd, final_g)


def _router_kernel(x_ref, mod_ref, g_ref, rt_ref, tri_ref, h_o, gate_o, slot_o, cnt_o):
    d = x_ref.shape[1]
    h = _rms_mod(x_ref[...], g_ref[...], mod_ref[0:1, 3 * d:4 * d], mod_ref[0:1, 4 * d:5 * d])
    h_o[...] = h.astype(BF16)
    logits = _dot_nt(rt_ref[...], h, HI)
    e, b = logits.shape
    eid = lax.broadcasted_iota(jnp.int32, (e, b), 0)
    m1 = jnp.max(logits, axis=0, keepdims=True)
    i1 = jnp.min(jnp.where(logits == m1, eid, e), axis=0, keepdims=True)
    rest = jnp.where(eid == i1, -jnp.inf, logits)
    m2 = jnp.max(rest, axis=0, keepdims=True)
    i2 = jnp.min(jnp.where(rest == m2, eid, e), axis=0, keepdims=True)
    t = jnp.exp(m2 - m1)
    p1 = 1.0 / (1.0 + t)
    p2 = t / (1.0 + t)
    sel1 = eid == i1
    sel2 = eid == i2
    gate_o[...] = jnp.where(sel1, p1, jnp.where(sel2, p2, 0.0))
    sel = jnp.logical_or(sel1, sel2)
    rank = _dot(sel.astype(BF16), tri_ref[...])
    slot_o[...] = jnp.where(sel, rank, -1.0).astype(jnp.int32)
    cnt = jnp.sum(sel.astype(F32), axis=1, keepdims=True)
    cnt_o[...] = jnp.broadcast_to(cnt, (e, LANES))[None].astype(jnp.int32)


def _router_call(xl, mod, g, router_t):
    n, d = xl.shape
    b = MOE_BLOCK
    nblk = n // b
    e = router_t.shape[0]
    pos = np.arange(b)
    tri = jnp.asarray((pos[:, None] < pos[None, :]).astype(np.float32), dtype=BF16)
    return pl.pallas_call(
        _router_kernel,
        grid=(nblk,),
        in_specs=[pl.BlockSpec((b, d), lambda i: (i, 0)), _full(mod.shape), _full(g.shape), _full(router_t.shape), _full(tri.shape)],
        out_specs=[pl.BlockSpec((b, d), lambda i: (i, 0)), pl.BlockSpec((e, b), lambda i: (0, i)),
                   pl.BlockSpec((e, b), lambda i: (0, i)), pl.BlockSpec((1, e, LANES), lambda i: (i, 0, 0))],
        out_shape=[jax.ShapeDtypeStruct((n, d), BF16), jax.ShapeDtypeStruct((e, n), F32),
                   jax.ShapeDtypeStruct((e, n), jnp.int32), jax.ShapeDtypeStruct((nblk, e, LANES), jnp.int32)],
        compiler_params=_cparams(("parallel",)),
        name="moe_router",
    )(xl, mod, g, router_t, tri)


def _moe_kernel(nfull_ref, tail_ref, x_ref, h_ref, gate_ref, slot_ref, mod_ref, fn_ref, wg_ref, wu_ref, wd_ref, o_ref,
                hs_ref, ys_ref, *, final):
    bi, ei, fi = pl.program_id(0), pl.program_id(1), pl.program_id(2)
    n_e, n_f = pl.num_programs(1), pl.num_programs(2)
    b, d = x_ref.shape
    nfull = nfull_ref[bi * n_e + ei]
    tail = tail_ref[bi * n_e + ei]
    tail_start = pl.multiple_of(nfull * MOE_SLOTS, MOE_SLOTS)
    tail_sizes = tuple(range(MOE_TAIL, MOE_SLOTS, MOE_TAIL))

    def for_tiles(fn):
        def body(j, carry):
            fn(pl.multiple_of(j * MOE_SLOTS, MOE_SLOTS), MOE_SLOTS)
            return carry
        lax.fori_loop(0, nfull, body, 0)
        for k, size in enumerate(tail_sizes):
            pl.when(tail == k + 1)(functools.partial(fn, tail_start, size))

    def onehot(start, size):
        return slot_ref[pl.ds(ei, 1), :] == lax.broadcasted_iota(jnp.int32, (size, b), 0) + start

    @pl.when(jnp.logical_and(ei == 0, fi == 0))
    def _():
        o_ref[...] = jnp.zeros_like(o_ref)

    def gather(start, size):
        hs_ref[pl.ds(start, size), :] = _dot(onehot(start, size).astype(BF16), h_ref[...]).astype(BF16)

    pl.when(fi == 0)(functools.partial(for_tiles, gather))

    def expert(start, size):
        hs = hs_ref[pl.ds(start, size), :]
        act = (_silu(_dot(hs, wg_ref[0])) * _dot(hs, wu_ref[0])).astype(BF16)
        y = _dot(act, wd_ref[0])

        @pl.when(fi == 0)
        def _():
            ys_ref[pl.ds(start, size), :] = y

        @pl.when(fi != 0)
        def _():
            ys_ref[pl.ds(start, size), :] = ys_ref[pl.ds(start, size), :] + y

    for_tiles(expert)

    def scatter(start, size):
        oh = onehot(start, size)
        gs = jnp.sum(jnp.where(oh, gate_ref[pl.ds(ei, 1), :], 0.0), axis=1, keepdims=True)
        o_ref[...] += _dot_tn(oh.astype(BF16), (ys_ref[pl.ds(start, size), :] * gs).astype(BF16))

    pl.when(fi == n_f - 1)(functools.partial(for_tiles, scatter))

    @pl.when(jnp.logical_and(ei == n_e - 1, fi == n_f - 1))
    def _():
        y = x_ref[...] + mod_ref[0:1, 5 * d:6 * d] * o_ref[...]
        if final:
            y = y * lax.rsqrt(jnp.mean(y * y, axis=-1, keepdims=True) + EPS) * fn_ref[...]
        o_ref[...] = y


def _moe_call(xl, h2, gate_t, slot_t, counts, mod, final_g, wg, wu, wd, final):
    n, d = xl.shape
    b = MOE_BLOCK
    e, _, f = wg.shape
    fs = f // MOE_FSPLIT
    counts = counts.reshape(-1)
    tail = (counts % MOE_SLOTS + (MOE_TAIL - 1)) // MOE_TAIL
    nfull = counts // MOE_SLOTS + tail // (MOE_SLOTS // MOE_TAIL)
    tail = tail % (MOE_SLOTS // MOE_TAIL)
    once = pl.Buffered(1)
    grid_spec = pltpu.PrefetchScalarGridSpec(
        num_scalar_prefetch=2,
        grid=(n // b, e, MOE_FSPLIT),
        in_specs=[pl.BlockSpec((b, d), lambda i, j, k, *_: (i, 0), pipeline_mode=once),
                  pl.BlockSpec((b, d), lambda i, j, k, *_: (i, 0)),
                  pl.BlockSpec((e, b), lambda i, j, k, *_: (0, i)),
                  pl.BlockSpec((e, b), lambda i, j, k, *_: (0, i)),
                  pl.BlockSpec(mod.shape, lambda i, j, k, *_: (0, 0)),
                  pl.BlockSpec(final_g.shape, lambda i, j, k, *_: (0, 0)),
                  pl.BlockSpec((1, d, fs), lambda i, j, k, *_: (j, 0, k)),
                  pl.BlockSpec((1, d, fs), lambda i, j, k, *_: (j, 0, k)),
                  pl.BlockSpec((1, fs, d), lambda i, j, k, *_: (j, k, 0))],
        out_specs=pl.BlockSpec((b, d), lambda i, j, k, *_: (i, 0)),
        scratch_shapes=[pltpu.VMEM((b, d), BF16), pltpu.VMEM((b, d), F32)],
    )
    return pl.pallas_call(
        functools.partial(_moe_kernel, final=final),
        grid_spec=grid_spec,
        out_shape=jax.ShapeDtypeStruct((n, d), F32),
        compiler_params=_cparams(("parallel", "arbitrary", "arbitrary")),
        name="moe_experts",
    )(nfull, tail, xl, h2, gate_t, slot_t, mod, final_g, wg, wu, wd)


def _mix_weight(w_mix):
    d = w_mix.shape[0]
    off = np.concatenate([[0], np.cumsum(MIX_SPLITS)])
    seg = lambda k: w_mix[:, off[k]:off[k + 1]]

    def halves(t):
        t = t.reshape(d, RET_H, RET_DK // 2, 2)
        return jnp.concatenate([t[..., 0], t[..., 1]], axis=-1).reshape(d, RET_H * RET_DK)

    n_small = MIX_SPLITS[8] + MIX_SPLITS[9] + MIX_SPLITS[12]
    return jnp.concatenate([
        seg(0), seg(6), seg(11),
        seg(1),
        halves(seg(2)), halves(seg(3)), seg(4), seg(5),
        seg(7), seg(10),
        seg(8), seg(9), seg(12), jnp.zeros((d, SMALL_COLS - n_small), w_mix.dtype),
    ], axis=1).astype(BF16)


def _block_diag(w):
    n, i, o = w.shape
    eye = jnp.eye(n, dtype=w.dtype)
    return (eye[:, None, :, None] * w[:, :, None, :]).reshape(n * i, n * o)


def _rotary_tables(n_lat, n_ctx):
    assert GRID_W % SUBLANES == 0 and n_ctx % SUBLANES == 0 and ROW_TILE % GRID_W == 0
    n_freq = RET_DK // 4
    freqs = ROPE_BASE ** (-jnp.arange(n_freq, dtype=F32) / n_freq)
    lane = np.arange(RET_H * RET_DK)
    freq_lane = freqs[(lane % (RET_DK // 2)) % n_freq]
    sign = jnp.asarray(np.where(lane % RET_DK < RET_DK // 2, -1.0, 1.0).astype(np.float32))
    t8 = jnp.arange((n_ctx + n_lat) // SUBLANES) * SUBLANES - n_ctx
    row = jnp.where(t8 >= 0, t8 // GRID_W, 0).astype(F32)
    col = (jnp.arange(ROW_TILE) % GRID_W).astype(F32)

    def cs(pos):
        ang = pos[:, None] * freq_lane[None, :]
        return jnp.stack([jnp.cos(ang), sign[None, :] * jnp.sin(ang)])

    return cs(row), cs(col)


def kernel(x, c, ctx, c_ctx, w_mod, b_mod, norm_mix, norm_ffn, w_in, lru_conv_w, lru_conv_b, lru_wa, lru_ba, lru_wx, lru_bx, lru_lambda, ret_norm, gdn_conv_w, gdn_a_log, gdn_dt_bias, gdn_norm, ssd_conv_w, ssd_conv_b, ssd_a_log, ssd_dt_bias, ssd_d, ssd_norm, w_branch, w_out, ffn_wg, ffn_wu, ffn_wd, moe_router, moe_wg, moe_wu, moe_wd, final_norm):
    assert x.shape[0] == 1 and c.shape[0] == 1 and ctx.shape[0] == 1
    depth = w_mod.shape[0]
    n_lat, d = x.shape[1], x.shape[2]
    n_ctx = ctx.shape[1]
    assert n_ctx % ROW_TILE == 0 and n_lat % ROW_TILE == 0 and n_ctx % MIX_ROWS == 0 and n_lat % MIX_ROWS == 0
    gate_cols = N_BRANCH * d
    rot_rows, rot_cols = _rotary_tables(n_lat, n_ctx)
    b_mod3 = b_mod[:, None, :]
    ct = jnp.stack([c[0], c_ctx], axis=1)
    final_g = final_norm[None, :]

    x_ctx, x_lat, lat_row0 = ctx[0], x[0], 0
    for layer in range(depth):
        ctx_out = layer < depth - 1
        last = layer == depth - 1
        mod = _mod_call(ct, w_mod, b_mod3, layer)
        w_mix = _mix_weight(w_in[layer][:, gate_cols:])
        conv_w = jnp.concatenate([lru_conv_w[layer], gdn_conv_w[layer], ssd_conv_w[layer]], axis=1)
        conv_b = jnp.concatenate([lru_conv_b[layer], jnp.zeros((GDN_QKV,), F32), ssd_conv_b[layer]])[None, :]
        (lru_u, gdn_qkv, ssd_xbc, p_lg, p_ret, p_gz, p_sz, p_small) = _proj_call(
            x_ctx, x_lat, lat_row0, mod, norm_mix[layer][None, :], w_mix, rot_rows, rot_cols, conv_w, conv_b, n_ctx)
        small_t = p_small[:, 0:32].T

        lru_w = jnp.stack([jnp.concatenate([_block_diag(lru_wa[layer, dd]), _block_diag(lru_wx[layer, dd])], axis=1)
                           for dd in range(2)]).astype(BF16)
        lru_b = jnp.concatenate([lru_ba[layer], lru_bx[layer]], axis=1)[:, None, :]
        lru_f, lru_b_ = _lru_call(lru_u, lru_w, lru_b, lru_lambda[layer][:, None, :], n_ctx)
        ret_f, ret_b = _ret_call(p_ret, n_ctx)
        gdn_f, gdn_b = _gdn_call(gdn_qkv, p_small, small_t, gdn_a_log[layer], gdn_dt_bias[layer], n_ctx)
        ssd_f, ssd_b = _ssd_call(ssd_xbc, p_small, small_t, ssd_dt_bias[layer], ssd_a_log[layer], ssd_d[layer], n_ctx)

        branches = (lru_f, lru_b_, p_lg, ret_f, ret_b, p_ret, gdn_f, gdn_b, p_gz, ssd_f, ssd_b, p_sz)
        norms = (ret_norm[layer][None, :], jnp.tile(gdn_norm[layer], GDN_H)[None, :], ssd_norm[layer][None, :])
        row_start = 0 if ctx_out else n_ctx
        x_rows = _merge_call(x_ctx, x_lat, lat_row0, mod, norm_mix[layer][None, :],
                             w_in[layer][:, :gate_cols].astype(BF16), w_branch[layer].astype(BF16),
                             w_out[layer].astype(BF16), branches, norms, n_ctx, row_start)
        n_ctx_rows = n_ctx - row_start
        j = layer // 2
        if layer % 2 == 0:
            x_rows = _ffn_call(x_rows, mod, norm_ffn[layer][None, :], ffn_wg[j].astype(BF16), ffn_wu[j].astype(BF16),
                               ffn_wd[j].astype(BF16), final_g, n_ctx_rows, last)
        else:
            assert not ctx_out, "expert layers that must also emit context tokens are not supported"
            assert x_rows.shape[0] % MOE_BLOCK == 0 and MOE_BLOCK % MOE_SLOTS == 0 and MOE_SLOTS % MOE_TAIL == 0
            h2, gate_t, slot_t, cnt = _router_call(x_rows, mod, norm_ffn[layer][None, :], moe_router[j].T)
            x_rows = _moe_call(x_rows, h2, gate_t, slot_t, cnt[:, :, 0], mod, final_g, moe_wg[j].astype(BF16),
                               moe_wu[j].astype(BF16), moe_wd[j].astype(BF16), last)
        x_ctx, x_lat, lat_row0 = x_rows, x_rows, n_ctx - row_start
    out = x_rows[x_rows.shape[0] - n_lat:]
    return out[None]
```
